```python
import jax
import jax.numpy as jnp
from jax import lax
import numpy as np

D_MODEL = 2048
BATCH = 32
SEQ = 256
DEPTH = 1
DEC_BATCH = 8
DEC_SEQ = 4096
PAST_LEN = 512

GRID_W = 64
RW_HEADS = 16
RW_HEAD = 64
RW_DIM = RW_HEADS * RW_HEAD
DECAY_LORA = 64
AAA_LORA = 64
GATE_LORA = 128
RW_LN_EPS = 64e-5
HG_HEADS = 8
HG_DK = 128
HG_DV = 128
HG_KDIM = HG_HEADS * HG_DK
HG_VDIM = HG_HEADS * HG_DV
HG_CHUNK = 32
N_EXPERTS = 16
EXPERT_FF = 1024
CAPACITY_FACTOR = 2
NORM_EPS = 1e-6
N_MOD = 6
RW_COLS = 3 * RW_DIM + 2 * DECAY_LORA + 2 * AAA_LORA + GATE_LORA
HG_COLS = 3 * HG_KDIM + 2 * HG_VDIM
GATE_COLS = 2 * D_MODEL
IN_COLS = RW_COLS + HG_COLS + GATE_COLS

kernel_name = 'hybrid_rwkv7_hgrn2_ecmoe_diffusion_step'


def _rmsnorm(x, g, eps=NORM_EPS):
    x32 = x.astype(jnp.float32)
    y = x32 * lax.rsqrt(jnp.mean(x32 * x32, axis=-1, keepdims=True) + eps)
    return (y * g.astype(jnp.float32)).astype(x.dtype)


def _neighbours(u, grid):
    B, T, C = u.shape
    if grid is not None:
        u = u.reshape(B, grid[0], grid[1], C)
    axis = u.ndim - 2
    n = u.shape[axis]
    pad = [(0, 0)] * u.ndim
    pad[axis] = (1, 1)
    up = jnp.pad(u, pad)
    prev = lax.slice_in_dim(up, 0, n, axis=axis).reshape(B, T, C)
    nxt = lax.slice_in_dim(up, 2, n + 2, axis=axis).reshape(B, T, C)
    return prev, nxt


def _rwkv7_scan(s0, r, w, k, v, kk, a):
    def step(s, inp):
        r_t, w_t, k_t, v_t, kk_t, a_t = inp
        sa = jnp.einsum('bhvk,bhk->bhv', s, -kk_t)
        s = (s * w_t[:, :, None, :] + sa[..., None] * (kk_t * a_t)[:, :, None, :]
             + v_t[..., None] * k_t[:, :, None, :])
        return s, jnp.einsum('bhvk,bhk->bhv', s, r_t)
    xs = tuple(jnp.swapaxes(t, 0, 1) for t in (r, w, k, v, kk, a))
    s, y = lax.scan(step, s0, xs)
    return s, jnp.swapaxes(y, 0, 1)


def _rwkv7_branch(pr, s0_f, s0_b, grid, p):
    B, T, _ = pr.shape
    pr = pr.astype(jnp.float32)
    prev, nxt = _neighbours(pr, grid)
    mu = p['rw_mu'].astype(jnp.float32)
    pr = pr + mu[0] * (prev - pr) + mu[1] * (nxt - pr)
    o0 = 3 * RW_DIM
    cuts = [RW_DIM, 2 * RW_DIM, o0, o0 + DECAY_LORA, o0 + 2 * DECAY_LORA,
            o0 + 2 * DECAY_LORA + AAA_LORA, o0 + 2 * DECAY_LORA + 2 * AAA_LORA]
    r, k, v, wd_f, wd_b, ad_f, ad_b, gd = jnp.split(pr, cuts, axis=-1)
    heads = lambda u: u.reshape(B, T, RW_HEADS, RW_HEAD)
    kk = heads(k * p['rw_k_k'])
    kk = kk * lax.rsqrt(jnp.maximum(jnp.sum(kk * kk, axis=-1, keepdims=True), 1e-24))
    y = jnp.zeros((B, T, RW_HEADS, RW_HEAD), jnp.float32)
    bonus = jnp.zeros((B, T, RW_HEADS, RW_HEAD), jnp.float32)
    finals = []
    for d, (wd, ad, s0) in enumerate(((wd_f, ad_f, s0_f), (wd_b, ad_b, s0_b))):
        w_log = -jax.nn.softplus(-(p['rw_w0'][d] + jnp.tanh(wd) @ p['rw_w2'][d])) - 0.5
        decay = jnp.exp(-jnp.exp(w_log))
        a = jax.nn.sigmoid(p['rw_a0'][d] + ad @ p['rw_a2'][d])
        kd = k * (1.0 + (a - 1.0) * p['rw_k_a'])
        ins = (heads(r), heads(decay), heads(kd), heads(v), kk, heads(a))
        if d == 1:
            ins = tuple(jnp.flip(t, axis=1) for t in ins)
        s_fin, yd = _rwkv7_scan(s0, *ins)
        if d == 1:
            yd = jnp.flip(yd, axis=1)
        y = y + yd
        bonus = bonus + jnp.sum(heads(r) * heads(kd) * p['rw_r_k'], axis=-1, keepdims=True) * heads(v)
        finals.append(s_fin)
    mean = jnp.mean(y, axis=-1, keepdims=True)
    var = jnp.mean(jnp.square(y - mean), axis=-1, keepdims=True)
    y = ((y - mean) * lax.rsqrt(var + RW_LN_EPS)).reshape(B, T, RW_DIM) * p['rw_ln_w'] + p['rw_ln_b']
    g = jax.nn.sigmoid(gd) @ p['rw_g2']
    out = (y + bonus.reshape(B, T, RW_DIM)) * g
    return out, finals[0], finals[1]


def _gla_chunk_scan(s0, q, k, v, logf):
    B, T, H, DK = q.shape
    DV = v.shape[-1]
    n = T // HG_CHUNK
    blocks = lambda u: jnp.moveaxis(u.reshape(B, n, HG_CHUNK, H, u.shape[-1]), 1, 0)
    lower = jnp.tril(jnp.ones((HG_CHUNK, HG_CHUNK), dtype=bool))[None, :, :, None, None]

    def step(s, inp):
        qc, kc, vc, gc = inp
        b = jnp.cumsum(gc, axis=1)
        inter = jnp.einsum('blhk,bhkv->blhv', qc * jnp.exp(b), s)
        rel = jnp.exp(jnp.where(lower, b[:, :, None] - b[:, None, :], -jnp.inf))
        att = jnp.einsum('bthk,bshk,btshk->bhts', qc, kc, rel)
        intra = jnp.einsum('bhts,bshv->bthv', att, vc)
        b_last = b[:, -1]
        s = (s * jnp.exp(b_last)[..., None]
             + jnp.einsum('blhk,blhv->bhkv', kc * jnp.exp(b_last[:, None] - b), vc))
        return s, inter + intra

    s, o = lax.scan(step, s0, tuple(blocks(u) for u in (q, k, v, logf)))
    return s, jnp.moveaxis(o, 0, 1).reshape(B, T, H, DV)


def _hgrn2_branch(ph, s0_f, s0_b, lb, p):
    B, T, _ = ph.shape
    ph = ph.astype(jnp.float32)
    q, f_f, f_b, i, g = jnp.split(ph, [HG_KDIM, 2 * HG_KDIM, 3 * HG_KDIM, 3 * HG_KDIM + HG_VDIM], axis=-1)
    kh = lambda u: u.reshape(B, T, HG_HEADS, HG_DK)
    vh = lambda u: u.reshape(B, T, HG_HEADS, HG_DV)
    q = kh(jax.nn.silu(q))
    v = vh(i)
    o = jnp.zeros((B, T, HG_HEADS, HG_DV), jnp.float32)
    finals = []
    for d, (fl, s0) in enumerate(((f_f, s0_f), (f_b, s0_b))):
        f = lb[d] + (1.0 - lb[d]) * jax.nn.sigmoid(fl)
        ins = (q, kh(1.0 - f), v, kh(jnp.log(f)))
        if d == 1:
            ins = tuple(jnp.flip(t, axis=1) for t in ins)
        s_fin, od = _gla_chunk_scan(s0, *ins)
        if d == 1:
            od = jnp.flip(od, axis=1)
        o = o + od
        finals.append(s_fin)
    o = o * lax.rsqrt(jnp.mean(o * o, axis=-1, keepdims=True) + NORM_EPS)
    out = o.reshape(B, T, HG_VDIM) * p['hg_norm_g'] * jax.nn.silu(g)
    return out, finals[0], finals[1]


def _expert_choice_moe(h, w_router, w_gate, w_up, w_down):
    B, T, _ = h.shape
    cap = CAPACITY_FACTOR * T // N_EXPERTS
    aff = jax.nn.softmax(jnp.einsum('btd,de->bte', h, w_router).astype(jnp.float32), axis=-1)
    vals, idx = lax.top_k(jnp.swapaxes(aff, 1, 2), cap)
    bidx = jnp.arange(B)[:, None, None]
    xg = h[bidx, idx]
    hid = jax.nn.silu(jnp.einsum('becd,edf->becf', xg, w_gate)) * jnp.einsum('becd,edf->becf', xg, w_up)
    yo = jnp.einsum('becf,efd->becd', hid, w_down) * vals[..., None].astype(h.dtype)
    return jnp.zeros_like(h).at[bidx, idx].add(yo.astype(h.dtype))


def _layer(x, cond, s_rw, s_hg, grid, lb, p):
    mod = jax.nn.silu(cond) @ p['ada_w'] + p['ada_b']
    sh1, sc1, gt1, sh2, sc2, gt2 = jnp.split(mod[:, None, :], N_MOD, axis=-1)
    h = _rmsnorm(x, p['norm1_g']) * (1 + sc1) + sh1
    w_in = p['w_in']
    pr = h @ w_in[:, :RW_COLS]
    ph = h @ w_in[:, RW_COLS:RW_COLS + HG_COLS]
    pg = h @ w_in[:, RW_COLS + HG_COLS:]
    s_rw = s_rw.astype(jnp.float32)
    s_hg = s_hg.astype(jnp.float32)
    out_r, srf, srb = _rwkv7_branch(pr, s_rw[:, 0], s_rw[:, 1], grid, p)
    out_h, shf, shb = _hgrn2_branch(ph, s_hg[:, 0], s_hg[:, 1], lb, p)
    gate_r, gate_h = jnp.split(jax.nn.sigmoid(pg.astype(jnp.float32)), 2, axis=-1)
    merged = gate_r * (out_r @ p['w_br_rwkv']) + gate_h * (out_h @ p['w_br_hgrn'])
    x = x + gt1 * (merged.astype(x.dtype) @ p['w_out'])
    h2 = _rmsnorm(x, p['norm2_g']) * (1 + sc2) + sh2
    x = x + gt2 * _expert_choice_moe(h2, p['moe_router'], p['moe_w_gate'], p['moe_w_up'], p['moe_w_down'])
    return x, jnp.stack([srf, srb], axis=1), jnp.stack([shf, shb], axis=1)


def setup_inputs(seed: int = 0) -> dict:
    key = jax.random.key(seed)
    ks = iter(jax.random.split(key, 32))
    nrm = lambda shape, scale: jax.random.normal(next(ks), shape, jnp.float32) * scale
    uni = lambda shape, lo, hi: jax.random.uniform(next(ks), shape, jnp.float32, lo, hi)
    D = D_MODEL
    return {
        'x_prompt': nrm((BATCH, SEQ, D), 1.0),
        'x_sample': nrm((DEC_BATCH, DEC_SEQ, D), 1.0),
        'c': nrm((DEC_BATCH, D), 1.0),
        'state_rwkv': nrm((DEC_BATCH, DEPTH, 2, RW_HEADS, RW_HEAD, RW_HEAD), 0.5),
        'state_hgrn': nrm((DEC_BATCH, DEPTH, 2, HG_HEADS, HG_DK, HG_DV), 0.5),
        'c_ctx': nrm((D,), 1.0),
        'norm1_g': 1.0 + nrm((DEPTH, D), 0.02),
        'norm2_g': 1.0 + nrm((DEPTH, D), 0.02),
        'final_norm_g': 1.0 + nrm((D,), 0.02),
        'ada_w': nrm((DEPTH, D, N_MOD * D), D ** -0.5),
        'ada_b': nrm((DEPTH, N_MOD * D), 0.02),
        'w_in': nrm((DEPTH, D, IN_COLS), D ** -0.5),
        'rw_mu': uni((DEPTH, 2, RW_COLS), 0.0, 0.5),
        'rw_w0': uni((DEPTH, 2, RW_DIM), -5.0, 1.0),
        'rw_w2': nrm((DEPTH, 2, DECAY_LORA, RW_DIM), 0.5 * DECAY_LORA ** -0.5),
        'rw_a0': nrm((DEPTH, 2, RW_DIM), 0.5),
        'rw_a2': nrm((DEPTH, 2, AAA_LORA, RW_DIM), 0.5 * AAA_LORA ** -0.5),
        'rw_g2': nrm((DEPTH, GATE_LORA, RW_DIM), GATE_LORA ** -0.5),
        'rw_k_k': 0.85 + nrm((DEPTH, RW_DIM), 0.05),
        'rw_k_a': 1.0 + nrm((DEPTH, RW_DIM), 0.05),
        'rw_r_k': nrm((DEPTH, RW_HEADS, RW_HEAD), 0.1),
        'rw_ln_w': 1.0 + nrm((DEPTH, RW_DIM), 0.05),
        'rw_ln_b': nrm((DEPTH, RW_DIM), 0.02),
        'hg_lb': nrm((DEPTH + 1, 2, HG_KDIM), 1.0),
        'hg_norm_g': 1.0 + nrm((DEPTH, HG_VDIM), 0.02),
        'w_br_rwkv': nrm((DEPTH, RW_DIM, D), RW_DIM ** -0.5),
        'w_br_hgrn': nrm((DEPTH, HG_VDIM, D), HG_VDIM ** -0.5),
        'w_out': nrm((DEPTH, D, D), D ** -0.5),
        'moe_router': nrm((DEPTH, D, N_EXPERTS), D ** -0.5),
        'moe_w_gate': nrm((DEPTH, N_EXPERTS, D, EXPERT_FF), D ** -0.5),
        'moe_w_up': nrm((DEPTH, N_EXPERTS, D, EXPERT_FF), D ** -0.5),
        'moe_w_down': nrm((DEPTH, N_EXPERTS, EXPERT_FF, D), EXPERT_FF ** -0.5),
    }


def reference(x_prompt, x_sample, c, state_rwkv, state_hgrn, c_ctx, norm1_g, norm2_g, final_norm_g,
              ada_w, ada_b, w_in, rw_mu, rw_w0, rw_w2, rw_a0, rw_a2, rw_g2, rw_k_k, rw_k_a, rw_r_k,
              rw_ln_w, rw_ln_b, hg_lb, hg_norm_g, w_br_rwkv, w_br_hgrn, w_out, moe_router,
              moe_w_gate, moe_w_up, moe_w_down):
    rows = x_sample.shape[1] // GRID_W
    grid = (rows, GRID_W)
    lb_all = jnp.cumsum(jax.nn.softmax(hg_lb.astype(jnp.float32), axis=0), axis=0)
    bp = x_prompt.shape[0]
    xp, xs = x_prompt, x_sample
    new_rw, new_hg = [], []
    for l in range(DEPTH):
        p = dict(norm1_g=norm1_g[l], norm2_g=norm2_g[l], ada_w=ada_w[l], ada_b=ada_b[l], w_in=w_in[l],
                 rw_mu=rw_mu[l], rw_w0=rw_w0[l], rw_w2=rw_w2[l], rw_a0=rw_a0[l], rw_a2=rw_a2[l],
                 rw_g2=rw_g2[l], rw_k_k=rw_k_k[l], rw_k_a=rw_k_a[l], rw_r_k=rw_r_k[l],
                 rw_ln_w=rw_ln_w[l], rw_ln_b=rw_ln_b[l], hg_norm_g=hg_norm_g[l],
                 w_br_rwkv=w_br_rwkv[l], w_br_hgrn=w_br_hgrn[l], w_out=w_out[l],
                 moe_router=moe_router[l], moe_w_gate=moe_w_gate[l], moe_w_up=moe_w_up[l],
                 moe_w_down=moe_w_down[l])
        zero_rw = jnp.zeros((bp, 2, RW_HEADS, RW_HEAD, RW_HEAD), jnp.float32)
        zero_hg = jnp.zeros((bp, 2, HG_HEADS, HG_DK, HG_DV), jnp.float32)
        xp, s_rw, s_hg = _layer(xp, c_ctx[None, :], zero_rw, zero_hg, None, lb_all[l], p)
        new_rw.append(s_rw)
        new_hg.append(s_hg)
        xs, _, _ = _layer(xs, c, state_rwkv[:, l], state_hgrn[:, l], grid, lb_all[l], p)
    y_prompt = _rmsnorm(xp, final_norm_g)
    y_sample = _rmsnorm(xs, final_norm_g)
    new_state_rwkv = jnp.stack(new_rw, axis=1)
    new_state_hgrn = jnp.stack(new_hg, axis=1)
    return (y_prompt, y_sample, new_state_rwkv, new_state_hgrn)
```

```python
import functools

import jax
import jax.numpy as jnp
from jax import lax
from jax.experimental import pallas as pl
from jax.experimental.pallas import tpu as pltpu

F32 = jnp.float32
BF16 = jnp.bfloat16

D_MODEL = 2048
GRID_W = 64
RW_HEADS = 16
RW_HEAD = 64
RW_DIM = RW_HEADS * RW_HEAD
DECAY_LORA = 64
AAA_LORA = 64
GATE_LORA = 128
RW_LN_EPS = 64e-5
HG_HEADS = 8
HG_DK = 128
HG_DV = 128
HG_KDIM = HG_HEADS * HG_DK
HG_VDIM = HG_HEADS * HG_DV
N_EXPERTS = 16
CAPACITY_FACTOR = 2
NORM_EPS = 1e-6
N_MOD = 6
RW_COLS = 3 * RW_DIM + 2 * DECAY_LORA + 2 * AAA_LORA + GATE_LORA
HG_COLS = 3 * HG_KDIM + 2 * HG_VDIM

V7X_VMEM_LIMIT_BYTES = 48 * 1024 * 1024
RW_CHUNK = 64
HG_CHUNK = 64


def _mm_body(x_ref, w_ref, o_ref):
    o_ref[...] = jnp.dot(x_ref[...], w_ref[...], preferred_element_type=F32).astype(o_ref.dtype)


def _pick_tile(n, target, align):
    if n <= target:
        return n
    best = None
    t = align
    while t <= target:
        if n % t == 0:
            best = t
        t += align
    assert best is not None, (n, target, align)
    return best


def matmul(x, w, *, tm=1024, tn=1024, out_dtype=F32):
    M, K = x.shape
    K2, N = w.shape
    assert K == K2
    tm = _pick_tile(M, tm, 8)
    tn = _pick_tile(N, tn, 128)
    return pl.pallas_call(
        _mm_body,
        grid=(M // tm, N // tn),
        in_specs=[pl.BlockSpec((tm, K), lambda i, j: (i, 0)),
                  pl.BlockSpec((K, tn), lambda i, j: (0, j))],
        out_specs=pl.BlockSpec((tm, tn), lambda i, j: (i, j)),
        out_shape=jax.ShapeDtypeStruct((M, N), out_dtype),
        compiler_params=pltpu.CompilerParams(
            dimension_semantics=("parallel", "parallel"),
            vmem_limit_bytes=V7X_VMEM_LIMIT_BYTES),
    )(x, w)


def _bmm_body(x_ref, w_ref, o_ref):
    o_ref[0] = jnp.dot(x_ref[0], w_ref[0], preferred_element_type=F32).astype(o_ref.dtype)


def batched_matmul(x, w, *, tm=1024, tn=1024, out_dtype=F32):
    E, M, K = x.shape
    _, _, N = w.shape
    tm = _pick_tile(M, tm, 8)
    tn = _pick_tile(N, tn, 128)
    return pl.pallas_call(
        _bmm_body,
        grid=(E, M // tm, N // tn),
        in_specs=[pl.BlockSpec((1, tm, K), lambda e, i, j: (e, i, 0)),
                  pl.BlockSpec((1, K, tn), lambda e, i, j: (e, 0, j))],
        out_specs=pl.BlockSpec((1, tm, tn), lambda e, i, j: (e, i, j)),
        out_shape=jax.ShapeDtypeStruct((E, M, N), out_dtype),
        compiler_params=pltpu.CompilerParams(
            dimension_semantics=("parallel", "parallel", "parallel"),
            vmem_limit_bytes=V7X_VMEM_LIMIT_BYTES),
    )(x, w)


_HI = lax.Precision.HIGHEST


def _mm(x, y):
    return jnp.einsum('...ij,...jk->...ik', x, y, precision=_HI)


def _mt(x, y):
    return jnp.einsum('...ik,...jk->...ij', x, y, precision=_HI)


def _tm(x, y):
    return jnp.einsum('...li,...lj->...ij', x, y, precision=_HI)


def _rmsnorm(x, g, eps=NORM_EPS):
    return x * lax.rsqrt(jnp.mean(x * x, axis=-1, keepdims=True) + eps) * g


def _neighbours(u, width):
    B, T, C = u.shape
    u4 = u.reshape(B, T // width, width, C)
    z = jnp.zeros((B, T // width, 1, C), u.dtype)
    prev = jnp.concatenate([z, u4[:, :, :-1]], axis=2).reshape(B, T, C)
    nxt = jnp.concatenate([u4[:, :, 1:], z], axis=2).reshape(B, T, C)
    return prev, nxt


def _rwkv7_chunked(s0, r, w_log, k, v, kk, a, L):
    B, T, H, N = r.shape
    n = T // L
    f = lambda u: jnp.moveaxis(u.reshape(B, n, L, H, N), 3, 1)
    q, lw, k, v, kk, a = map(f, (r, w_log, k, v, kk, a))
    alpha = -kk
    beta = kk * a
    c = jnp.cumsum(lw, axis=3)
    g_inc = jnp.exp(c)
    g_exc = jnp.exp(c - lw)
    g_inv = jnp.exp(-c)
    g_last = jnp.exp(c[:, :, :, -1:, :])
    ab = alpha * g_exc
    qb = q * g_inc
    bt = beta * g_inv
    kt = k * g_inv
    idx = jnp.arange(L)
    strict = idx[:, None] > idx[None, :]
    incl = idx[:, None] >= idx[None, :]
    A_ab = jnp.where(strict, _mt(ab, bt), 0.0)
    A_ak = jnp.where(strict, _mt(ab, kt), 0.0)
    A_qb = jnp.where(incl, _mt(qb, bt), 0.0)
    A_qk = jnp.where(incl, _mt(qb, kt), 0.0)
    P = jnp.eye(L, dtype=F32) + A_ab
    Np = A_ab
    m = 1
    while 2 * m < L:
        Np = _mm(Np, Np)
        P = P + _mm(P, Np)
        m *= 2
    Wu = _mm(P, ab)
    U0 = _mm(P, _mm(A_ak, v))
    Qeff = qb + _mm(A_qb, Wu)
    Y0 = _mm(A_qb, U0) + _mm(A_qk, v)
    bh = bt * g_last
    kh = kt * g_last
    M = g_last[..., 0, :, None] * jnp.eye(N, dtype=F32) + _tm(bh, Wu)
    Nn = _tm(bh, U0) + _tm(kh, v)

    def step(Hs, inp):
        Mc, Nc = inp
        return _mm(Mc, Hs) + Nc, Hs

    H0 = jnp.swapaxes(s0, -1, -2)
    Hfin, Hstarts = lax.scan(step, H0, (jnp.moveaxis(M, 2, 0), jnp.moveaxis(Nn, 2, 0)))
    Hstarts = jnp.moveaxis(Hstarts, 0, 2)
    y = _mm(Qeff, Hstarts) + Y0
    y = jnp.moveaxis(y, 1, 3).reshape(B, T, H, N)
    return jnp.swapaxes(Hfin, -1, -2), y


def _gla_chunked(s0, q, k, v, logf, L):
    B, T, H, DK = q.shape
    DV = v.shape[-1]
    n = T // L
    f = lambda u: jnp.moveaxis(u.reshape(B, n, L, H, u.shape[-1]), 3, 1)
    q, k, v, g = map(f, (q, k, v, logf))
    b = jnp.cumsum(g, axis=3)
    idx = jnp.arange(L)
    att = jnp.sum(q * k, axis=-1)[..., :, None] * jnp.eye(L, dtype=F32)
    size = 2
    while size <= L:
        half = size // 2
        blk = idx // size
        upper = ((idx % size) >= half)[:, None]
        bref = b[:, :, :, blk * size + half - 1, :]
        qh = jnp.where(upper, q * jnp.exp(jnp.where(upper, b - bref, 0.0)), 0.0)
        kh = jnp.where(upper, 0.0, k * jnp.exp(jnp.where(upper, 0.0, bref - b)))
        same = blk[:, None] == blk[None, :]
        att = att + jnp.where(same, _mt(qh, kh), 0.0)
        size *= 2
    intra = _mm(att, v)
    qin = q * jnp.exp(b)
    blast = b[:, :, :, -1:, :]
    kst = k * jnp.exp(blast - b)
    KV = _tm(kst, v)
    dec = jnp.exp(blast[:, :, :, 0, :])

    def step(S, inp):
        d, kv = inp
        return S * d[..., None] + kv, S

    Sfin, Sstarts = lax.scan(step, s0, (jnp.moveaxis(dec, 2, 0), jnp.moveaxis(KV, 2, 0)))
    Sstarts = jnp.moveaxis(Sstarts, 0, 2)
    o = _mm(qin, Sstarts) + intra
    o = jnp.moveaxis(o, 1, 3).reshape(B, T, H, DV)
    return Sfin, o


def _rwkv7_branch(pr, s0_f, s0_b, width, p):
    B, T, _ = pr.shape
    prev, nxt = _neighbours(pr, width)
    mu = p['rw_mu']
    pr = pr + mu[0] * (prev - pr) + mu[1] * (nxt - pr)
    o0 = 3 * RW_DIM
    cuts = [RW_DIM, 2 * RW_DIM, o0, o0 + DECAY_LORA, o0 + 2 * DECAY_LORA,
            o0 + 2 * DECAY_LORA + AAA_LORA, o0 + 2 * DECAY_LORA + 2 * AAA_LORA]
    r, k, v, wd_f, wd_b, ad_f, ad_b, gd = jnp.split(pr, cuts, axis=-1)
    heads = lambda u: u.reshape(B, T, RW_HEADS, RW_HEAD)
    kk = heads(k * p['rw_k_k'])
    kk = kk * lax.rsqrt(jnp.maximum(jnp.sum(kk * kk, axis=-1, keepdims=True), 1e-24))
    y = jnp.zeros((B, T, RW_HEADS, RW_HEAD), F32)
    bonus = jnp.zeros((B, T, RW_HEADS, RW_HEAD), F32)
    finals = []
    for d, (wd, ad, s0) in enumerate(((wd_f, ad_f, s0_f), (wd_b, ad_b, s0_b))):
        lora_w = matmul(jnp.tanh(wd).reshape(B * T, DECAY_LORA).astype(BF16), p['rw_w2'][d].astype(BF16)).reshape(B, T, RW_DIM)
        w_log = -jax.nn.softplus(-(p['rw_w0'][d] + lora_w)) - 0.5
        log_decay = -jnp.exp(w_log)
        lora_a = matmul(ad.reshape(B * T, AAA_LORA).astype(BF16), p['rw_a2'][d].astype(BF16)).reshape(B, T, RW_DIM)
        a = jax.nn.sigmoid(p['rw_a0'][d] + lora_a)
        kd = k * (1.0 + (a - 1.0) * p['rw_k_a'])
        ins = (heads(r), heads(log_decay), heads(kd), heads(v), kk, heads(a))
        if d == 1:
            ins = tuple(jnp.flip(t, axis=1) for t in ins)
        s_fin, yd = _rwkv7_chunked(s0, *ins, RW_CHUNK)
        if d == 1:
            yd = jnp.flip(yd, axis=1)
        y = y + yd
        bonus = bonus + jnp.sum(heads(r) * heads(kd) * p['rw_r_k'], axis=-1, keepdims=True) * heads(v)
        finals.append(s_fin)
    mean = jnp.mean(y, axis=-1, keepdims=True)
    var = jnp.mean(jnp.square(y - mean), axis=-1, keepdims=True)
    y = ((y - mean) * lax.rsqrt(var + RW_LN_EPS)).reshape(B, T, RW_DIM) * p['rw_ln_w'] + p['rw_ln_b']
    g = matmul(jax.nn.sigmoid(gd).reshape(B * T, GATE_LORA).astype(BF16), p['rw_g2'].astype(BF16)).reshape(B, T, RW_DIM)
    out = (y + bonus.reshape(B, T, RW_DIM)) * g
    return out, finals[0], finals[1]


def _hgrn2_branch(ph, s0_f, s0_b, lb, p):
    B, T, _ = ph.shape
    q, f_f, f_b, i, g = jnp.split(ph, [HG_KDIM, 2 * HG_KDIM, 3 * HG_KDIM, 3 * HG_KDIM + HG_VDIM], axis=-1)
    kh = lambda u: u.reshape(B, T, HG_HEADS, HG_DK)
    vh = lambda u: u.reshape(B, T, HG_HEADS, HG_DV)
    q = kh(jax.nn.silu(q))
    v = vh(i)
    o = jnp.zeros((B, T, HG_HEADS, HG_DV), F32)
    finals = []
    for d, (fl, s0) in enumerate(((f_f, s0_f), (f_b, s0_b))):
        f = lb[d] + (1.0 - lb[d]) * jax.nn.sigmoid(fl)
        ins = (q, kh(1.0 - f), v, kh(jnp.log(f)))
        if d == 1:
            ins = tuple(jnp.flip(t, axis=1) for t in ins)
        s_fin, od = _gla_chunked(s0, *ins, HG_CHUNK)
        if d == 1:
            od = jnp.flip(od, axis=1)
        o = o + od
        finals.append(s_fin)
    o = o * lax.rsqrt(jnp.mean(o * o, axis=-1, keepdims=True) + NORM_EPS)
    out = o.reshape(B, T, HG_VDIM) * p['hg_norm_g'] * jax.nn.silu(g)
    return out, finals[0], finals[1]


def _expert_choice_moe(h, p):
    B, T, D = h.shape
    cap = CAPACITY_FACTOR * T // N_EXPERTS
    w_r = jnp.pad(p['moe_router'], ((0, 0), (0, 128 - N_EXPERTS))).astype(BF16)
    logits = matmul(h.reshape(B * T, D).astype(BF16), w_r)[:, :N_EXPERTS].reshape(B, T, N_EXPERTS)
    aff = jax.nn.softmax(logits, axis=-1)
    vals, idx = lax.top_k(jnp.swapaxes(aff, 1, 2), cap)
    bidx = jnp.arange(B)[:, None, None]
    xg = h[bidx, idx].astype(BF16)
    xe = jnp.swapaxes(xg, 0, 1).reshape(N_EXPERTS, B * cap, D)
    gate = batched_matmul(xe, p['moe_w_gate'])
    up = batched_matmul(xe, p['moe_w_up'])
    hid = (jax.nn.silu(gate) * up).astype(BF16)
    yo = batched_matmul(hid, p['moe_w_down'])
    yo = jnp.swapaxes(yo.reshape(N_EXPERTS, B, cap, D), 0, 1) * vals[..., None]
    return jnp.zeros_like(h).at[bidx, idx].add(yo)


def _layer(x, mod, s_rw, s_hg, width, lb, p):
    B, T, D = x.shape
    sh1, sc1, gt1, sh2, sc2, gt2 = jnp.split(mod[:, None, :], N_MOD, axis=-1)
    h = (_rmsnorm(x, p['norm1_g']) * (1 + sc1) + sh1).astype(BF16).reshape(B * T, D)
    pr = matmul(h, p['w_in_r']).reshape(B, T, RW_COLS)
    ph = matmul(h, p['w_in_h']).reshape(B, T, HG_COLS)
    pg = matmul(h, p['w_in_g']).reshape(B, T, 2 * D)
    out_r, srf, srb = _rwkv7_branch(pr, s_rw[:, 0], s_rw[:, 1], width, p)
    out_h, shf, shb = _hgrn2_branch(ph, s_hg[:, 0], s_hg[:, 1], lb, p)
    gate_r, gate_h = jnp.split(jax.nn.sigmoid(pg), 2, axis=-1)
    br = matmul(out_r.reshape(B * T, RW_DIM).astype(BF16), p['w_br_rwkv']).reshape(B, T, D)
    bh = matmul(out_h.reshape(B * T, HG_VDIM).astype(BF16), p['w_br_hgrn']).reshape(B, T, D)
    merged = gate_r * br + gate_h * bh
    x = x + gt1 * matmul(merged.reshape(B * T, D).astype(BF16), p['w_out']).reshape(B, T, D)
    h2 = _rmsnorm(x, p['norm2_g']) * (1 + sc2) + sh2
    x = x + gt2 * _expert_choice_moe(h2, p)
    return x, jnp.stack([srf, srb], axis=1), jnp.stack([shf, shb], axis=1)


def kernel(x_prompt, x_sample, c, state_rwkv, state_hgrn, c_ctx, norm1_g, norm2_g, final_norm_g, ada_w, ada_b, w_in, rw_mu, rw_w0, rw_w2, rw_a0, rw_a2, rw_g2, rw_k_k, rw_k_a, rw_r_k, rw_ln_w, rw_ln_b, hg_lb, hg_norm_g, w_br_rwkv, w_br_hgrn, w_out, moe_router, moe_w_gate, moe_w_up, moe_w_down):
    depth = norm1_g.shape[0]
    lb_all = jnp.cumsum(jax.nn.softmax(hg_lb.astype(F32), axis=0), axis=0)
    bp = x_prompt.shape[0]
    bs = x_sample.shape[0]
    xp, xs = x_prompt, x_sample
    new_rw, new_hg = [], []
    for l in range(depth):
        w_in_l = w_in[l].astype(BF16)
        p = dict(norm1_g=norm1_g[l], norm2_g=norm2_g[l],
                 w_in_r=w_in_l[:, :RW_COLS], w_in_h=w_in_l[:, RW_COLS:RW_COLS + HG_COLS],
                 w_in_g=w_in_l[:, RW_COLS + HG_COLS:],
                 rw_mu=rw_mu[l], rw_w0=rw_w0[l], rw_w2=rw_w2[l], rw_a0=rw_a0[l], rw_a2=rw_a2[l],
                 rw_g2=rw_g2[l], rw_k_k=rw_k_k[l], rw_k_a=rw_k_a[l], rw_r_k=rw_r_k[l],
                 rw_ln_w=rw_ln_w[l], rw_ln_b=rw_ln_b[l], hg_norm_g=hg_norm_g[l],
                 w_br_rwkv=w_br_rwkv[l].astype(BF16), w_br_hgrn=w_br_hgrn[l].astype(BF16),
                 w_out=w_out[l].astype(BF16), moe_router=moe_router[l],
                 moe_w_gate=moe_w_gate[l].astype(BF16), moe_w_up=moe_w_up[l].astype(BF16),
                 moe_w_down=moe_w_down[l].astype(BF16))
        cond = jnp.concatenate([c_ctx[None, :], c], axis=0)
        cond = jnp.pad(jax.nn.silu(cond), ((0, 16 - 1 - bs), (0, 0))).astype(BF16)
        mod = matmul(cond, ada_w[l].astype(BF16))[:1 + bs] + ada_b[l]
        zero_rw = jnp.zeros((bp, 2, RW_HEADS, RW_HEAD, RW_HEAD), F32)
        zero_hg = jnp.zeros((bp, 2, HG_HEADS, HG_DK, HG_DV), F32)
        xp, s_rw, s_hg = _layer(xp, mod[:1], zero_rw, zero_hg, xp.shape[1], lb_all[l], p)
        new_rw.append(s_rw)
        new_hg.append(s_hg)
        xs, _, _ = _layer(xs, mod[1:], state_rwkv[:, l].astype(F32), state_hgrn[:, l].astype(F32),
                          GRID_W, lb_all[l], p)
    y_prompt = _rmsnorm(xp, final_norm_g)
    y_sample = _rmsnorm(xs, final_norm_g)
    return (y_prompt, y_sample, jnp.stack(new_rw, axis=1), jnp.stack(new_hg, axis=1))
```

```python
import functools

import jax
import jax.numpy as jnp
from jax import lax
from jax.experimental import pallas as pl
from jax.experimental.pallas import tpu as pltpu

F32 = jnp.float32
BF16 = jnp.bfloat16

D_MODEL = 2048
GRID_W = 64
RW_HEADS = 16
RW_HEAD = 64
RW_DIM = RW_HEADS * RW_HEAD
DECAY_LORA = 64
AAA_LORA = 64
GATE_LORA = 128
RW_LORA_COLS = 2 * DECAY_LORA + 2 * AAA_LORA + GATE_LORA
RW_LN_EPS = 64e-5
HG_HEADS = 8
HG_DK = 128
HG_DV = 128
HG_KDIM = HG_HEADS * HG_DK
HG_VDIM = HG_HEADS * HG_DV
N_EXPERTS = 16
EXPERT_FF = 1024
CAPACITY_FACTOR = 2
NORM_EPS = 1e-6
N_MOD = 6
RW_COLS = 3 * RW_DIM + RW_LORA_COLS
HG_COLS = 3 * HG_KDIM + 2 * HG_VDIM
GATE_COLS = 2 * D_MODEL
IN_COLS = RW_COLS + HG_COLS + GATE_COLS

LANES = 128
V7X_VMEM_LIMIT_BYTES = 52 * 1024 * 1024

PROJ_TN = 1152
PROJ_PLAIN_TILES = (GATE_COLS + HG_COLS) // PROJ_TN
COL_GATE_R, COL_GATE_H = 0, 1
COL_HG_Q, COL_HG_FF, COL_HG_FB, COL_HG_I, COL_HG_G = 4, 5, 6, 7, 8
COL_RW_R, COL_RW_K, COL_RW_V = 9, 10, 11
COL_RW_LORA = (GATE_COLS + HG_COLS + 3 * RW_DIM) // RW_LORA_COLS

RW_CHUNK = 64
RW_PAIRS = RW_DIM // LANES
HG_L = 128
DECAY_SCALE = 0.6065306597126334


def _params(sem):
    return pltpu.CompilerParams(dimension_semantics=sem, vmem_limit_bytes=V7X_VMEM_LIMIT_BYTES)


def _dot(a, b):
    return jnp.dot(a.astype(BF16), b.astype(BF16), preferred_element_type=F32)


def _dot_nt(a, b):
    return lax.dot_general(a.astype(BF16), b.astype(BF16), (((1,), (1,)), ((), ())),
                           preferred_element_type=F32)


def _split_dot(x, w_bf16):
    hi = x.astype(BF16)
    lo = (x - hi.astype(F32)).astype(BF16)
    return (jnp.dot(hi, w_bf16, preferred_element_type=F32)
            + jnp.dot(lo, w_bf16, preferred_element_type=F32))


def _group_sums(x, ones_bd):
    parts = [_split_dot(x[:, p * LANES:(p + 1) * LANES], ones_bd) for p in range(x.shape[1] // LANES)]
    return jnp.concatenate(parts, axis=1)


def _ones_blocks(group):
    ri = lax.broadcasted_iota(jnp.int32, (LANES, LANES), 0)
    ci = lax.broadcasted_iota(jnp.int32, (LANES, LANES), 1)
    return jnp.where((ri // group) == (ci // group), 1.0, 0.0).astype(BF16)


def _sigmoid(x):
    return 1.0 / (1.0 + jnp.exp(-x))


def _cumsum_rows(x, reverse):
    L = x.shape[0]
    row = lax.broadcasted_iota(jnp.int32, x.shape, 0)
    s = 1
    while s < L:
        if reverse:
            x = x + jnp.where(row < L - s, pltpu.roll(x, L - s, 0), 0.0)
        else:
            x = x + jnp.where(row >= s, pltpu.roll(x, s, 0), 0.0)
        s *= 2
    return x


def _interleave(gens):
    results = [None] * len(gens)
    active = list(enumerate(gens))
    while active:
        still = []
        for idx, g in active:
            try:
                next(g)
                still.append((idx, g))
            except StopIteration as e:
                results[idx] = e.value
        active = still
    return results


def _mm_body(x_ref, w_ref, o_ref):
    o_ref[...] = _dot(x_ref[...], w_ref[...]).astype(o_ref.dtype)


def matmul(x, w, *, tm, tn, out_dtype=F32):
    M, K = x.shape
    _, N = w.shape
    return pl.pallas_call(
        _mm_body,
        grid=(M // tm, N // tn),
        in_specs=[pl.BlockSpec((tm, K), lambda i, j: (i, 0)),
                  pl.BlockSpec((K, tn), lambda i, j: (0, j))],
        out_specs=pl.BlockSpec((tm, tn), lambda i, j: (i, j)),
        out_shape=jax.ShapeDtypeStruct((M, N), out_dtype),
        compiler_params=_params(("parallel", "parallel")),
        name="matmul",
    )(x, w)


def _in_proj_body(x_ref, mod_ref, g_ref, w_ref, mu_ref, o_ref, h_scr, *, width):
    @pl.when(pl.program_id(1) == 0)
    def _():
        x = x_ref[...]
        y = x * lax.rsqrt(jnp.mean(x * x, axis=-1, keepdims=True) + NORM_EPS) * g_ref[...]
        m = mod_ref[0]
        h_scr[...] = (y * (1.0 + m[:, D_MODEL:2 * D_MODEL]) + m[:, 0:D_MODEL]).astype(BF16)

    acc = jnp.dot(h_scr[...], w_ref[...], preferred_element_type=F32)
    tm = acc.shape[0]
    pos = lax.broadcasted_iota(jnp.int32, acc.shape, 0) & (width - 1)
    prev = jnp.where(pos == 0, 0.0, pltpu.roll(acc, 1, 0))
    nxt = jnp.where(pos == width - 1, 0.0, pltpu.roll(acc, tm - 1, 0))
    mu = mu_ref[...]
    o_ref[...] = acc + mu[0:1, :] * (prev - acc) + mu[1:2, :] * (nxt - acc)


def in_proj(x, mod, norm_g, w_perm, mu_perm, *, width, rows_per_mod, tm=512):
    M, D = x.shape
    assert tm % width == 0 and rows_per_mod % tm == 0
    per = rows_per_mod // tm
    return pl.pallas_call(
        functools.partial(_in_proj_body, width=width),
        grid=(M // tm, IN_COLS // PROJ_TN),
        in_specs=[pl.BlockSpec((tm, D), lambda i, j: (i, 0)),
                  pl.BlockSpec((1, 1, N_MOD * D), lambda i, j: (i // per, 0, 0)),
                  pl.BlockSpec((1, D), lambda i, j: (0, 0)),
                  pl.BlockSpec((D, PROJ_TN), lambda i, j: (0, j)),
                  pl.BlockSpec((2, PROJ_TN), lambda i, j: (0, j))],
        out_specs=pl.BlockSpec((tm, PROJ_TN), lambda i, j: (i, j)),
        out_shape=jax.ShapeDtypeStruct((M, IN_COLS), F32),
        scratch_shapes=[pltpu.VMEM((tm, D), BF16)],
        compiler_params=_params(("parallel", "arbitrary")),
        name="in_proj",
    )(x, mod, norm_g, w_perm, mu_perm)


def _block_diag(x, lane_lo):
    return jnp.concatenate([jnp.where(lane_lo, x, 0.0), jnp.where(lane_lo, 0.0, x)], axis=0)


def _rwkv_pair_chunk(q, lw, kd, v, kk, a, H, reverse):
    L = q.shape[0]
    lane = lax.broadcasted_iota(jnp.int32, (L, LANES), 1)
    lane_lo = lane < RW_HEAD
    c = _cumsum_rows(lw, reverse)
    c_last = c[0:1, :] if reverse else c[L - 1:L, :]
    g_inc = jnp.exp(c)
    g_exc = jnp.exp(c - lw)
    g_inv = jnp.exp(-c)
    g_rem = jnp.exp(c_last - c)
    beta = kk * a
    ab = _block_diag(-kk * g_exc, lane_lo)
    qb = _block_diag(q * g_inc, lane_lo)
    bt = _block_diag(beta * g_inv, lane_lo)
    kt = _block_diag(kd * g_inv, lane_lo)
    vb = _block_diag(v, lane_lo)
    bh = _block_diag(beta * g_rem, lane_lo)
    kh = _block_diag(kd * g_rem, lane_lo)

    n2 = 2 * L
    ri = lax.broadcasted_iota(jnp.int32, (n2, n2), 0)
    ci = lax.broadcasted_iota(jnp.int32, (n2, n2), 1)
    rt = ri % L
    ct = ci % L
    if reverse:
        strict = rt < ct
        incl = rt <= ct
    else:
        strict = rt > ct
        incl = rt >= ct

    aa = _dot_nt(jnp.concatenate([ab, qb], axis=0), jnp.concatenate([bt, kt], axis=0))
    yield
    a_ab = jnp.where(strict, aa[:n2, :n2], 0.0)
    a_ak = jnp.where(strict, aa[:n2, n2:], 0.0)
    a_qb = jnp.where(incl, aa[n2:, :n2], 0.0)
    a_qk = jnp.where(incl, aa[n2:, n2:], 0.0)

    rb = rt // 2
    cb = ct // 2
    P = jnp.where(ri == ci, 1.0, 0.0) + jnp.where(rb == cb, a_ab, 0.0)
    av = _dot(a_ak, vb)
    s = 2
    while s < L:
        rs, cs = rb, cb
        rb = rb // 2
        cb = cb // 2
        coupling = jnp.where(jnp.logical_and(rb == cb, rs != cs), a_ab, 0.0)
        pc = _dot(P, coupling)
        yield
        P = P + _dot(pc, P)
        yield
        s *= 2

    wu_u0 = _dot(P, jnp.concatenate([ab, av], axis=1))
    yield
    rhs = jnp.concatenate([wu_u0, jnp.concatenate([jnp.zeros_like(vb), vb], axis=1)], axis=0)
    qy = _dot(jnp.concatenate([a_qb, a_qk], axis=1), rhs)
    st = _dot(jnp.concatenate([bh, kh], axis=0).T, rhs)
    yield
    q_eff = qb + qy[:, :LANES]
    y0 = qy[:, LANES:]
    g_last = jnp.exp(jnp.broadcast_to(c_last, (n2, LANES)))
    m_mat = jnp.where(ri == ci, g_last, 0.0) + st[:, :LANES]
    n_mat = st[:, LANES:]
    y = _dot(q_eff, H) + y0
    h_new = _dot(m_mat, H) + n_mat
    yield
    return y[:L] + y[L:], h_new


def _rwkv_prep(r, k, v, lora, vec, w2pad, a2pad, ones_bd, d, want_other_kd):
    lora_in_w = lora[:, 0:LANES]
    lora_in_a = lora[:, LANES:2 * LANES]
    kk = k * vec[4:5, :]
    ss = _group_sums(kk * kk, ones_bd)
    kk = kk * lax.rsqrt(jnp.maximum(ss, 1e-24))
    u = vec[d:d + 1, :] + jnp.dot(jnp.tanh(lora_in_w).astype(BF16), w2pad[d], preferred_element_type=F32)
    lw = -DECAY_SCALE * _sigmoid(u)
    a = _sigmoid(vec[2 + d:3 + d, :] + jnp.dot(lora_in_a.astype(BF16), a2pad[d], preferred_element_type=F32))
    kd = k * (1.0 + (a - 1.0) * vec[5:6, :])
    out = dict(r=r, k=k, v=v, kk=kk, lw=lw, a=a, kd=kd)
    if want_other_kd:
        o = 1 - d
        a_o = _sigmoid(vec[2 + o:3 + o, :] + jnp.dot(lora_in_a.astype(BF16), a2pad[o], preferred_element_type=F32))
        out['kd_other'] = k * (1.0 + (a_o - 1.0) * vec[5:6, :])
    return out


def _rwkv_scan_body(rf_ref, kf_ref, vf_ref, lf_ref, rb_ref, kb_ref, vb_ref, lb_ref,
                    vec_ref, w2_ref, a2_ref, g2_ref, h0_ref,
                    yf_ref, yb_ref, bonus_ref, g_ref, hfin_ref, h_scr):
    i = pl.program_id(1)
    n = pl.num_programs(1)

    @pl.when(i == 0)
    def _():
        h_scr[...] = h0_ref[0]

    vec = vec_ref[...]
    ones_bd = _ones_blocks(RW_HEAD)
    fw = _rwkv_prep(rf_ref[0], kf_ref[0], vf_ref[0], lf_ref[0], vec, w2_ref, a2_ref, ones_bd, 0, True)
    bw = _rwkv_prep(rb_ref[0], kb_ref[0], vb_ref[0], lb_ref[0], vec, w2_ref, a2_ref, ones_bd, 1, False)

    rk = fw['r'] * vec[6:7, :]
    bonus_ref[0] = _group_sums(rk * (fw['kd'] + fw['kd_other']), ones_bd) * fw['v']
    gd = lf_ref[0][:, 2 * LANES:3 * LANES]
    g_ref[0] = jnp.dot(_sigmoid(gd).astype(BF16), g2_ref[...], preferred_element_type=F32)

    chains = []
    for d, t in enumerate((fw, bw)):
        for p in range(RW_PAIRS):
            sl = slice(p * LANES, (p + 1) * LANES)
            chains.append(_rwkv_pair_chunk(t['r'][:, sl], t['lw'][:, sl], t['kd'][:, sl], t['v'][:, sl],
                                           t['kk'][:, sl], t['a'][:, sl], h_scr[d, p], d == 1))
    outs = _interleave(chains)
    for d, y_ref in enumerate((yf_ref, yb_ref)):
        for p in range(RW_PAIRS):
            y, h_new = outs[d * RW_PAIRS + p]
            y_ref[0, :, p * LANES:(p + 1) * LANES] = y
            h_scr[d, p] = h_new

    @pl.when(i == n - 1)
    def _():
        hfin_ref[0] = h_scr[...]


def rwkv_scan(proj, vec, w2pad, a2pad, g2, h0):
    B, T, _ = proj.shape
    n = T // RW_CHUNK
    fwd = lambda c: (lambda b, i: (b, i, c))
    bwd = lambda c: (lambda b, i: (b, n - 1 - i, c))
    wide = lambda fn: pl.BlockSpec((1, RW_CHUNK, RW_DIM), fn)
    lora = lambda fn: pl.BlockSpec((1, RW_CHUNK, RW_LORA_COLS), fn)
    const = lambda shape: pl.BlockSpec(shape, lambda b, i: (0,) * len(shape))
    st_spec = pl.BlockSpec((1, 2, RW_PAIRS, LANES, LANES), lambda b, i: (b, 0, 0, 0, 0))
    out_sd = jax.ShapeDtypeStruct((B, T, RW_DIM), F32)
    return pl.pallas_call(
        _rwkv_scan_body,
        grid=(B, n),
        in_specs=[wide(fwd(COL_RW_R)), wide(fwd(COL_RW_K)), wide(fwd(COL_RW_V)), lora(fwd(COL_RW_LORA)),
                  wide(bwd(COL_RW_R)), wide(bwd(COL_RW_K)), wide(bwd(COL_RW_V)), lora(bwd(COL_RW_LORA)),
                  const((8, RW_DIM)), const((2, LANES, RW_DIM)), const((2, LANES, RW_DIM)),
                  const((LANES, RW_DIM)), st_spec],
        out_specs=[wide(fwd(0)), wide(bwd(0)), wide(fwd(0)), wide(fwd(0)), st_spec],
        out_shape=[out_sd, out_sd, out_sd, out_sd,
                   jax.ShapeDtypeStruct((B, 2, RW_PAIRS, LANES, LANES), F32)],
        scratch_shapes=[pltpu.VMEM((2, RW_PAIRS, LANES, LANES), F32)],
        compiler_params=_params(("parallel", "arbitrary")),
        name="rwkv_scan",
    )(proj, proj, proj, proj, proj, proj, proj, proj, vec, w2pad, a2pad, g2, h0)


def rwkv_pack_params(p):
    z = jnp.zeros((RW_HEAD, RW_DIM), F32)
    w2pad = jnp.stack([jnp.concatenate([p['rw_w2'][0], z], 0), jnp.concatenate([z, p['rw_w2'][1]], 0)]).astype(BF16)
    a2pad = jnp.stack([jnp.concatenate([p['rw_a2'][0], z], 0), jnp.concatenate([z, p['rw_a2'][1]], 0)]).astype(BF16)
    vec = jnp.stack([p['rw_w0'][0], p['rw_w0'][1], p['rw_a0'][0], p['rw_a0'][1], p['rw_k_k'], p['rw_k_a'],
                     p['rw_r_k'].reshape(RW_DIM), jnp.zeros((RW_DIM,), F32)])
    return vec, w2pad, a2pad, p['rw_g2'].astype(BF16)


def rwkv_state_to_blockdiag(s):
    B = s.shape[0]
    h = jnp.swapaxes(s, -1, -2).reshape(B, 2, RW_PAIRS, 2, RW_HEAD, RW_HEAD)
    z = jnp.zeros_like(h[:, :, :, 0])
    top = jnp.concatenate([h[:, :, :, 0], z], axis=-1)
    bot = jnp.concatenate([z, h[:, :, :, 1]], axis=-1)
    return jnp.concatenate([top, bot], axis=-2)


def rwkv_blockdiag_to_state(hb):
    B = hb.shape[0]
    h0 = hb[:, :, :, :RW_HEAD, :RW_HEAD]
    h1 = hb[:, :, :, RW_HEAD:, RW_HEAD:]
    h = jnp.stack([h0, h1], axis=3).reshape(B, 2, RW_HEADS, RW_HEAD, RW_HEAD)
    return jnp.swapaxes(h, -1, -2)


def _gla_head_chunk(q_raw, fl, v, lb, St, reverse):
    L = q_raw.shape[0]
    q = q_raw * _sigmoid(q_raw)
    f = lb + (1.0 - lb) * _sigmoid(fl)
    k = 1.0 - f
    g = jnp.log(f)
    b = _cumsum_rows(g, reverse)
    b_last = b[0:1, :] if reverse else b[L - 1:L, :]
    row = lax.broadcasted_iota(jnp.int32, (L, LANES), 0)
    ri = lax.broadcasted_iota(jnp.int32, (L, L), 0)
    ci = lax.broadcasted_iota(jnp.int32, (L, L), 1)

    prods = [(_dot_nt(q, k), ri == ci)]
    fk = b
    half = 1
    while half < L:
        size = 2 * half
        upper = (row & half) != 0
        if reverse:
            bref = jnp.where(upper, fk, pltpu.roll(fk, L - half, 0))
            is_q = jnp.logical_not(upper)
        else:
            bref = jnp.where(upper, pltpu.roll(fk, half, 0), fk)
            is_q = upper
        e = jnp.exp(jnp.where(is_q, b - bref, bref - b))
        qh = jnp.where(is_q, q * e, 0.0)
        kh = jnp.where(is_q, 0.0, k * e)
        mask = None if size == L else (ri // size) == (ci // size)
        prods.append((_dot_nt(qh, kh), mask))
        if reverse:
            fk = jnp.where(upper, pltpu.roll(fk, half, 0), fk)
        else:
            fk = jnp.where(upper, fk, pltpu.roll(fk, L - half, 0))
        half = size
    yield
    att = None
    for pr, mask in prods:
        term = pr if mask is None else jnp.where(mask, pr, 0.0)
        att = term if att is None else att + term
    o = _dot(att, v) + _dot_nt(q * jnp.exp(b), St)
    kst = k * jnp.exp(b_last - b)
    st_new = St * jnp.exp(b_last) + _dot(v.T, kst)
    yield
    return o, st_new


def _hgrn_scan_body(qf_ref, ff_ref, vf_ref, qb_ref, fb_ref, vb_ref, lb_ref, s0_ref,
                    of_ref, ob_ref, sfin_ref, s_scr):
    i = pl.program_id(1)
    n = pl.num_programs(1)

    @pl.when(i == 0)
    def _():
        s_scr[...] = s0_ref[0]

    chains = []
    for d, (q_ref, f_ref, v_ref) in enumerate(((qf_ref, ff_ref, vf_ref), (qb_ref, fb_ref, vb_ref))):
        for h in range(HG_HEADS):
            sl = slice(h * LANES, (h + 1) * LANES)
            chains.append(_gla_head_chunk(q_ref[0, :, sl], f_ref[0, :, sl], v_ref[0, :, sl],
                                          lb_ref[d:d + 1, sl], s_scr[d, h], d == 1))
    outs = _interleave(chains)
    for d, o_ref in enumerate((of_ref, ob_ref)):
        for h in range(HG_HEADS):
            o, st_new = outs[d * HG_HEADS + h]
            o_ref[0, :, h * LANES:(h + 1) * LANES] = o
            s_scr[d, h] = st_new

    @pl.when(i == n - 1)
    def _():
        sfin_ref[0] = s_scr[...]


def hgrn_scan(proj, lb, s0t):
    B, T, _ = proj.shape
    n = T // HG_L
    col = lambda c, rev: pl.BlockSpec((1, HG_L, HG_KDIM),
                                      (lambda b, i: (b, n - 1 - i, c)) if rev else (lambda b, i: (b, i, c)))
    st_spec = pl.BlockSpec((1, 2, HG_HEADS, HG_DV, HG_DK), lambda b, i: (b, 0, 0, 0, 0))
    out_sd = jax.ShapeDtypeStruct((B, T, HG_VDIM), F32)
    return pl.pallas_call(
        _hgrn_scan_body,
        grid=(B, n),
        in_specs=[col(COL_HG_Q, False), col(COL_HG_FF, False), col(COL_HG_I, False),
                  col(COL_HG_Q, True), col(COL_HG_FB, True), col(COL_HG_I, True),
                  pl.BlockSpec((2, HG_KDIM), lambda b, i: (0, 0)), st_spec],
        out_specs=[pl.BlockSpec((1, HG_L, HG_VDIM), lambda b, i: (b, i, 0)),
                   pl.BlockSpec((1, HG_L, HG_VDIM), lambda b, i: (b, n - 1 - i, 0)), st_spec],
        out_shape=[out_sd, out_sd, jax.ShapeDtypeStruct((B, 2, HG_HEADS, HG_DV, HG_DK), F32)],
        scratch_shapes=[pltpu.VMEM((2, HG_HEADS, HG_DV, HG_DK), F32)],
        compiler_params=_params(("parallel", "arbitrary")),
        name="hgrn_scan",
    )(proj, proj, proj, proj, proj, proj, lb, s0t)


def _branch_merge_body(yf_ref, yb_ref, bonus_ref, gr_ref, of_ref, ob_ref, gh_ref, pgr_ref, pgh_ref,
                       vec_ref, wr_ref, wh_ref, o_ref):
    vec = vec_ref[...]
    y = yf_ref[...] + yb_ref[...]
    ones_head = _ones_blocks(RW_HEAD)
    mean = _group_sums(y, ones_head) * (1.0 / RW_HEAD)
    dlt = y - mean
    var = _group_sums(dlt * dlt, ones_head) * (1.0 / RW_HEAD)
    yn = dlt * lax.rsqrt(var + RW_LN_EPS) * vec[0:1, :] + vec[1:2, :]
    out_r = (yn + bonus_ref[...]) * gr_ref[...]
    o = of_ref[...] + ob_ref[...]
    ms = _group_sums(o * o, _ones_blocks(LANES)) * (1.0 / HG_DV)
    gh = gh_ref[...]
    out_h = o * lax.rsqrt(ms + NORM_EPS) * vec[2:3, :] * (gh * _sigmoid(gh))
    br = _dot(out_r, wr_ref[...])
    bh = _dot(out_h, wh_ref[...])
    o_ref[...] = (_sigmoid(pgr_ref[...]) * br + _sigmoid(pgh_ref[...]) * bh).astype(o_ref.dtype)


def branch_merge(yf, yb, bonus, gr, of, ob, proj, vec, w_br_r, w_br_h, *, tm=128):
    M = yf.shape[0]
    row = lambda w, c: pl.BlockSpec((tm, w), lambda i: (i, c))
    const = lambda shape: pl.BlockSpec(shape, lambda i: (0,) * len(shape))
    return pl.pallas_call(
        _branch_merge_body,
        grid=(M // tm,),
        in_specs=[row(RW_DIM, 0), row(RW_DIM, 0), row(RW_DIM, 0), row(RW_DIM, 0),
                  row(HG_VDIM, 0), row(HG_VDIM, 0), row(HG_VDIM, COL_HG_G),
                  row(D_MODEL, COL_GATE_R), row(D_MODEL, COL_GATE_H),
                  const((8, RW_DIM)), const((RW_DIM, D_MODEL)), const((HG_VDIM, D_MODEL))],
        out_specs=pl.BlockSpec((tm, D_MODEL), lambda i: (i, 0)),
        out_shape=jax.ShapeDtypeStruct((M, D_MODEL), BF16),
        compiler_params=_params(("parallel",)),
        name="branch_merge",
    )(yf, yb, bonus, gr, of, ob, proj, proj, proj, vec, w_br_r, w_br_h)


def _out_proj_body(m_ref, x_ref, mod_ref, g_ref, w_ref, wr_ref, x1_ref, h2_ref, lg_ref):
    mod = mod_ref[0]
    D = D_MODEL
    x1 = x_ref[...] + mod[:, 2 * D:3 * D] * jnp.dot(m_ref[...], w_ref[...], preferred_element_type=F32)
    x1_ref[...] = x1
    y = x1 * lax.rsqrt(jnp.mean(x1 * x1, axis=-1, keepdims=True) + NORM_EPS) * g_ref[...]
    h2 = (y * (1.0 + mod[:, 4 * D:5 * D]) + mod[:, 3 * D:4 * D]).astype(BF16)
    h2_ref[...] = h2
    lg_ref[...] = jnp.dot(h2, wr_ref[...], preferred_element_type=F32)


def out_proj(merged, x, mod, norm_g, w_out, w_router, *, rows_per_mod, tm=256):
    M, D = x.shape
    per = rows_per_mod // tm
    return pl.pallas_call(
        _out_proj_body,
        grid=(M // tm,),
        in_specs=[pl.BlockSpec((tm, D), lambda i: (i, 0)),
                  pl.BlockSpec((tm, D), lambda i: (i, 0)),
                  pl.BlockSpec((1, 1, N_MOD * D), lambda i: (i // per, 0, 0)),
                  pl.BlockSpec((1, D), lambda i: (0, 0)),
                  pl.BlockSpec((D, D), lambda i: (0, 0)),
                  pl.BlockSpec((D, LANES), lambda i: (0, 0))],
        out_specs=[pl.BlockSpec((tm, D), lambda i: (i, 0)),
                   pl.BlockSpec((tm, D), lambda i: (i, 0)),
                   pl.BlockSpec((tm, LANES), lambda i: (i, 0))],
        out_shape=[jax.ShapeDtypeStruct((M, D), F32), jax.ShapeDtypeStruct((M, D), BF16),
                   jax.ShapeDtypeStruct((M, LANES), F32)],
        compiler_params=_params(("parallel",)),
        name="out_proj",
    )(merged, x, mod, norm_g, w_out, w_router)


def _experts_body(x_ref, wg_ref, wu_ref, wd_ref, val_ref, o_ref):
    x = x_ref[0]
    gate = jnp.dot(x, wg_ref[0], preferred_element_type=F32)
    up = jnp.dot(x, wu_ref[0], preferred_element_type=F32)
    hid = (gate * _sigmoid(gate) * up).astype(BF16)
    yo = jnp.dot(hid, wd_ref[0], preferred_element_type=F32)
    o_ref[0] = yo * pltpu.repeat(val_ref[0], D_MODEL // LANES, axis=1)


def moe_experts(xe, w_gate, w_up, w_down, vals, *, tm=256):
    E, M, D = xe.shape
    tm = min(tm, M)
    return pl.pallas_call(
        _experts_body,
        grid=(E, M // tm),
        in_specs=[pl.BlockSpec((1, tm, D), lambda e, i: (e, i, 0)),
                  pl.BlockSpec((1, D, EXPERT_FF), lambda e, i: (e, 0, 0)),
                  pl.BlockSpec((1, D, EXPERT_FF), lambda e, i: (e, 0, 0)),
                  pl.BlockSpec((1, EXPERT_FF, D), lambda e, i: (e, 0, 0)),
                  pl.BlockSpec((1, tm, LANES), lambda e, i: (e, i, 0))],
        out_specs=pl.BlockSpec((1, tm, D), lambda e, i: (e, i, 0)),
        out_shape=jax.ShapeDtypeStruct((E, M, D), F32),
        compiler_params=_params(("parallel", "parallel")),
        name="moe_experts",
    )(xe, w_gate, w_up, w_down, vals)


def _expert_choice_moe(h2, logits, p):
    B, T, D = h2.shape
    cap = CAPACITY_FACTOR * T // N_EXPERTS
    aff = jax.nn.softmax(logits[..., :N_EXPERTS], axis=-1)
    vals, idx = lax.top_k(jnp.swapaxes(aff, 1, 2), cap)
    bidx = jnp.arange(B)[:, None, None]
    xg = h2[bidx, idx]
    xe = jnp.swapaxes(xg, 0, 1).reshape(N_EXPERTS, B * cap, D)
    ve = jnp.broadcast_to(jnp.swapaxes(vals, 0, 1).reshape(N_EXPERTS, B * cap, 1), (N_EXPERTS, B * cap, LANES))
    yo = moe_experts(xe, p['moe_w_gate'], p['moe_w_up'], p['moe_w_down'], ve)
    yo = jnp.swapaxes(yo.reshape(N_EXPERTS, B, cap, D), 0, 1)
    return jnp.zeros((B, T, D), F32).at[bidx, idx].add(yo)


def _layer(x, mod, s_rw, s_hg, width, p):
    B, T, D = x.shape
    rows_per_mod = (B * T) // mod.shape[0]
    xf = x.reshape(B * T, D)
    proj = in_proj(xf, mod, p['norm1_g'], p['w_in'], p['mu'], width=width, rows_per_mod=rows_per_mod)
    proj3 = proj.reshape(B, T, IN_COLS)
    yf, yb, bonus, gr, hfin = rwkv_scan(proj3, p['rw_vec'], p['rw_w2pad'], p['rw_a2pad'], p['rw_g2'],
                                        rwkv_state_to_blockdiag(s_rw))
    of, ob, sfin = hgrn_scan(proj3, p['hg_lb'], jnp.swapaxes(s_hg, -1, -2))
    flat = lambda u: u.reshape(B * T, u.shape[-1])
    merged = branch_merge(flat(yf), flat(yb), flat(bonus), flat(gr), flat(of), flat(ob), proj,
                          p['br_vec'], p['w_br_rwkv'], p['w_br_hgrn'])
    x1, h2, logits = out_proj(merged, xf, mod, p['norm2_g'], p['w_out'], p['moe_router'],
                              rows_per_mod=rows_per_mod)
    moe = _expert_choice_moe(h2.reshape(B, T, D), logits.reshape(B, T, LANES), p)
    gt2 = mod[:, :, 5 * D:6 * D]
    x2 = x1.reshape(B, T, D) + gt2 * moe
    return x2, rwkv_blockdiag_to_state(hfin), jnp.swapaxes(sfin, -1, -2)


def _rmsnorm(x, g, eps=NORM_EPS):
    return x * lax.rsqrt(jnp.mean(x * x, axis=-1, keepdims=True) + eps) * g


def kernel(x_prompt, x_sample, c, state_rwkv, state_hgrn, c_ctx, norm1_g, norm2_g, final_norm_g, ada_w, ada_b, w_in, rw_mu, rw_w0, rw_w2, rw_a0, rw_a2, rw_g2, rw_k_k, rw_k_a, rw_r_k, rw_ln_w, rw_ln_b, hg_lb, hg_norm_g, w_br_rwkv, w_br_hgrn, w_out, moe_router, moe_w_gate, moe_w_up, moe_w_down):
    depth = norm1_g.shape[0]
    D = D_MODEL
    lb_all = jnp.cumsum(jax.nn.softmax(hg_lb.astype(F32), axis=0), axis=0)
    bp = x_prompt.shape[0]
    bs = x_sample.shape[0]
    xp, xs = x_prompt, x_sample
    new_rw, new_hg = [], []
    for l in range(depth):
        wl = w_in[l]
        w_perm = jnp.concatenate([wl[:, RW_COLS + HG_COLS:], wl[:, RW_COLS:RW_COLS + HG_COLS], wl[:, :RW_COLS]],
                                 axis=1).astype(BF16)
        mu_perm = jnp.concatenate([jnp.zeros((2, GATE_COLS + HG_COLS), F32), rw_mu[l]], axis=1)
        pr = dict(rw_w0=rw_w0[l], rw_w2=rw_w2[l], rw_a0=rw_a0[l], rw_a2=rw_a2[l], rw_g2=rw_g2[l],
                  rw_k_k=rw_k_k[l], rw_k_a=rw_k_a[l], rw_r_k=rw_r_k[l])
        rw_vec, rw_w2pad, rw_a2pad, rw_g2b = rwkv_pack_params(pr)
        zrow = jnp.zeros((RW_DIM,), F32)
        br_vec = jnp.stack([rw_ln_w[l], rw_ln_b[l], hg_norm_g[l], zrow, zrow, zrow, zrow, zrow])
        p = dict(norm1_g=norm1_g[l][None, :], norm2_g=norm2_g[l][None, :], w_in=w_perm, mu=mu_perm,
                 rw_vec=rw_vec, rw_w2pad=rw_w2pad, rw_a2pad=rw_a2pad, rw_g2=rw_g2b,
                 hg_lb=lb_all[l], br_vec=br_vec,
                 w_br_rwkv=w_br_rwkv[l].astype(BF16), w_br_hgrn=w_br_hgrn[l].astype(BF16),
                 w_out=w_out[l].astype(BF16),
                 moe_router=jnp.pad(moe_router[l], ((0, 0), (0, LANES - N_EXPERTS))).astype(BF16),
                 moe_w_gate=moe_w_gate[l].astype(BF16), moe_w_up=moe_w_up[l].astype(BF16),
                 moe_w_down=moe_w_down[l].astype(BF16))
        cond = jnp.concatenate([c_ctx[None, :], c], axis=0)
        cond = jnp.pad(jax.nn.silu(cond), ((0, 16 - 1 - bs), (0, 0)))
        mod = matmul(cond, ada_w[l], tm=16, tn=1024)[:1 + bs] + ada_b[l]
        mod = mod.reshape(1 + bs, 1, N_MOD * D)
        zero_rw = jnp.zeros((bp, 2, RW_HEADS, RW_HEAD, RW_HEAD), F32)
        zero_hg = jnp.zeros((bp, 2, HG_HEADS, HG_DK, HG_DV), F32)
        xp, s_rw, s_hg = _layer(xp, mod[:1], zero_rw, zero_hg, xp.shape[1], p)
        new_rw.append(s_rw)
        new_hg.append(s_hg)
        xs, _, _ = _layer(xs, mod[1:], state_rwkv[:, l].astype(F32), state_hgrn[:, l].astype(F32), GRID_W, p)
    y_prompt = _rmsnorm(xp, final_norm_g)
    y_sample = _rmsnorm(xs, final_norm_g)
    return (y_prompt, y_sample, jnp.stack(new_rw, axis=1), jnp.stack(new_hg, axis=1))
```

```python
import functools

import jax
import jax.numpy as jnp
from jax import lax
from jax.experimental import pallas as pl
from jax.experimental.pallas import tpu as pltpu

F32 = jnp.float32
BF16 = jnp.bfloat16

D_MODEL = 2048
GRID_W = 64
RW_HEADS = 16
RW_HEAD = 64
RW_DIM = RW_HEADS * RW_HEAD
DECAY_LORA = 64
AAA_LORA = 64
GATE_LORA = 128
RW_LORA_COLS = 2 * DECAY_LORA + 2 * AAA_LORA + GATE_LORA
RW_LN_EPS = 64e-5
HG_HEADS = 8
HG_DK = 128
HG_DV = 128
HG_KDIM = HG_HEADS * HG_DK
HG_VDIM = HG_HEADS * HG_DV
N_EXPERTS = 16
EXPERT_FF = 1024
CAPACITY_FACTOR = 2
NORM_EPS = 1e-6
N_MOD = 6
RW_COLS = 3 * RW_DIM + RW_LORA_COLS
HG_COLS = 3 * HG_KDIM + 2 * HG_VDIM
GATE_COLS = 2 * D_MODEL
IN_COLS = RW_COLS + HG_COLS + GATE_COLS

LANES = 128
V7X_VMEM_LIMIT_BYTES = 52 * 1024 * 1024

PROJ_TN = 1152
PROJ_PLAIN_TILES = (GATE_COLS + HG_COLS) // PROJ_TN
COL_GATE_R, COL_GATE_H = 0, 1
COL_HG_Q, COL_HG_FF, COL_HG_FB, COL_HG_I, COL_HG_G = 4, 5, 6, 7, 8
COL_RW_R, COL_RW_K, COL_RW_V = 9, 10, 11
COL_RW_LORA = (GATE_COLS + HG_COLS + 3 * RW_DIM) // RW_LORA_COLS

RW_CHUNK = 64
RW_PAIRS = RW_DIM // LANES
HG_L = 128
DECAY_SCALE = 0.6065306597126334


def _params(sem):
    return pltpu.CompilerParams(dimension_semantics=sem, vmem_limit_bytes=V7X_VMEM_LIMIT_BYTES)


def _dot(a, b):
    return jnp.dot(a.astype(BF16), b.astype(BF16), preferred_element_type=F32)


def _dot_nt(a, b):
    return lax.dot_general(a.astype(BF16), b.astype(BF16), (((1,), (1,)), ((), ())),
                           preferred_element_type=F32)


def _split_dot(x, w_bf16):
    hi = x.astype(BF16)
    lo = (x - hi.astype(F32)).astype(BF16)
    return (jnp.dot(hi, w_bf16, preferred_element_type=F32)
            + jnp.dot(lo, w_bf16, preferred_element_type=F32))


def _group_sums(x, ones_bd):
    parts = [_split_dot(x[:, p * LANES:(p + 1) * LANES], ones_bd) for p in range(x.shape[1] // LANES)]
    return jnp.concatenate(parts, axis=1)


def _ones_blocks(group):
    ri = lax.broadcasted_iota(jnp.int32, (LANES, LANES), 0)
    ci = lax.broadcasted_iota(jnp.int32, (LANES, LANES), 1)
    return jnp.where((ri // group) == (ci // group), 1.0, 0.0).astype(BF16)


def _sigmoid(x):
    return 1.0 / (1.0 + jnp.exp(-x))


def _cumsum_rows(x, reverse):
    L = x.shape[0]
    row = lax.broadcasted_iota(jnp.int32, x.shape, 0)
    s = 1
    while s < L:
        if reverse:
            x = x + jnp.where(row < L - s, pltpu.roll(x, L - s, 0), 0.0)
        else:
            x = x + jnp.where(row >= s, pltpu.roll(x, s, 0), 0.0)
        s *= 2
    return x


def _interleave(gens):
    results = [None] * len(gens)
    active = list(enumerate(gens))
    while active:
        still = []
        for idx, g in active:
            try:
                next(g)
                still.append((idx, g))
            except StopIteration as e:
                results[idx] = e.value
        active = still
    return results


def _mm_body(x_ref, w_ref, o_ref):
    o_ref[...] = _dot(x_ref[...], w_ref[...]).astype(o_ref.dtype)


def matmul(x, w, *, tm, tn, out_dtype=F32):
    M, K = x.shape
    _, N = w.shape
    return pl.pallas_call(
        _mm_body,
        grid=(M // tm, N // tn),
        in_specs=[pl.BlockSpec((tm, K), lambda i, j: (i, 0)),
                  pl.BlockSpec((K, tn), lambda i, j: (0, j))],
        out_specs=pl.BlockSpec((tm, tn), lambda i, j: (i, j)),
        out_shape=jax.ShapeDtypeStruct((M, N), out_dtype),
        compiler_params=_params(("parallel", "parallel")),
        name="matmul",
    )(x, w)


def _in_proj_body(x_ref, mod_ref, g_ref, w_ref, mu_ref, o_ref, h_scr, *, width):
    j = pl.program_id(1)

    @pl.when(j == 0)
    def _():
        x = x_ref[...]
        y = x * lax.rsqrt(jnp.mean(x * x, axis=-1, keepdims=True) + NORM_EPS) * g_ref[...]
        m = mod_ref[0]
        h_scr[...] = (y * (1.0 + m[:, D_MODEL:2 * D_MODEL]) + m[:, 0:D_MODEL]).astype(BF16)

    @pl.when(j < PROJ_PLAIN_TILES)
    def _():
        o_ref[...] = jnp.dot(h_scr[...], w_ref[...], preferred_element_type=F32)

    @pl.when(j >= PROJ_PLAIN_TILES)
    def _():
        acc = jnp.dot(h_scr[...], w_ref[...], preferred_element_type=F32)
        tm = acc.shape[0]
        pos = lax.broadcasted_iota(jnp.int32, acc.shape, 0) & (width - 1)
        prev = jnp.where(pos == 0, 0.0, pltpu.roll(acc, 1, 0))
        nxt = jnp.where(pos == width - 1, 0.0, pltpu.roll(acc, tm - 1, 0))
        mu = mu_ref[...]
        o_ref[...] = acc + mu[0:1, :] * (prev - acc) + mu[1:2, :] * (nxt - acc)


def in_proj(x, mod, norm_g, w_perm, mu_perm, *, width, rows_per_mod, tm=1024):
    M, D = x.shape
    assert tm % width == 0 and rows_per_mod % tm == 0
    per = rows_per_mod // tm
    return pl.pallas_call(
        functools.partial(_in_proj_body, width=width),
        grid=(M // tm, IN_COLS // PROJ_TN),
        in_specs=[pl.BlockSpec((tm, D), lambda i, j: (i, 0), pipeline_mode=pl.Buffered(1)),
                  pl.BlockSpec((1, 1, N_MOD * D), lambda i, j: (i // per, 0, 0)),
                  pl.BlockSpec((1, D), lambda i, j: (0, 0)),
                  pl.BlockSpec((D, PROJ_TN), lambda i, j: (0, j)),
                  pl.BlockSpec((2, PROJ_TN), lambda i, j: (0, j))],
        out_specs=pl.BlockSpec((tm, PROJ_TN), lambda i, j: (i, j)),
        out_shape=jax.ShapeDtypeStruct((M, IN_COLS), F32),
        scratch_shapes=[pltpu.VMEM((tm, D), BF16)],
        compiler_params=_params(("parallel", "arbitrary")),
        name="in_proj",
    )(x, mod, norm_g, w_perm, mu_perm)


def _block_diag(x, lane_lo):
    return jnp.concatenate([jnp.where(lane_lo, x, 0.0), jnp.where(lane_lo, 0.0, x)], axis=0)


def _rwkv_pair_chunk(q, lw, kd, v, kk, a, H, reverse):
    L = q.shape[0]
    lane = lax.broadcasted_iota(jnp.int32, (L, LANES), 1)
    lane_lo = lane < RW_HEAD
    c = _cumsum_rows(lw, reverse)
    c_last = c[0:1, :] if reverse else c[L - 1:L, :]
    g_inc = jnp.exp(c)
    g_exc = jnp.exp(c - lw)
    g_inv = jnp.exp(-c)
    g_rem = jnp.exp(c_last - c)
    beta = kk * a
    ab = _block_diag(-kk * g_exc, lane_lo)
    qb = _block_diag(q * g_inc, lane_lo)
    bt = _block_diag(beta * g_inv, lane_lo)
    kt = _block_diag(kd * g_inv, lane_lo)
    vb = _block_diag(v, lane_lo)
    bh = _block_diag(beta * g_rem, lane_lo)
    kh = _block_diag(kd * g_rem, lane_lo)

    n2 = 2 * L
    ri = lax.broadcasted_iota(jnp.int32, (n2, n2), 0)
    ci = lax.broadcasted_iota(jnp.int32, (n2, n2), 1)
    rt = ri % L
    ct = ci % L
    if reverse:
        strict = rt < ct
        incl = rt <= ct
    else:
        strict = rt > ct
        incl = rt >= ct

    aa = _dot_nt(jnp.concatenate([ab, qb], axis=0), jnp.concatenate([bt, kt], axis=0))
    yield
    a_ab = jnp.where(strict, aa[:n2, :n2], 0.0)
    a_ak = jnp.where(strict, aa[:n2, n2:], 0.0)
    a_qb = jnp.where(incl, aa[n2:, :n2], 0.0)
    a_qk = jnp.where(incl, aa[n2:, n2:], 0.0)

    rb = rt // 2
    cb = ct // 2
    P = jnp.where(ri == ci, 1.0, 0.0) + jnp.where(rb == cb, a_ab, 0.0)
    av = _dot(a_ak, vb)
    s = 2
    while s < L:
        rs, cs = rb, cb
        rb = rb // 2
        cb = cb // 2
        coupling = jnp.where(jnp.logical_and(rb == cb, rs != cs), a_ab, 0.0)
        pc = _dot(P, coupling)
        yield
        P = P + _dot(pc, P)
        yield
        s *= 2

    wu_u0 = _dot(P, jnp.concatenate([ab, av], axis=1))
    yield
    rhs = jnp.concatenate([wu_u0, jnp.concatenate([jnp.zeros_like(vb), vb], axis=1)], axis=0)
    qy = _dot(jnp.concatenate([a_qb, a_qk], axis=1), rhs)
    st = _dot(jnp.concatenate([bh, kh], axis=0).T, rhs)
    yield
    q_eff = qb + qy[:, :LANES]
    y0 = qy[:, LANES:]
    g_last = jnp.exp(jnp.broadcast_to(c_last, (n2, LANES)))
    m_mat = jnp.where(ri == ci, g_last, 0.0) + st[:, :LANES]
    n_mat = st[:, LANES:]
    y = _dot(q_eff, H) + y0
    h_new = _dot(m_mat, H) + n_mat
    yield
    return y[:L] + y[L:], h_new


def _rwkv_prep(r, k, v, lora, vec, w2pad, a2pad, ones_bd, d, want_other_kd):
    lora_in_w = lora[:, 0:LANES]
    lora_in_a = lora[:, LANES:2 * LANES]
    kk = k * vec[4:5, :]
    ss = _group_sums(kk * kk, ones_bd)
    kk = kk * lax.rsqrt(jnp.maximum(ss, 1e-24))
    u = vec[d:d + 1, :] + jnp.dot(jnp.tanh(lora_in_w).astype(BF16), w2pad[d], preferred_element_type=F32)
    lw = -DECAY_SCALE * _sigmoid(u)
    a = _sigmoid(vec[2 + d:3 + d, :] + jnp.dot(lora_in_a.astype(BF16), a2pad[d], preferred_element_type=F32))
    kd = k * (1.0 + (a - 1.0) * vec[5:6, :])
    out = dict(r=r, k=k, v=v, kk=kk, lw=lw, a=a, kd=kd)
    if want_other_kd:
        o = 1 - d
        a_o = _sigmoid(vec[2 + o:3 + o, :] + jnp.dot(lora_in_a.astype(BF16), a2pad[o], preferred_element_type=F32))
        out['kd_other'] = k * (1.0 + (a_o - 1.0) * vec[5:6, :])
    return out


def _rwkv_scan_body(rf_ref, kf_ref, vf_ref, lf_ref, rb_ref, kb_ref, vb_ref, lb_ref,
                    vec_ref, w2_ref, a2_ref, g2_ref, h0_ref,
                    yf_ref, yb_ref, bonus_ref, g_ref, hfin_ref, h_scr):
    i = pl.program_id(1)
    n = pl.num_programs(1)

    @pl.when(i == 0)
    def _():
        h_scr[...] = h0_ref[0]

    vec = vec_ref[...]
    ones_bd = _ones_blocks(RW_HEAD)
    fw = _rwkv_prep(rf_ref[0], kf_ref[0], vf_ref[0], lf_ref[0], vec, w2_ref, a2_ref, ones_bd, 0, True)
    bw = _rwkv_prep(rb_ref[0], kb_ref[0], vb_ref[0], lb_ref[0], vec, w2_ref, a2_ref, ones_bd, 1, False)

    rk = fw['r'] * vec[6:7, :]
    bonus_ref[0] = _group_sums(rk * (fw['kd'] + fw['kd_other']), ones_bd) * fw['v']
    gd = lf_ref[0][:, 2 * LANES:3 * LANES]
    g_ref[0] = jnp.dot(_sigmoid(gd).astype(BF16), g2_ref[...], preferred_element_type=F32)

    chains = []
    for d, t in enumerate((fw, bw)):
        for p in range(RW_PAIRS):
            sl = slice(p * LANES, (p + 1) * LANES)
            chains.append(_rwkv_pair_chunk(t['r'][:, sl], t['lw'][:, sl], t['kd'][:, sl], t['v'][:, sl],
                                           t['kk'][:, sl], t['a'][:, sl], h_scr[d, p], d == 1))
    outs = _interleave(chains)
    for d, y_ref in enumerate((yf_ref, yb_ref)):
        for p in range(RW_PAIRS):
            y, h_new = outs[d * RW_PAIRS + p]
            y_ref[0, :, p * LANES:(p + 1) * LANES] = y
            h_scr[d, p] = h_new

    @pl.when(i == n - 1)
    def _():
        hfin_ref[0] = h_scr[...]


def rwkv_scan(proj, vec, w2pad, a2pad, g2, h0):
    B, T, _ = proj.shape
    n = T // RW_CHUNK
    fwd = lambda c: (lambda b, i: (b, i, c))
    bwd = lambda c: (lambda b, i: (b, n - 1 - i, c))
    wide = lambda fn: pl.BlockSpec((1, RW_CHUNK, RW_DIM), fn)
    lora = lambda fn: pl.BlockSpec((1, RW_CHUNK, RW_LORA_COLS), fn)
    const = lambda shape: pl.BlockSpec(shape, lambda b, i: (0,) * len(shape))
    st_spec = pl.BlockSpec((1, 2, RW_PAIRS, LANES, LANES), lambda b, i: (b, 0, 0, 0, 0))
    out_sd = jax.ShapeDtypeStruct((B, T, RW_DIM), F32)
    return pl.pallas_call(
        _rwkv_scan_body,
        grid=(B, n),
        in_specs=[wide(fwd(COL_RW_R)), wide(fwd(COL_RW_K)), wide(fwd(COL_RW_V)), lora(fwd(COL_RW_LORA)),
                  wide(bwd(COL_RW_R)), wide(bwd(COL_RW_K)), wide(bwd(COL_RW_V)), lora(bwd(COL_RW_LORA)),
                  const((8, RW_DIM)), const((2, LANES, RW_DIM)), const((2, LANES, RW_DIM)),
                  const((LANES, RW_DIM)), st_spec],
        out_specs=[wide(fwd(0)), wide(bwd(0)), wide(fwd(0)), wide(fwd(0)), st_spec],
        out_shape=[out_sd, out_sd, out_sd, out_sd,
                   jax.ShapeDtypeStruct((B, 2, RW_PAIRS, LANES, LANES), F32)],
        scratch_shapes=[pltpu.VMEM((2, RW_PAIRS, LANES, LANES), F32)],
        compiler_params=_params(("parallel", "arbitrary")),
        name="rwkv_scan",
    )(proj, proj, proj, proj, proj, proj, proj, proj, vec, w2pad, a2pad, g2, h0)


def rwkv_pack_params(p):
    z = jnp.zeros((RW_HEAD, RW_DIM), F32)
    w2pad = jnp.stack([jnp.concatenate([p['rw_w2'][0], z], 0), jnp.concatenate([z, p['rw_w2'][1]], 0)]).astype(BF16)
    a2pad = jnp.stack([jnp.concatenate([p['rw_a2'][0], z], 0), jnp.concatenate([z, p['rw_a2'][1]], 0)]).astype(BF16)
    vec = jnp.stack([p['rw_w0'][0], p['rw_w0'][1], p['rw_a0'][0], p['rw_a0'][1], p['rw_k_k'], p['rw_k_a'],
                     p['rw_r_k'].reshape(RW_DIM), jnp.zeros((RW_DIM,), F32)])
    return vec, w2pad, a2pad, p['rw_g2'].astype(BF16)


def rwkv_state_to_blockdiag(s):
    B = s.shape[0]
    h = jnp.swapaxes(s, -1, -2).reshape(B, 2, RW_PAIRS, 2, RW_HEAD, RW_HEAD)
    z = jnp.zeros_like(h[:, :, :, 0])
    top = jnp.concatenate([h[:, :, :, 0], z], axis=-1)
    bot = jnp.concatenate([z, h[:, :, :, 1]], axis=-1)
    return jnp.concatenate([top, bot], axis=-2)


def rwkv_blockdiag_to_state(hb):
    B = hb.shape[0]
    h0 = hb[:, :, :, :RW_HEAD, :RW_HEAD]
    h1 = hb[:, :, :, RW_HEAD:, RW_HEAD:]
    h = jnp.stack([h0, h1], axis=3).reshape(B, 2, RW_HEADS, RW_HEAD, RW_HEAD)
    return jnp.swapaxes(h, -1, -2)


def _gla_head_chunk(q_raw, fl, v, lb, St, reverse):
    L = q_raw.shape[0]
    q = q_raw * _sigmoid(q_raw)
    f = lb + (1.0 - lb) * _sigmoid(fl)
    k = 1.0 - f
    g = jnp.log(f)
    b = _cumsum_rows(g, reverse)
    b_last = b[0:1, :] if reverse else b[L - 1:L, :]
    row = lax.broadcasted_iota(jnp.int32, (L, LANES), 0)
    ri = lax.broadcasted_iota(jnp.int32, (L, L), 0)
    ci = lax.broadcasted_iota(jnp.int32, (L, L), 1)

    prods = [(_dot_nt(q, k), ri == ci)]
    fk = b
    half = 1
    while half < L:
        size = 2 * half
        upper = (row & half) != 0
        if reverse:
            bref = jnp.where(upper, fk, pltpu.roll(fk, L - half, 0))
            is_q = jnp.logical_not(upper)
        else:
            bref = jnp.where(upper, pltpu.roll(fk, half, 0), fk)
            is_q = upper
        e = jnp.exp(jnp.where(is_q, b - bref, bref - b))
        qh = jnp.where(is_q, q * e, 0.0)
        kh = jnp.where(is_q, 0.0, k * e)
        mask = None if size == L else (ri // size) == (ci // size)
        prods.append((_dot_nt(qh, kh), mask))
        if reverse:
            fk = jnp.where(upper, pltpu.roll(fk, half, 0), fk)
        else:
            fk = jnp.where(upper, fk, pltpu.roll(fk, L - half, 0))
        half = size
    yield
    att = None
    for pr, mask in prods:
        term = pr if mask is None else jnp.where(mask, pr, 0.0)
        att = term if att is None else att + term
    o = _dot(att, v) + _dot_nt(q * jnp.exp(b), St)
    kst = k * jnp.exp(b_last - b)
    st_new = St * jnp.exp(b_last) + _dot(v.T, kst)
    yield
    return o, st_new


def _hgrn_scan_body(qf_ref, ff_ref, vf_ref, qb_ref, fb_ref, vb_ref, lb_ref, s0_ref,
                    of_ref, ob_ref, sfin_ref, s_scr):
    i = pl.program_id(1)
    n = pl.num_programs(1)

    @pl.when(i == 0)
    def _():
        s_scr[...] = s0_ref[0]

    chains = []
    for d, (q_ref, f_ref, v_ref) in enumerate(((qf_ref, ff_ref, vf_ref), (qb_ref, fb_ref, vb_ref))):
        for h in range(HG_HEADS):
            sl = slice(h * LANES, (h + 1) * LANES)
            chains.append(_gla_head_chunk(q_ref[0, :, sl], f_ref[0, :, sl], v_ref[0, :, sl],
                                          lb_ref[d:d + 1, sl], s_scr[d, h], d == 1))
    outs = _interleave(chains)
    for d, o_ref in enumerate((of_ref, ob_ref)):
        for h in range(HG_HEADS):
            o, st_new = outs[d * HG_HEADS + h]
            o_ref[0, :, h * LANES:(h + 1) * LANES] = o
            s_scr[d, h] = st_new

    @pl.when(i == n - 1)
    def _():
        sfin_ref[0] = s_scr[...]


def hgrn_scan(proj, lb, s0t):
    B, T, _ = proj.shape
    n = T // HG_L
    col = lambda c, rev: pl.BlockSpec((1, HG_L, HG_KDIM),
                                      (lambda b, i: (b, n - 1 - i, c)) if rev else (lambda b, i: (b, i, c)))
    st_spec = pl.BlockSpec((1, 2, HG_HEADS, HG_DV, HG_DK), lambda b, i: (b, 0, 0, 0, 0))
    out_sd = jax.ShapeDtypeStruct((B, T, HG_VDIM), F32)
    return pl.pallas_call(
        _hgrn_scan_body,
        grid=(B, n),
        in_specs=[col(COL_HG_Q, False), col(COL_HG_FF, False), col(COL_HG_I, False),
                  col(COL_HG_Q, True), col(COL_HG_FB, True), col(COL_HG_I, True),
                  pl.BlockSpec((2, HG_KDIM), lambda b, i: (0, 0)), st_spec],
        out_specs=[pl.BlockSpec((1, HG_L, HG_VDIM), lambda b, i: (b, i, 0)),
                   pl.BlockSpec((1, HG_L, HG_VDIM), lambda b, i: (b, n - 1 - i, 0)), st_spec],
        out_shape=[out_sd, out_sd, jax.ShapeDtypeStruct((B, 2, HG_HEADS, HG_DV, HG_DK), F32)],
        scratch_shapes=[pltpu.VMEM((2, HG_HEADS, HG_DV, HG_DK), F32)],
        compiler_params=_params(("parallel", "arbitrary")),
        name="hgrn_scan",
    )(proj, proj, proj, proj, proj, proj, lb, s0t)


def _branch_merge_body(yf_ref, yb_ref, bonus_ref, gr_ref, of_ref, ob_ref, gh_ref, pgr_ref, pgh_ref,
                       vec_ref, wr_ref, wh_ref, o_ref):
    vec = vec_ref[...]
    y = yf_ref[...] + yb_ref[...]
    ones_head = _ones_blocks(RW_HEAD)
    mean = _group_sums(y, ones_head) * (1.0 / RW_HEAD)
    dlt = y - mean
    var = _group_sums(dlt * dlt, ones_head) * (1.0 / RW_HEAD)
    yn = dlt * lax.rsqrt(var + RW_LN_EPS) * vec[0:1, :] + vec[1:2, :]
    out_r = (yn + bonus_ref[...]) * gr_ref[...]
    o = of_ref[...] + ob_ref[...]
    ms = _group_sums(o * o, _ones_blocks(LANES)) * (1.0 / HG_DV)
    gh = gh_ref[...]
    out_h = o * lax.rsqrt(ms + NORM_EPS) * vec[2:3, :] * (gh * _sigmoid(gh))
    br = _dot(out_r, wr_ref[...])
    bh = _dot(out_h, wh_ref[...])
    o_ref[...] = (_sigmoid(pgr_ref[...]) * br + _sigmoid(pgh_ref[...]) * bh).astype(o_ref.dtype)


def branch_merge(yf, yb, bonus, gr, of, ob, proj, vec, w_br_r, w_br_h, *, tm=256):
    M = yf.shape[0]
    row = lambda w, c: pl.BlockSpec((tm, w), lambda i: (i, c))
    const = lambda shape: pl.BlockSpec(shape, lambda i: (0,) * len(shape), pipeline_mode=pl.Buffered(1))
    return pl.pallas_call(
        _branch_merge_body,
        grid=(M // tm,),
        in_specs=[row(RW_DIM, 0), row(RW_DIM, 0), row(RW_DIM, 0), row(RW_DIM, 0),
                  row(HG_VDIM, 0), row(HG_VDIM, 0), row(HG_VDIM, COL_HG_G),
                  row(D_MODEL, COL_GATE_R), row(D_MODEL, COL_GATE_H),
                  const((8, RW_DIM)), const((RW_DIM, D_MODEL)), const((HG_VDIM, D_MODEL))],
        out_specs=pl.BlockSpec((tm, D_MODEL), lambda i: (i, 0)),
        out_shape=jax.ShapeDtypeStruct((M, D_MODEL), BF16),
        compiler_params=_params(("parallel",)),
        name="branch_merge",
    )(yf, yb, bonus, gr, of, ob, proj, proj, proj, vec, w_br_r, w_br_h)


def _out_proj_body(m_ref, x_ref, mod_ref, g_ref, w_ref, wr_ref, x1_ref, h2_ref, lg_ref):
    mod = mod_ref[0]
    D = D_MODEL
    x1 = x_ref[...] + mod[:, 2 * D:3 * D] * jnp.dot(m_ref[...], w_ref[...], preferred_element_type=F32)
    x1_ref[...] = x1
    y = x1 * lax.rsqrt(jnp.mean(x1 * x1, axis=-1, keepdims=True) + NORM_EPS) * g_ref[...]
    h2 = (y * (1.0 + mod[:, 4 * D:5 * D]) + mod[:, 3 * D:4 * D]).astype(BF16)
    h2_ref[...] = h2
    lg_ref[...] = jnp.dot(h2, wr_ref[...], preferred_element_type=F32)


def out_proj(merged, x, mod, norm_g, w_out, w_router, *, rows_per_mod, tm=512):
    M, D = x.shape
    per = rows_per_mod // tm
    return pl.pallas_call(
        _out_proj_body,
        grid=(M // tm,),
        in_specs=[pl.BlockSpec((tm, D), lambda i: (i, 0)),
                  pl.BlockSpec((tm, D), lambda i: (i, 0)),
                  pl.BlockSpec((1, 1, N_MOD * D), lambda i: (i // per, 0, 0)),
                  pl.BlockSpec((1, D), lambda i: (0, 0)),
                  pl.BlockSpec((D, D), lambda i: (0, 0), pipeline_mode=pl.Buffered(1)),
                  pl.BlockSpec((D, LANES), lambda i: (0, 0), pipeline_mode=pl.Buffered(1))],
        out_specs=[pl.BlockSpec((tm, D), lambda i: (i, 0)),
                   pl.BlockSpec((tm, D), lambda i: (i, 0)),
                   pl.BlockSpec((tm, LANES), lambda i: (i, 0))],
        out_shape=[jax.ShapeDtypeStruct((M, D), F32), jax.ShapeDtypeStruct((M, D), BF16),
                   jax.ShapeDtypeStruct((M, LANES), F32)],
        compiler_params=_params(("parallel",)),
        name="out_proj",
    )(merged, x, mod, norm_g, w_out, w_router)


def _experts_body(x_ref, wg_ref, wu_ref, wd_ref, val_ref, o_ref):
    x = x_ref[0]
    gate = jnp.dot(x, wg_ref[0], preferred_element_type=F32)
    up = jnp.dot(x, wu_ref[0], preferred_element_type=F32)
    hid = (gate * _sigmoid(gate) * up).astype(BF16)
    yo = jnp.dot(hid, wd_ref[0], preferred_element_type=F32)
    scale = jnp.concatenate([val_ref[0]] * (D_MODEL // LANES), axis=1)
    o_ref[0] = (yo * scale).astype(o_ref.dtype)


def moe_experts(xe, w_gate, w_up, w_down, vals, *, tm=256):
    E, M, D = xe.shape
    tm = min(tm, M)
    return pl.pallas_call(
        _experts_body,
        grid=(E, M // tm),
        in_specs=[pl.BlockSpec((1, tm, D), lambda e, i: (e, i, 0)),
                  pl.BlockSpec((1, D, EXPERT_FF), lambda e, i: (e, 0, 0)),
                  pl.BlockSpec((1, D, EXPERT_FF), lambda e, i: (e, 0, 0)),
                  pl.BlockSpec((1, EXPERT_FF, D), lambda e, i: (e, 0, 0)),
                  pl.BlockSpec((1, tm, LANES), lambda e, i: (e, i, 0))],
        out_specs=pl.BlockSpec((1, tm, D), lambda e, i: (e, i, 0)),
        out_shape=jax.ShapeDtypeStruct((E, M, D), BF16),
        compiler_params=_params(("parallel", "parallel")),
        name="moe_experts",
    )(xe, w_gate, w_up, w_down, vals)


def _moe_combine_body(yo_ref, idx_ref, x1_ref, gt_ref, o_ref):
    T = x1_ref.shape[1]
    cap = idx_ref.shape[2]
    rows = min(T, 512)
    o_ref[0] = jnp.zeros(o_ref.shape[1:], F32)

    def one_expert(e, carry):
        slot_tok = idx_ref[0, pl.ds(e, 1), :]
        yo_e = yo_ref[e, 0]
        for r in range(T // rows):
            tok = lax.broadcasted_iota(jnp.int32, (rows, cap), 0) + r * rows
            onehot = jnp.where(tok == slot_tok, 1.0, 0.0).astype(BF16)
            o_ref[0, r * rows:(r + 1) * rows, :] += jnp.dot(onehot, yo_e, preferred_element_type=F32)
        return carry

    lax.fori_loop(0, N_EXPERTS, one_expert, 0)
    o_ref[0] = x1_ref[0] + gt_ref[0] * o_ref[0]


def moe_combine(yo, idx, x1, gt2, *, dn=256):
    E, B, cap, D = yo.shape
    T = x1.shape[1]
    per = B // gt2.shape[0]
    return pl.pallas_call(
        _moe_combine_body,
        grid=(B, D // dn),
        in_specs=[pl.BlockSpec((E, 1, cap, dn), lambda b, j: (0, b, 0, j)),
                  pl.BlockSpec((1, E, cap), lambda b, j: (b, 0, 0)),
                  pl.BlockSpec((1, T, dn), lambda b, j: (b, 0, j)),
                  pl.BlockSpec((1, 1, dn), lambda b, j: (b // per, 0, j))],
        out_specs=pl.BlockSpec((1, T, dn), lambda b, j: (b, 0, j)),
        out_shape=jax.ShapeDtypeStruct((B, T, D), F32),
        compiler_params=_params(("parallel", "parallel")),
        name="moe_combine",
    )(yo, idx, x1, gt2)


def _expert_choice_moe(h2, logits, x1, gt2, p):
    B, T, D = h2.shape
    cap = CAPACITY_FACTOR * T // N_EXPERTS
    aff = jax.nn.softmax(logits[..., :N_EXPERTS], axis=-1)
    vals, idx = lax.top_k(jnp.swapaxes(aff, 1, 2), cap)
    bidx = jnp.arange(B)[:, None, None]
    xg = h2[bidx, idx]
    xe = jnp.swapaxes(xg, 0, 1).reshape(N_EXPERTS, B * cap, D)
    ve = jnp.broadcast_to(jnp.swapaxes(vals, 0, 1).reshape(N_EXPERTS, B * cap, 1), (N_EXPERTS, B * cap, LANES))
    yo = moe_experts(xe, p['moe_w_gate'], p['moe_w_up'], p['moe_w_down'], ve)
    return moe_combine(yo.reshape(N_EXPERTS, B, cap, D), idx, x1, gt2)


def _layer(x, mod, s_rw, s_hg, width, p):
    B, T, D = x.shape
    rows_per_mod = (B * T) // mod.shape[0]
    xf = x.reshape(B * T, D)
    proj = in_proj(xf, mod, p['norm1_g'], p['w_in'], p['mu'], width=width, rows_per_mod=rows_per_mod)
    proj3 = proj.reshape(B, T, IN_COLS)
    yf, yb, bonus, gr, hfin = rwkv_scan(proj3, p['rw_vec'], p['rw_w2pad'], p['rw_a2pad'], p['rw_g2'],
                                        rwkv_state_to_blockdiag(s_rw))
    of, ob, sfin = hgrn_scan(proj3, p['hg_lb'], jnp.swapaxes(s_hg, -1, -2))
    flat = lambda u: u.reshape(B * T, u.shape[-1])
    merged = branch_merge(flat(yf), flat(yb), flat(bonus), flat(gr), flat(of), flat(ob), proj,
                          p['br_vec'], p['w_br_rwkv'], p['w_br_hgrn'])
    x1, h2, logits = out_proj(merged, xf, mod, p['norm2_g'], p['w_out'], p['moe_router'],
                              rows_per_mod=rows_per_mod)
    x2 = _expert_choice_moe(h2.reshape(B, T, D), logits.reshape(B, T, LANES), x1.reshape(B, T, D),
                            mod[:, :, 5 * D:6 * D], p)
    return x2, rwkv_blockdiag_to_state(hfin), jnp.swapaxes(sfin, -1, -2)


def _rmsnorm(x, g, eps=NORM_EPS):
    return x * lax.rsqrt(jnp.mean(x * x, axis=-1, keepdims=True) + eps) * g


def kernel(x_prompt, x_sample, c, state_rwkv, state_hgrn, c_ctx, norm1_g, norm2_g, final_norm_g, ada_w, ada_b, w_in, rw_mu, rw_w0, rw_w2, rw_a0, rw_a2, rw_g2, rw_k_k, rw_k_a, rw_r_k, rw_ln_w, rw_ln_b, hg_lb, hg_norm_g, w_br_rwkv, w_br_hgrn, w_out, moe_router, moe_w_gate, moe_w_up, moe_w_down):
    depth = norm1_g.shape[0]
    D = D_MODEL
    lb_all = jnp.cumsum(jax.nn.softmax(hg_lb.astype(F32), axis=0), axis=0)
    bp = x_prompt.shape[0]
    bs = x_sample.shape[0]
    xp, xs = x_prompt, x_sample
    new_rw, new_hg = [], []
    for l in range(depth):
        wl = w_in[l]
        w_perm = jnp.concatenate([wl[:, RW_COLS + HG_COLS:], wl[:, RW_COLS:RW_COLS + HG_COLS], wl[:, :RW_COLS]],
                                 axis=1).astype(BF16)
        mu_perm = jnp.concatenate([jnp.zeros((2, GATE_COLS + HG_COLS), F32), rw_mu[l]], axis=1)
        pr = dict(rw_w0=rw_w0[l], rw_w2=rw_w2[l], rw_a0=rw_a0[l], rw_a2=rw_a2[l], rw_g2=rw_g2[l],
                  rw_k_k=rw_k_k[l], rw_k_a=rw_k_a[l], rw_r_k=rw_r_k[l])
        rw_vec, rw_w2pad, rw_a2pad, rw_g2b = rwkv_pack_params(pr)
        zrow = jnp.zeros((RW_DIM,), F32)
        br_vec = jnp.stack([rw_ln_w[l], rw_ln_b[l], hg_norm_g[l], zrow, zrow, zrow, zrow, zrow])
        p = dict(norm1_g=norm1_g[l][None, :], norm2_g=norm2_g[l][None, :], w_in=w_perm, mu=mu_perm,
                 rw_vec=rw_vec, rw_w2pad=rw_w2pad, rw_a2pad=rw_a2pad, rw_g2=rw_g2b,
                 hg_lb=lb_all[l], br_vec=br_vec,
                 w_br_rwkv=w_br_rwkv[l].astype(BF16), w_br_hgrn=w_br_hgrn[l].astype(BF16),
                 w_out=w_out[l].astype(BF16),
                 moe_router=jnp.pad(moe_router[l], ((0, 0), (0, LANES - N_EXPERTS))).astype(BF16),
                 moe_w_gate=moe_w_gate[l].astype(BF16), moe_w_up=moe_w_up[l].astype(BF16),
                 moe_w_down=moe_w_down[l].astype(BF16))
        cond = jnp.concatenate([c_ctx[None, :], c], axis=0)
        cond = jnp.pad(jax.nn.silu(cond), ((0, 16 - 1 - bs), (0, 0)))
        mod = matmul(cond, ada_w[l], tm=16, tn=1024)[:1 + bs] + ada_b[l]
        mod = mod.reshape(1 + bs, 1, N_MOD * D)
        zero_rw = jnp.zeros((bp, 2, RW_HEADS, RW_HEAD, RW_HEAD), F32)
        zero_hg = jnp.zeros((bp, 2, HG_HEADS, HG_DK, HG_DV), F32)
        xp, s_rw, s_hg = _layer(xp, mod[:1], zero_rw, zero_hg, xp.shape[1], p)
        new_rw.append(s_rw)
        new_hg.append(s_hg)
        xs, _, _ = _layer(xs, mod[1:], state_rwkv[:, l].astype(F32), state_hgrn[:, l].astype(F32), GRID_W, p)
    y_prompt = _rmsnorm(xp, final_norm_g)
    y_sample = _rmsnorm(xs, final_norm_g)
    return (y_prompt, y_sample, jnp.stack(new_rw, axis=1), jnp.stack(new_hg, axis=1))
```

```python
import functools

import jax
import jax.numpy as jnp
from jax import lax
from jax.experimental import pallas as pl
from jax.experimental.pallas import tpu as pltpu

F32 = jnp.float32
BF16 = jnp.bfloat16

D_MODEL = 2048
GRID_W = 64
RW_HEADS = 16
RW_HEAD = 64
RW_DIM = RW_HEADS * RW_HEAD
DECAY_LORA = 64
AAA_LORA = 64
GATE_LORA = 128
RW_LORA_COLS = 2 * DECAY_LORA + 2 * AAA_LORA + GATE_LORA
RW_LN_EPS = 64e-5
HG_HEADS = 8
HG_DK = 128
HG_DV = 128
HG_KDIM = HG_HEADS * HG_DK
HG_VDIM = HG_HEADS * HG_DV
N_EXPERTS = 16
EXPERT_FF = 1024
CAPACITY_FACTOR = 2
NORM_EPS = 1e-6
N_MOD = 6
RW_COLS = 3 * RW_DIM + RW_LORA_COLS
HG_COLS = 3 * HG_KDIM + 2 * HG_VDIM
GATE_COLS = 2 * D_MODEL
IN_COLS = RW_COLS + HG_COLS + GATE_COLS

LANES = 128
V7X_VMEM_LIMIT_BYTES = 52 * 1024 * 1024

PROJ_TN = 1152
PROJ_PLAIN_TILES = (GATE_COLS + HG_COLS) // PROJ_TN
COL_GATE_R, COL_GATE_H = 0, 1
COL_HG_Q, COL_HG_FF, COL_HG_FB, COL_HG_I, COL_HG_G = 4, 5, 6, 7, 8
COL_RW_R, COL_RW_K, COL_RW_V = 9, 10, 11
COL_RW_LORA = (GATE_COLS + HG_COLS + 3 * RW_DIM) // RW_LORA_COLS

RW_CHUNK = 64
RW_PAIRS = RW_DIM // LANES
HG_L = 128
HG_FIRST_ROUND = 4
COMBINE_TILE_ELEMS = 1024 * 1024
DECAY_SCALE = 0.6065306597126334


def _params(sem):
    return pltpu.CompilerParams(dimension_semantics=sem, vmem_limit_bytes=V7X_VMEM_LIMIT_BYTES)


def _dot(a, b):
    return jnp.dot(a.astype(BF16), b.astype(BF16), preferred_element_type=F32)


def _dot_nt(a, b):
    return lax.dot_general(a.astype(BF16), b.astype(BF16), (((1,), (1,)), ((), ())),
                           preferred_element_type=F32)


def _split_dot(x, w_bf16):
    hi = x.astype(BF16)
    lo = (x - hi.astype(F32)).astype(BF16)
    return (jnp.dot(hi, w_bf16, preferred_element_type=F32)
            + jnp.dot(lo, w_bf16, preferred_element_type=F32))


def _group_sums(x, ones_bd):
    parts = [_split_dot(x[:, p * LANES:(p + 1) * LANES], ones_bd) for p in range(x.shape[1] // LANES)]
    return jnp.concatenate(parts, axis=1)


def _ones_blocks(group):
    ri = lax.broadcasted_iota(jnp.int32, (LANES, LANES), 0)
    ci = lax.broadcasted_iota(jnp.int32, (LANES, LANES), 1)
    return jnp.where((ri // group) == (ci // group), 1.0, 0.0).astype(BF16)


def _sigmoid(x):
    return 1.0 / (1.0 + jnp.exp(-x))


def _cumsum_rows(x, reverse):
    L = x.shape[0]
    row = lax.broadcasted_iota(jnp.int32, x.shape, 0)
    s = 1
    while s < L:
        if reverse:
            x = x + jnp.where(row < L - s, pltpu.roll(x, L - s, 0), 0.0)
        else:
            x = x + jnp.where(row >= s, pltpu.roll(x, s, 0), 0.0)
        s *= 2
    return x


def _interleave(gens, first_round=None):
    results = [None] * len(gens)
    first_round = first_round or [0] * len(gens)
    active = list(enumerate(gens))
    rnd = 0
    while active:
        still = []
        for idx, g in active:
            if rnd < first_round[idx]:
                still.append((idx, g))
                continue
            try:
                next(g)
                still.append((idx, g))
            except StopIteration as e:
                results[idx] = e.value
        active = still
        rnd += 1
    return results


def _mm_body(x_ref, w_ref, o_ref):
    o_ref[...] = _dot(x_ref[...], w_ref[...]).astype(o_ref.dtype)


def matmul(x, w, *, tm, tn, out_dtype=F32):
    M, K = x.shape
    _, N = w.shape
    return pl.pallas_call(
        _mm_body,
        grid=(M // tm, N // tn),
        in_specs=[pl.BlockSpec((tm, K), lambda i, j: (i, 0)),
                  pl.BlockSpec((K, tn), lambda i, j: (0, j))],
        out_specs=pl.BlockSpec((tm, tn), lambda i, j: (i, j)),
        out_shape=jax.ShapeDtypeStruct((M, N), out_dtype),
        compiler_params=_params(("parallel", "parallel")),
        name="matmul",
    )(x, w)


def _in_proj_body(x_ref, mod_ref, g_ref, w_ref, mu_ref, o_ref, h_scr, *, width):
    j = pl.program_id(1)

    @pl.when(j == 0)
    def _():
        x = x_ref[...]
        y = x * lax.rsqrt(jnp.mean(x * x, axis=-1, keepdims=True) + NORM_EPS) * g_ref[...]
        m = mod_ref[0]
        h_scr[...] = (y * (1.0 + m[:, D_MODEL:2 * D_MODEL]) + m[:, 0:D_MODEL]).astype(BF16)

    @pl.when(j < PROJ_PLAIN_TILES)
    def _():
        o_ref[...] = jnp.dot(h_scr[...], w_ref[...], preferred_element_type=F32)

    @pl.when(j >= PROJ_PLAIN_TILES)
    def _():
        acc = jnp.dot(h_scr[...], w_ref[...], preferred_element_type=F32)
        tm = acc.shape[0]
        pos = lax.broadcasted_iota(jnp.int32, acc.shape, 0) & (width - 1)
        prev = jnp.where(pos == 0, 0.0, pltpu.roll(acc, 1, 0))
        nxt = jnp.where(pos == width - 1, 0.0, pltpu.roll(acc, tm - 1, 0))
        mu = mu_ref[...]
        o_ref[...] = acc + mu[0:1, :] * (prev - acc) + mu[1:2, :] * (nxt - acc)


def in_proj(x, mod, norm_g, w_perm, mu_perm, *, width, rows_per_mod, tm=1024):
    M, D = x.shape
    assert tm % width == 0 and rows_per_mod % tm == 0
    per = rows_per_mod // tm
    return pl.pallas_call(
        functools.partial(_in_proj_body, width=width),
        grid=(M // tm, IN_COLS // PROJ_TN),
        in_specs=[pl.BlockSpec((tm, D), lambda i, j: (i, 0), pipeline_mode=pl.Buffered(1)),
                  pl.BlockSpec((1, 1, N_MOD * D), lambda i, j: (i // per, 0, 0)),
                  pl.BlockSpec((1, D), lambda i, j: (0, 0)),
                  pl.BlockSpec((D, PROJ_TN), lambda i, j: (0, j)),
                  pl.BlockSpec((2, PROJ_TN), lambda i, j: (0, j))],
        out_specs=pl.BlockSpec((tm, PROJ_TN), lambda i, j: (i, j)),
        out_shape=jax.ShapeDtypeStruct((M, IN_COLS), F32),
        scratch_shapes=[pltpu.VMEM((tm, D), BF16)],
        compiler_params=_params(("parallel", "arbitrary")),
        name="in_proj",
    )(x, mod, norm_g, w_perm, mu_perm)


def _block_diag(x, lane_lo):
    return jnp.concatenate([jnp.where(lane_lo, x, 0.0), jnp.where(lane_lo, 0.0, x)], axis=0)


def _rwkv_pair_chunk(q, lw, kd, v, kk, a, reverse):
    L = q.shape[0]
    lane = lax.broadcasted_iota(jnp.int32, (L, LANES), 1)
    lane_lo = lane < RW_HEAD
    c = _cumsum_rows(lw, reverse)
    c_last = c[0:1, :] if reverse else c[L - 1:L, :]
    g_inc = jnp.exp(c)
    g_exc = jnp.exp(c - lw)
    g_inv = jnp.exp(-c)
    g_rem = jnp.exp(c_last - c)
    beta = kk * a
    ab = _block_diag(-kk * g_exc, lane_lo)
    qb = _block_diag(q * g_inc, lane_lo)
    bt = _block_diag(beta * g_inv, lane_lo)
    kt = _block_diag(kd * g_inv, lane_lo)
    vb = _block_diag(v, lane_lo)
    bh = _block_diag(beta * g_rem, lane_lo)
    kh = _block_diag(kd * g_rem, lane_lo)

    n2 = 2 * L
    ri = lax.broadcasted_iota(jnp.int32, (n2, n2), 0)
    ci = lax.broadcasted_iota(jnp.int32, (n2, n2), 1)
    rt = ri % L
    ct = ci % L
    if reverse:
        strict = rt < ct
        incl = rt <= ct
    else:
        strict = rt > ct
        incl = rt >= ct

    aa = _dot_nt(jnp.concatenate([ab, qb], axis=0), jnp.concatenate([bt, kt], axis=0))
    yield
    a_ab = jnp.where(strict, aa[:n2, :n2], 0.0)
    a_ak = jnp.where(strict, aa[:n2, n2:], 0.0)
    a_qb = jnp.where(incl, aa[n2:, :n2], 0.0)
    a_qk = jnp.where(incl, aa[n2:, n2:], 0.0)

    rb = rt // 2
    cb = ct // 2
    P = jnp.where(ri == ci, 1.0, 0.0) + jnp.where(rb == cb, a_ab, 0.0)
    av = _dot(a_ak, vb)
    s = 2
    while s < L:
        rs, cs = rb, cb
        rb = rb // 2
        cb = cb // 2
        coupling = jnp.where(jnp.logical_and(rb == cb, rs != cs), a_ab, 0.0)
        pc = _dot(P, coupling)
        yield
        P = P + _dot(pc, P)
        yield
        s *= 2

    wu_u0 = _dot(P, jnp.concatenate([ab, av], axis=1))
    yield
    rhs = jnp.concatenate([wu_u0, jnp.concatenate([jnp.zeros_like(vb), vb], axis=1)], axis=0)
    qy = _dot(jnp.concatenate([a_qb, a_qk], axis=1), rhs)
    st = _dot(jnp.concatenate([bh, kh], axis=0).T, rhs)
    yield
    q_eff = qb + qy[:, :LANES]
    g_last = jnp.exp(jnp.broadcast_to(c_last, (n2, LANES)))
    m_mat = jnp.where(ri == ci, g_last, 0.0) + st[:, :LANES]
    return jnp.concatenate([q_eff, m_mat], axis=0), jnp.concatenate([qy[:, LANES:], st[:, LANES:]], axis=0)


def _rwkv_apply_state(lhs, add, H):
    L = RW_CHUNK
    out = _dot(lhs, H) + add
    return out[:L] + out[L:2 * L], out[2 * L:]


def _rwkv_prep(r, k, v, lora, vec, w2pad, a2pad, ones_bd, d, want_other_kd):
    lora_in_w = lora[:, 0:LANES]
    lora_in_a = lora[:, LANES:2 * LANES]
    kk = k * vec[4:5, :]
    ss = _group_sums(kk * kk, ones_bd)
    kk = kk * lax.rsqrt(jnp.maximum(ss, 1e-24))
    u = vec[d:d + 1, :] + jnp.dot(jnp.tanh(lora_in_w).astype(BF16), w2pad[d], preferred_element_type=F32)
    lw = -DECAY_SCALE * _sigmoid(u)
    a = _sigmoid(vec[2 + d:3 + d, :] + jnp.dot(lora_in_a.astype(BF16), a2pad[d], preferred_element_type=F32))
    kd = k * (1.0 + (a - 1.0) * vec[5:6, :])
    out = dict(r=r, k=k, v=v, kk=kk, lw=lw, a=a, kd=kd)
    if want_other_kd:
        o = 1 - d
        a_o = _sigmoid(vec[2 + o:3 + o, :] + jnp.dot(lora_in_a.astype(BF16), a2pad[o], preferred_element_type=F32))
        out['kd_other'] = k * (1.0 + (a_o - 1.0) * vec[5:6, :])
    return out


def _scan_body(rf_ref, kf_ref, vf_ref, lf_ref, rb_ref, kb_ref, vb_ref, lb_ref,
               qf_ref, ff_ref, if_ref, qb_ref, fb_ref, ib_ref,
               vec_ref, w2_ref, a2_ref, g2_ref, hglb_ref, h0_ref, s0_ref,
               yf_ref, yb_ref, bonus_ref, g_ref, of_ref, ob_ref, hfin_ref, sfin_ref, h_scr, s_scr):
    i = pl.program_id(1)
    n = pl.num_programs(1)

    @pl.when(i == 0)
    def _():
        h_scr[...] = h0_ref[0]
        s_scr[...] = s0_ref[0]

    vec = vec_ref[...]
    ones_bd = _ones_blocks(RW_HEAD)
    fw = _rwkv_prep(rf_ref[0], kf_ref[0], vf_ref[0], lf_ref[0], vec, w2_ref, a2_ref, ones_bd, 0, True)
    bw = _rwkv_prep(rb_ref[0], kb_ref[0], vb_ref[0], lb_ref[0], vec, w2_ref, a2_ref, ones_bd, 1, False)

    rk = fw['r'] * vec[6:7, :]
    bonus_ref[0] = _group_sums(rk * (fw['kd'] + fw['kd_other']), ones_bd) * fw['v']
    gd = lf_ref[0][:, 2 * LANES:3 * LANES]
    g_ref[0] = jnp.dot(_sigmoid(gd).astype(BF16), g2_ref[...], preferred_element_type=F32)

    chunk_order = ((0, 1), (1, 0))
    chains = []
    for d, t in enumerate((fw, bw)):
        for half in chunk_order[d]:
            rows = slice(half * RW_CHUNK, (half + 1) * RW_CHUNK)
            for p in range(RW_PAIRS):
                sl = slice(p * LANES, (p + 1) * LANES)
                chains.append(_rwkv_pair_chunk(t['r'][rows, sl], t['lw'][rows, sl], t['kd'][rows, sl],
                                               t['v'][rows, sl], t['kk'][rows, sl], t['a'][rows, sl], d == 1))
    n_rw = len(chains)
    for d, (q_ref, f_ref, v_ref) in enumerate(((qf_ref, ff_ref, if_ref), (qb_ref, fb_ref, ib_ref))):
        for h in range(HG_HEADS):
            sl = slice(h * LANES, (h + 1) * LANES)
            chains.append(_gla_head_chunk(q_ref[0, :, sl], f_ref[0, :, sl], v_ref[0, :, sl],
                                          hglb_ref[d:d + 1, sl], s_scr[d, h], d == 1))
    outs = _interleave(chains, [0] * n_rw + [HG_FIRST_ROUND] * (len(chains) - n_rw))

    states =[[h_scr[d, p] for p in range(RW_PAIRS)] for d in range(2)]
    for step in range(2):
        for d, y_ref in enumerate((yf_ref, yb_ref)):
            half = chunk_order[d][step]
            for p in range(RW_PAIRS):
                lhs, add = outs[(d * 2 + step) * RW_PAIRS + p]
                y, states[d][p] = _rwkv_apply_state(lhs, add, states[d][p])
                y_ref[0, half * RW_CHUNK:(half + 1) * RW_CHUNK, p * LANES:(p + 1) * LANES] = y
    for d in range(2):
        for p in range(RW_PAIRS):
            h_scr[d, p] = states[d][p]

    for d, o_ref in enumerate((of_ref, ob_ref)):
        for h in range(HG_HEADS):
            o, st_new = outs[n_rw + d * HG_HEADS + h]
            o_ref[0, :, h * LANES:(h + 1) * LANES] = o
            s_scr[d, h] = st_new

    @pl.when(i == n - 1)
    def _():
        hfin_ref[0] = h_scr[...]
        sfin_ref[0] = s_scr[...]


def recurrent_scans(proj, vec, w2pad, a2pad, g2, hg_lb, h0, s0t):
    B, T, _ = proj.shape
    n = T // HG_L
    fwd = lambda c: (lambda b, i: (b, i, c))
    bwd = lambda c: (lambda b, i: (b, n - 1 - i, c))
    wide = lambda fn: pl.BlockSpec((1, HG_L, RW_DIM), fn)
    lora = lambda fn: pl.BlockSpec((1, HG_L, RW_LORA_COLS), fn)
    const = lambda shape: pl.BlockSpec(shape, lambda b, i: (0,) * len(shape))
    rw_st = pl.BlockSpec((1, 2, RW_PAIRS, LANES, LANES), lambda b, i: (b, 0, 0, 0, 0))
    hg_st = pl.BlockSpec((1, 2, HG_HEADS, HG_DV, HG_DK), lambda b, i: (b, 0, 0, 0, 0))
    out_sd = jax.ShapeDtypeStruct((B, T, RW_DIM), F32)
    return pl.pallas_call(
        _scan_body,
        grid=(B, n),
        in_specs=[wide(fwd(COL_RW_R)), wide(fwd(COL_RW_K)), wide(fwd(COL_RW_V)), lora(fwd(COL_RW_LORA)),
                  wide(bwd(COL_RW_R)), wide(bwd(COL_RW_K)), wide(bwd(COL_RW_V)), lora(bwd(COL_RW_LORA)),
                  wide(fwd(COL_HG_Q)), wide(fwd(COL_HG_FF)), wide(fwd(COL_HG_I)),
                  wide(bwd(COL_HG_Q)), wide(bwd(COL_HG_FB)), wide(bwd(COL_HG_I)),
                  const((8, RW_DIM)), const((2, LANES, RW_DIM)), const((2, LANES, RW_DIM)),
                  const((LANES, RW_DIM)), const((2, HG_KDIM)), rw_st, hg_st],
        out_specs=[wide(fwd(0)), wide(bwd(0)), wide(fwd(0)), wide(fwd(0)), wide(fwd(0)), wide(bwd(0)),
                   rw_st, hg_st],
        out_shape=[out_sd] * 6 + [jax.ShapeDtypeStruct((B, 2, RW_PAIRS, LANES, LANES), F32),
                                  jax.ShapeDtypeStruct((B, 2, HG_HEADS, HG_DV, HG_DK), F32)],
        scratch_shapes=[pltpu.VMEM((2, RW_PAIRS, LANES, LANES), F32),
                        pltpu.VMEM((2, HG_HEADS, HG_DV, HG_DK), F32)],
        compiler_params=_params(("parallel", "arbitrary")),
        name="recurrent_scans",
    )(*([proj] * 14), vec, w2pad, a2pad, g2, hg_lb, h0, s0t)


def rwkv_pack_params(p):
    z = jnp.zeros((RW_HEAD, RW_DIM), F32)
    w2pad = jnp.stack([jnp.concatenate([p['rw_w2'][0], z], 0), jnp.concatenate([z, p['rw_w2'][1]], 0)]).astype(BF16)
    a2pad = jnp.stack([jnp.concatenate([p['rw_a2'][0], z], 0), jnp.concatenate([z, p['rw_a2'][1]], 0)]).astype(BF16)
    vec = jnp.stack([p['rw_w0'][0], p['rw_w0'][1], p['rw_a0'][0], p['rw_a0'][1], p['rw_k_k'], p['rw_k_a'],
                     p['rw_r_k'].reshape(RW_DIM), jnp.zeros((RW_DIM,), F32)])
    return vec, w2pad, a2pad, p['rw_g2'].astype(BF16)


def rwkv_state_to_blockdiag(s):
    B = s.shape[0]
    h = jnp.swapaxes(s, -1, -2).reshape(B, 2, RW_PAIRS, 2, RW_HEAD, RW_HEAD)
    z = jnp.zeros_like(h[:, :, :, 0])
    top = jnp.concatenate([h[:, :, :, 0], z], axis=-1)
    bot = jnp.concatenate([z, h[:, :, :, 1]], axis=-1)
    return jnp.concatenate([top, bot], axis=-2)


def rwkv_blockdiag_to_state(hb):
    B = hb.shape[0]
    h0 = hb[:, :, :, :RW_HEAD, :RW_HEAD]
    h1 = hb[:, :, :, RW_HEAD:, RW_HEAD:]
    h = jnp.stack([h0, h1], axis=3).reshape(B, 2, RW_HEADS, RW_HEAD, RW_HEAD)
    return jnp.swapaxes(h, -1, -2)


def _gla_head_chunk(q_raw, fl, v, lb, St, reverse):
    L = q_raw.shape[0]
    q = q_raw * _sigmoid(q_raw)
    f = lb + (1.0 - lb) * _sigmoid(fl)
    k = 1.0 - f
    g = jnp.log(f)
    b = _cumsum_rows(g, reverse)
    b_last = b[0:1, :] if reverse else b[L - 1:L, :]
    row = lax.broadcasted_iota(jnp.int32, (L, LANES), 0)
    ri = lax.broadcasted_iota(jnp.int32, (L, L), 0)
    ci = lax.broadcasted_iota(jnp.int32, (L, L), 1)

    prods = [(_dot_nt(q, k), ri == ci)]
    fk = b
    half = 1
    while half < L:
        size = 2 * half
        upper = (row & half) != 0
        if reverse:
            bref = jnp.where(upper, fk, pltpu.roll(fk, L - half, 0))
            is_q = jnp.logical_not(upper)
        else:
            bref = jnp.where(upper, pltpu.roll(fk, half, 0), fk)
            is_q = upper
        e = jnp.exp(jnp.where(is_q, b - bref, bref - b))
        qh = jnp.where(is_q, q * e, 0.0)
        kh = jnp.where(is_q, 0.0, k * e)
        mask = None if size == L else (ri // size) == (ci // size)
        prods.append((_dot_nt(qh, kh), mask))
        yield
        if reverse:
            fk = jnp.where(upper, pltpu.roll(fk, half, 0), fk)
        else:
            fk = jnp.where(upper, fk, pltpu.roll(fk, L - half, 0))
        half = size
    att = None
    for pr, mask in prods:
        term = pr if mask is None else jnp.where(mask, pr, 0.0)
        att = term if att is None else att + term
    o = _dot(att, v) + _dot_nt(q * jnp.exp(b), St)
    kst = k * jnp.exp(b_last - b)
    st_new = St * jnp.exp(b_last) + _dot(v.T, kst)
    yield
    return o, st_new


def _branch_merge_body(yf_ref, yb_ref, bonus_ref, gr_ref, of_ref, ob_ref, gh_ref, pgr_ref, pgh_ref,
                       vec_ref, wr_ref, wh_ref, o_ref):
    vec = vec_ref[...]
    y = yf_ref[...] + yb_ref[...]
    ones_head = _ones_blocks(RW_HEAD)
    mean = _group_sums(y, ones_head) * (1.0 / RW_HEAD)
    dlt = y - mean
    var = _group_sums(dlt * dlt, ones_head) * (1.0 / RW_HEAD)
    yn = dlt * lax.rsqrt(var + RW_LN_EPS) * vec[0:1, :] + vec[1:2, :]
    out_r = (yn + bonus_ref[...]) * gr_ref[...]
    o = of_ref[...] + ob_ref[...]
    ms = _group_sums(o * o, _ones_blocks(LANES)) * (1.0 / HG_DV)
    gh = gh_ref[...]
    out_h = o * lax.rsqrt(ms + NORM_EPS) * vec[2:3, :] * (gh * _sigmoid(gh))
    br = _dot(out_r, wr_ref[...])
    bh = _dot(out_h, wh_ref[...])
    o_ref[...] = (_sigmoid(pgr_ref[...]) * br + _sigmoid(pgh_ref[...]) * bh).astype(o_ref.dtype)


def branch_merge(yf, yb, bonus, gr, of, ob, proj, vec, w_br_r, w_br_h, *, tm=256):
    M = yf.shape[0]
    row = lambda w, c: pl.BlockSpec((tm, w), lambda i: (i, c))
    const = lambda shape: pl.BlockSpec(shape, lambda i: (0,) * len(shape), pipeline_mode=pl.Buffered(1))
    return pl.pallas_call(
        _branch_merge_body,
        grid=(M // tm,),
        in_specs=[row(RW_DIM, 0), row(RW_DIM, 0), row(RW_DIM, 0), row(RW_DIM, 0),
                  row(HG_VDIM, 0), row(HG_VDIM, 0), row(HG_VDIM, COL_HG_G),
                  row(D_MODEL, COL_GATE_R), row(D_MODEL, COL_GATE_H),
                  const((8, RW_DIM)), const((RW_DIM, D_MODEL)), const((HG_VDIM, D_MODEL))],
        out_specs=pl.BlockSpec((tm, D_MODEL), lambda i: (i, 0)),
        out_shape=jax.ShapeDtypeStruct((M, D_MODEL), BF16),
        compiler_params=_params(("parallel",)),
        name="branch_merge",
    )(yf, yb, bonus, gr, of, ob, proj, proj, proj, vec, w_br_r, w_br_h)


def _out_proj_body(m_ref, x_ref, mod_ref, g_ref, w_ref, wr_ref, x1_ref, h2_ref, lg_ref):
    mod = mod_ref[0]
    D = D_MODEL
    x1 = x_ref[...] + mod[:, 2 * D:3 * D] * jnp.dot(m_ref[...], w_ref[...], preferred_element_type=F32)
    x1_ref[...] = x1
    y = x1 * lax.rsqrt(jnp.mean(x1 * x1, axis=-1, keepdims=True) + NORM_EPS) * g_ref[...]
    h2 = (y * (1.0 + mod[:, 4 * D:5 * D]) + mod[:, 3 * D:4 * D]).astype(BF16)
    h2_ref[...] = h2
    lg_ref[...] = jnp.dot(h2, wr_ref[...], preferred_element_type=F32)


def out_proj(merged, x, mod, norm_g, w_out, w_router, *, rows_per_mod, tm=512):
    M, D = x.shape
    per = rows_per_mod // tm
    return pl.pallas_call(
        _out_proj_body,
        grid=(M // tm,),
        in_specs=[pl.BlockSpec((tm, D), lambda i: (i, 0)),
                  pl.BlockSpec((tm, D), lambda i: (i, 0)),
                  pl.BlockSpec((1, 1, N_MOD * D), lambda i: (i // per, 0, 0)),
                  pl.BlockSpec((1, D), lambda i: (0, 0)),
                  pl.BlockSpec((D, D), lambda i: (0, 0), pipeline_mode=pl.Buffered(1)),
                  pl.BlockSpec((D, LANES), lambda i: (0, 0), pipeline_mode=pl.Buffered(1))],
        out_specs=[pl.BlockSpec((tm, D), lambda i: (i, 0)),
                   pl.BlockSpec((tm, D), lambda i: (i, 0)),
                   pl.BlockSpec((tm, LANES), lambda i: (i, 0))],
        out_shape=[jax.ShapeDtypeStruct((M, D), F32), jax.ShapeDtypeStruct((M, D), BF16),
                   jax.ShapeDtypeStruct((M, LANES), F32)],
        compiler_params=_params(("parallel",)),
        name="out_proj",
    )(merged, x, mod, norm_g, w_out, w_router)


def _experts_body(x_ref, wg_ref, wu_ref, wd_ref, val_ref, o_ref):
    x = x_ref[0]
    gate = jnp.dot(x, wg_ref[0], preferred_element_type=F32)
    up = jnp.dot(x, wu_ref[0], preferred_element_type=F32)
    hid = (gate * _sigmoid(gate) * up).astype(BF16)
    yo = jnp.dot(hid, wd_ref[0], preferred_element_type=F32)
    scale = jnp.concatenate([val_ref[0]] * (D_MODEL // LANES), axis=1)
    o_ref[0] = (yo * scale).astype(o_ref.dtype)


def moe_experts(xe, w_gate, w_up, w_down, vals, *, tm=256):
    E, M, D = xe.shape
    tm = min(tm, M)
    return pl.pallas_call(
        _experts_body,
        grid=(E, M // tm),
        in_specs=[pl.BlockSpec((1, tm, D), lambda e, i: (e, i, 0)),
                  pl.BlockSpec((1, D, EXPERT_FF), lambda e, i: (e, 0, 0)),
                  pl.BlockSpec((1, D, EXPERT_FF), lambda e, i: (e, 0, 0)),
                  pl.BlockSpec((1, EXPERT_FF, D), lambda e, i: (e, 0, 0)),
                  pl.BlockSpec((1, tm, LANES), lambda e, i: (e, i, 0))],
        out_specs=pl.BlockSpec((1, tm, D), lambda e, i: (e, i, 0)),
        out_shape=jax.ShapeDtypeStruct((E, M, D), BF16),
        compiler_params=_params(("parallel", "parallel")),
        name="moe_experts",
    )(xe, w_gate, w_up, w_down, vals)


def _moe_combine_body(yo_ref, idx_ref, x1_ref, gt_ref, o_ref):
    T = x1_ref.shape[1]
    cap = idx_ref.shape[2]
    rows = min(T, 512)
    o_ref[0] = jnp.zeros(o_ref.shape[1:], F32)

    def one_expert(e, carry):
        slot_tok = idx_ref[0, pl.ds(e, 1), :]
        yo_e = yo_ref[e, 0]
        for r in range(T // rows):
            tok = lax.broadcasted_iota(jnp.int32, (rows, cap), 0) + r * rows
            onehot = jnp.where(tok == slot_tok, 1.0, 0.0).astype(BF16)
            o_ref[0, r * rows:(r + 1) * rows, :] += jnp.dot(onehot, yo_e, preferred_element_type=F32)
        return carry

    lax.fori_loop(0, N_EXPERTS, one_expert, 0)
    o_ref[0] = x1_ref[0] + gt_ref[0] * o_ref[0]


def moe_combine(yo, idx, x1, gt2):
    E, B, cap, D = yo.shape
    T = x1.shape[1]
    dn = min(D, COMBINE_TILE_ELEMS // T)
    per = B // gt2.shape[0]
    return pl.pallas_call(
        _moe_combine_body,
        grid=(B, D // dn),
        in_specs=[pl.BlockSpec((E, 1, cap, dn), lambda b, j: (0, b, 0, j)),
                  pl.BlockSpec((1, E, cap), lambda b, j: (b, 0, 0)),
                  pl.BlockSpec((1, T, dn), lambda b, j: (b, 0, j)),
                  pl.BlockSpec((1, 1, dn), lambda b, j: (b // per, 0, j))],
        out_specs=pl.BlockSpec((1, T, dn), lambda b, j: (b, 0, j)),
        out_shape=jax.ShapeDtypeStruct((B, T, D), F32),
        compiler_params=_params(("parallel", "parallel")),
        name="moe_combine",
    )(yo, idx, x1, gt2)


def _expert_choice_moe(h2, logits, x1, gt2, p):
    B, T, D = h2.shape
    cap = CAPACITY_FACTOR * T // N_EXPERTS
    aff = jax.nn.softmax(logits[..., :N_EXPERTS], axis=-1)
    vals, idx = lax.top_k(jnp.swapaxes(aff, 1, 2), cap)
    bidx = jnp.arange(B)[:, None, None]
    xg = h2[bidx, idx]
    xe = jnp.swapaxes(xg, 0, 1).reshape(N_EXPERTS, B * cap, D)
    ve = jnp.broadcast_to(jnp.swapaxes(vals, 0, 1).reshape(N_EXPERTS, B * cap, 1), (N_EXPERTS, B * cap, LANES))
    yo = moe_experts(xe, p['moe_w_gate'], p['moe_w_up'], p['moe_w_down'], ve)
    return moe_combine(yo.reshape(N_EXPERTS, B, cap, D), idx, x1, gt2)


def _layer(x, mod, s_rw, s_hg, width, p):
    B, T, D = x.shape
    rows_per_mod = (B * T) // mod.shape[0]
    xf = x.reshape(B * T, D)
    proj = in_proj(xf, mod, p['norm1_g'], p['w_in'], p['mu'], width=width, rows_per_mod=rows_per_mod)
    proj3 = proj.reshape(B, T, IN_COLS)
    yf, yb, bonus, gr, of, ob, hfin, sfin = recurrent_scans(
        proj3, p['rw_vec'], p['rw_w2pad'], p['rw_a2pad'], p['rw_g2'], p['hg_lb'],
        rwkv_state_to_blockdiag(s_rw), jnp.swapaxes(s_hg, -1, -2))
    flat = lambda u: u.reshape(B * T, u.shape[-1])
    merged = branch_merge(flat(yf), flat(yb), flat(bonus), flat(gr), flat(of), flat(ob), proj,
                          p['br_vec'], p['w_br_rwkv'], p['w_br_hgrn'])
    x1, h2, logits = out_proj(merged, xf, mod, p['norm2_g'], p['w_out'], p['moe_router'],
                              rows_per_mod=rows_per_mod)
    x2 = _expert_choice_moe(h2.reshape(B, T, D), logits.reshape(B, T, LANES), x1.reshape(B, T, D),
                            mod[:, :, 5 * D:6 * D], p)
    return x2, rwkv_blockdiag_to_state(hfin), jnp.swapaxes(sfin, -1, -2)


def _rmsnorm(x, g, eps=NORM_EPS):
    return x * lax.rsqrt(jnp.mean(x * x, axis=-1, keepdims=True) + eps) * g


def kernel(x_prompt, x_sample, c, state_rwkv, state_hgrn, c_ctx, norm1_g, norm2_g, final_norm_g, ada_w, ada_b, w_in, rw_mu, rw_w0, rw_w2, rw_a0, rw_a2, rw_g2, rw_k_k, rw_k_a, rw_r_k, rw_ln_w, rw_ln_b, hg_lb, hg_norm_g, w_br_rwkv, w_br_hgrn, w_out, moe_router, moe_w_gate, moe_w_up, moe_w_down):
    depth = norm1_g.shape[0]
    D = D_MODEL
    lb_all = jnp.cumsum(jax.nn.softmax(hg_lb.astype(F32), axis=0), axis=0)
    bp = x_prompt.shape[0]
    bs = x_sample.shape[0]
    xp, xs = x_prompt, x_sample
    new_rw, new_hg = [], []
    for l in range(depth):
        wl = w_in[l]
        w_perm = jnp.concatenate([wl[:, RW_COLS + HG_COLS:], wl[:, RW_COLS:RW_COLS + HG_COLS], wl[:, :RW_COLS]],
                                 axis=1).astype(BF16)
        mu_perm = jnp.concatenate([jnp.zeros((2, GATE_COLS + HG_COLS), F32), rw_mu[l]], axis=1)
        pr = dict(rw_w0=rw_w0[l], rw_w2=rw_w2[l], rw_a0=rw_a0[l], rw_a2=rw_a2[l], rw_g2=rw_g2[l],
                  rw_k_k=rw_k_k[l], rw_k_a=rw_k_a[l], rw_r_k=rw_r_k[l])
        rw_vec, rw_w2pad, rw_a2pad, rw_g2b = rwkv_pack_params(pr)
        zrow = jnp.zeros((RW_DIM,), F32)
        br_vec = jnp.stack([rw_ln_w[l], rw_ln_b[l], hg_norm_g[l], zrow, zrow, zrow, zrow, zrow])
        p = dict(norm1_g=norm1_g[l][None, :], norm2_g=norm2_g[l][None, :], w_in=w_perm, mu=mu_perm,
                 rw_vec=rw_vec, rw_w2pad=rw_w2pad, rw_a2pad=rw_a2pad, rw_g2=rw_g2b,
                 hg_lb=lb_all[l], br_vec=br_vec,
                 w_br_rwkv=w_br_rwkv[l].astype(BF16), w_br_hgrn=w_br_hgrn[l].astype(BF16),
                 w_out=w_out[l].astype(BF16),
                 moe_router=jnp.pad(moe_router[l], ((0, 0), (0, LANES - N_EXPERTS))).astype(BF16),
                 moe_w_gate=moe_w_gate[l].astype(BF16), moe_w_up=moe_w_up[l].astype(BF16),
                 moe_w_down=moe_w_down[l].astype(BF16))
        cond = jnp.concatenate([c_ctx[None, :], c], axis=0)
        cond = jnp.pad(jax.nn.silu(cond), ((0, 16 - 1 - bs), (0, 0)))
        mod = matmul(cond, ada_w[l], tm=16, tn=1024)[:1 + bs] + ada_b[l]
        mod = mod.reshape(1 + bs, 1, N_MOD * D)
        zero_rw = jnp.zeros((bp, 2, RW_HEADS, RW_HEAD, RW_HEAD), F32)
        zero_hg = jnp.zeros((bp, 2, HG_HEADS, HG_DK, HG_DV), F32)
        xp, s_rw, s_hg = _layer(xp, mod[:1], zero_rw, zero_hg, xp.shape[1], p)
        new_rw.append(s_rw)
        new_hg.append(s_hg)
        xs, _, _ = _layer(xs, mod[1:], state_rwkv[:, l].astype(F32), state_hgrn[:, l].astype(F32), GRID_W, p)
    y_prompt = _rmsnorm(xp, final_norm_g)
    y_sample = _rmsnorm(xs, final_norm_g)
    return (y_prompt, y_sample, jnp.stack(new_rw, axis=1), jnp.stack(new_hg, axis=1))
```

```python
import functools

import jax
import jax.numpy as jnp
from jax import lax
from jax.experimental import pallas as pl
from jax.experimental.pallas import tpu as pltpu

F32 = jnp.float32
BF16 = jnp.bfloat16

D_MODEL = 2048
GRID_W = 64
RW_HEADS = 16
RW_HEAD = 64
RW_DIM = RW_HEADS * RW_HEAD
DECAY_LORA = 64
AAA_LORA = 64
GATE_LORA = 128
RW_LORA_COLS = 2 * DECAY_LORA + 2 * AAA_LORA + GATE_LORA
RW_LN_EPS = 64e-5
HG_HEADS = 8
HG_DK = 128
HG_DV = 128
HG_KDIM = HG_HEADS * HG_DK
HG_VDIM = HG_HEADS * HG_DV
N_EXPERTS = 16
EXPERT_FF = 1024
CAPACITY_FACTOR = 2
NORM_EPS = 1e-6
N_MOD = 6
RW_COLS = 3 * RW_DIM + RW_LORA_COLS
HG_COLS = 3 * HG_KDIM + 2 * HG_VDIM
GATE_COLS = 2 * D_MODEL
IN_COLS = RW_COLS + HG_COLS + GATE_COLS

LANES = 128
V7X_VMEM_LIMIT_BYTES = 52 * 1024 * 1024

V7X_MXU_WIDTH = 256
PROJ_TN = 5 * V7X_MXU_WIDTH
PROJ_COLS = -(-IN_COLS // PROJ_TN) * PROJ_TN
PROJ_PLAIN_TILES = (GATE_COLS + HG_COLS) // PROJ_TN
COL_GATE_R, COL_GATE_H = 0, 1
COL_HG_Q, COL_HG_FF, COL_HG_FB, COL_HG_I, COL_HG_G = 4, 5, 6, 7, 8
COL_RW_R, COL_RW_K, COL_RW_V = 9, 10, 11
COL_RW_LORA = (GATE_COLS + HG_COLS + 3 * RW_DIM) // RW_LORA_COLS

RW_CHUNK = 64
RW_PAIRS = RW_DIM // LANES
HG_L = 128
HG_FIRST_ROUND = 4
COMBINE_TILE_ELEMS = 1024 * 1024
DECAY_SCALE = 0.6065306597126334
LOG2_E = 1.4426950408889634


def _params(sem):
    return pltpu.CompilerParams(dimension_semantics=sem, vmem_limit_bytes=V7X_VMEM_LIMIT_BYTES)


def _dot(a, b):
    return jnp.dot(a.astype(BF16), b.astype(BF16), preferred_element_type=F32)


def _dot_nt(a, b):
    return lax.dot_general(a.astype(BF16), b.astype(BF16), (((1,), (1,)), ((), ())),
                           preferred_element_type=F32)


def _split_dot(x, w_bf16):
    hi = x.astype(BF16)
    lo = (x - hi.astype(F32)).astype(BF16)
    return (jnp.dot(hi, w_bf16, preferred_element_type=F32)
            + jnp.dot(lo, w_bf16, preferred_element_type=F32))


def _group_sums(x, ones_bd):
    parts = [_split_dot(x[:, p * LANES:(p + 1) * LANES], ones_bd) for p in range(x.shape[1] // LANES)]
    return jnp.concatenate(parts, axis=1)


def _ones_blocks(group):
    ri = lax.broadcasted_iota(jnp.int32, (LANES, LANES), 0)
    ci = lax.broadcasted_iota(jnp.int32, (LANES, LANES), 1)
    return jnp.where((ri // group) == (ci // group), 1.0, 0.0).astype(BF16)


def _sigmoid(x):
    return 1.0 / (1.0 + jnp.exp(-x))


def _cumsum_rows(x, reverse):
    L = x.shape[0]
    row = lax.broadcasted_iota(jnp.int32, x.shape, 0)
    s = 1
    while s < L:
        if reverse:
            x = x + jnp.where(row < L - s, pltpu.roll(x, L - s, 0), 0.0)
        else:
            x = x + jnp.where(row >= s, pltpu.roll(x, s, 0), 0.0)
        s *= 2
    return x


def _interleave(gens, first_round=None):
    results = [None] * len(gens)
    first_round = first_round or [0] * len(gens)
    active = list(enumerate(gens))
    rnd = 0
    while active:
        still = []
        for idx, g in active:
            if rnd < first_round[idx]:
                still.append((idx, g))
                continue
            try:
                next(g)
                still.append((idx, g))
            except StopIteration as e:
                results[idx] = e.value
        active = still
        rnd += 1
    return results


def _mm_body(x_ref, w_ref, o_ref):
    o_ref[...] = _dot(x_ref[...], w_ref[...]).astype(o_ref.dtype)


def matmul(x, w, *, tm, tn, out_dtype=F32):
    M, K = x.shape
    _, N = w.shape
    return pl.pallas_call(
        _mm_body,
        grid=(M // tm, N // tn),
        in_specs=[pl.BlockSpec((tm, K), lambda i, j: (i, 0)),
                  pl.BlockSpec((K, tn), lambda i, j: (0, j))],
        out_specs=pl.BlockSpec((tm, tn), lambda i, j: (i, j)),
        out_shape=jax.ShapeDtypeStruct((M, N), out_dtype),
        compiler_params=_params(("parallel", "parallel")),
        name="matmul",
    )(x, w)


def _cast_body(w_ref, o_ref):
    j = pl.program_id(0)

    @pl.when(j < IN_COLS // LANES)
    def _():
        o_ref[...] = w_ref[...].astype(BF16)

    @pl.when(j >= IN_COLS // LANES)
    def _():
        o_ref[...] = jnp.zeros(o_ref.shape, BF16)


def permute_cast_w_in(w):
    D = w.shape[0]
    n_rw, n_hg, n_gate = RW_COLS // LANES, HG_COLS // LANES, GATE_COLS // LANES

    def src(j):
        in_gate = j < n_gate
        in_hg = j < n_gate + n_hg
        jj = jnp.where(in_gate, j + n_rw + n_hg, jnp.where(in_hg, j - n_gate + n_rw, j - n_gate - n_hg))
        return (0, jnp.minimum(jj, IN_COLS // LANES - 1))

    return pl.pallas_call(
        _cast_body,
        grid=(PROJ_COLS // LANES,),
        in_specs=[pl.BlockSpec((D, LANES), src)],
        out_specs=pl.BlockSpec((D, LANES), lambda j: (0, j)),
        out_shape=jax.ShapeDtypeStruct((D, PROJ_COLS), BF16),
        compiler_params=_params(("parallel",)),
        name="permute_cast_w_in",
    )(w)


def _in_proj_body(x_ref, mod_ref, g_ref, w_ref, mu_ref, o_ref, h_scr, *, width):
    j = pl.program_id(1)

    @pl.when(j == 0)
    def _():
        x = x_ref[...]
        y = x * lax.rsqrt(jnp.mean(x * x, axis=-1, keepdims=True) + NORM_EPS) * g_ref[...]
        m = mod_ref[0]
        h_scr[...] = (y * (1.0 + m[:, D_MODEL:2 * D_MODEL]) + m[:, 0:D_MODEL]).astype(BF16)

    @pl.when(j < PROJ_PLAIN_TILES)
    def _():
        o_ref[...] = jnp.dot(h_scr[...], w_ref[...], preferred_element_type=F32)

    @pl.when(j >= PROJ_PLAIN_TILES)
    def _():
        acc = jnp.dot(h_scr[...], w_ref[...], preferred_element_type=F32)
        tm = acc.shape[0]
        pos = lax.broadcasted_iota(jnp.int32, acc.shape, 0) & (width - 1)
        prev = jnp.where(pos == 0, 0.0, pltpu.roll(acc, 1, 0))
        nxt = jnp.where(pos == width - 1, 0.0, pltpu.roll(acc, tm - 1, 0))
        mu = mu_ref[...]
        o_ref[...] = acc + mu[0:1, :] * (prev - acc) + mu[1:2, :] * (nxt - acc)


def in_proj(x, mod, norm_g, w_perm, mu_perm, *, width, rows_per_mod, tm=1024):
    M, D = x.shape
    assert tm % width == 0 and rows_per_mod % tm == 0
    per = rows_per_mod // tm
    return pl.pallas_call(
        functools.partial(_in_proj_body, width=width),
        grid=(M // tm, PROJ_COLS // PROJ_TN),
        in_specs=[pl.BlockSpec((tm, D), lambda i, j: (i, 0), pipeline_mode=pl.Buffered(1)),
                  pl.BlockSpec((1, 1, N_MOD * D), lambda i, j: (i // per, 0, 0)),
                  pl.BlockSpec((1, D), lambda i, j: (0, 0)),
                  pl.BlockSpec((D, PROJ_TN), lambda i, j: (0, j)),
                  pl.BlockSpec((2, PROJ_TN), lambda i, j: (0, j))],
        out_specs=pl.BlockSpec((tm, PROJ_TN), lambda i, j: (i, j)),
        out_shape=jax.ShapeDtypeStruct((M, PROJ_COLS), F32),
        scratch_shapes=[pltpu.VMEM((tm, D), BF16)],
        compiler_params=_params(("parallel", "arbitrary")),
        name="in_proj",
    )(x, mod, norm_g, w_perm, mu_perm)


def _block_diag(x, lane_lo):
    return jnp.concatenate([jnp.where(lane_lo, x, 0.0), jnp.where(lane_lo, 0.0, x)], axis=0)


def _rwkv_masks(reverse):
    L = RW_CHUNK
    n2 = 2 * L
    ri = lax.broadcasted_iota(jnp.int32, (n2, n2), 0)
    ci = lax.broadcasted_iota(jnp.int32, (n2, n2), 1)
    rt = ri & (L - 1)
    ct = ci & (L - 1)
    blocks = []
    rb, cb = rt, ct
    s = 1
    while s < L:
        rb = rb >> 1
        cb = cb >> 1
        blocks.append(rb == cb)
        s *= 2
    couplings = [jnp.logical_and(blocks[j + 1], jnp.logical_not(blocks[j])) for j in range(len(blocks) - 1)]
    return dict(
        lane_lo=lax.broadcasted_iota(jnp.int32, (L, LANES), 1) < RW_HEAD,
        eye=ri == ci,
        strict=(rt < ct) if reverse else (rt > ct),
        incl=(rt <= ct) if reverse else (rt >= ct),
        pair_block=blocks[0],
        couplings=couplings)


def _rwkv_pair_chunk(q, lw, kd, v, kk, a, reverse, m):
    L = q.shape[0]
    n2 = 2 * L
    lane_lo = m['lane_lo']
    c = _cumsum_rows(lw, reverse)
    c_last = c[0:1, :] if reverse else c[L - 1:L, :]
    g_inc = jnp.exp(c)
    g_exc = jnp.exp(c - lw)
    g_inv = jnp.exp(-c)
    g_rem = jnp.exp(c_last - c)
    beta = kk * a
    ab = _block_diag(-kk * g_exc, lane_lo)
    qb = _block_diag(q * g_inc, lane_lo)
    bt = _block_diag(beta * g_inv, lane_lo)
    kt = _block_diag(kd * g_inv, lane_lo)
    vb = _block_diag(v, lane_lo)
    bh = _block_diag(beta * g_rem, lane_lo)
    kh = _block_diag(kd * g_rem, lane_lo)

    aa = _dot_nt(jnp.concatenate([ab, qb], axis=0), jnp.concatenate([bt, kt], axis=0))
    yield
    a_ab = jnp.where(m['strict'], aa[:n2, :n2], 0.0)
    a_ak = jnp.where(m['strict'], aa[:n2, n2:], 0.0)
    a_qb = jnp.where(m['incl'], aa[n2:, :n2], 0.0)
    a_qk = jnp.where(m['incl'], aa[n2:, n2:], 0.0)

    P = jnp.where(m['eye'], 1.0, 0.0) + jnp.where(m['pair_block'], a_ab, 0.0)
    av = _dot(a_ak, vb)
    for coupling_mask in m['couplings']:
        pc = _dot(P, jnp.where(coupling_mask, a_ab, 0.0))
        yield
        P = P + _dot(pc, P)
        yield

    wu_u0 = _dot(P, jnp.concatenate([ab, av], axis=1))
    yield
    rhs = jnp.concatenate([wu_u0, jnp.concatenate([jnp.zeros_like(vb), vb], axis=1)], axis=0)
    qy = _dot(jnp.concatenate([a_qb, a_qk], axis=1), rhs)
    st = _dot(jnp.concatenate([bh, kh], axis=0).T, rhs)
    yield
    q_eff = qb + qy[:, :LANES]
    g_last = jnp.exp(jnp.broadcast_to(c_last, (n2, LANES)))
    m_mat = jnp.where(m['eye'], g_last, 0.0) + st[:, :LANES]
    return jnp.concatenate([q_eff, m_mat], axis=0), jnp.concatenate([qy[:, LANES:], st[:, LANES:]], axis=0)


def _rwkv_apply_state(lhs, add, H):
    L = RW_CHUNK
    out = _dot(lhs, H) + add
    return out[:L] + out[L:2 * L], out[2 * L:]


def _rwkv_prep(r, k, v, lora, vec, w2pad, a2pad, ones_bd, d, want_other_kd):
    lora_in_w = lora[:, 0:LANES]
    lora_in_a = lora[:, LANES:2 * LANES]
    kk = k * vec[4:5, :]
    ss = _group_sums(kk * kk, ones_bd)
    kk = kk * lax.rsqrt(jnp.maximum(ss, 1e-24))
    u = vec[d:d + 1, :] + jnp.dot(jnp.tanh(lora_in_w).astype(BF16), w2pad[d], preferred_element_type=F32)
    lw = -DECAY_SCALE * _sigmoid(u)
    a = _sigmoid(vec[2 + d:3 + d, :] + jnp.dot(lora_in_a.astype(BF16), a2pad[d], preferred_element_type=F32))
    kd = k * (1.0 + (a - 1.0) * vec[5:6, :])
    out = dict(r=r, k=k, v=v, kk=kk, lw=lw, a=a, kd=kd)
    if want_other_kd:
        o = 1 - d
        a_o = _sigmoid(vec[2 + o:3 + o, :] + jnp.dot(lora_in_a.astype(BF16), a2pad[o], preferred_element_type=F32))
        out['kd_other'] = k * (1.0 + (a_o - 1.0) * vec[5:6, :])
    return out


def _scan_body(rf_ref, kf_ref, vf_ref, lf_ref, rb_ref, kb_ref, vb_ref, lb_ref,
               qf_ref, ff_ref, if_ref, qb_ref, fb_ref, ib_ref,
               vec_ref, w2_ref, a2_ref, g2_ref, hglb_ref, h0_ref, s0_ref,
               yf_ref, yb_ref, bonus_ref, g_ref, of_ref, ob_ref, hfin_ref, sfin_ref, h_scr, s_scr):
    i = pl.program_id(1)
    n = pl.num_programs(1)

    @pl.when(i == 0)
    def _():
        h_scr[...] = h0_ref[0]
        s_scr[...] = s0_ref[0]

    vec = vec_ref[...]
    ones_bd = _ones_blocks(RW_HEAD)
    fw = _rwkv_prep(rf_ref[0], kf_ref[0], vf_ref[0], lf_ref[0], vec, w2_ref, a2_ref, ones_bd, 0, True)
    bw = _rwkv_prep(rb_ref[0], kb_ref[0], vb_ref[0], lb_ref[0], vec, w2_ref, a2_ref, ones_bd, 1, False)

    rk = fw['r'] * vec[6:7, :]
    bonus_ref[0] = _group_sums(rk * (fw['kd'] + fw['kd_other']), ones_bd) * fw['v']
    gd = lf_ref[0][:, 2 * LANES:3 * LANES]
    g_ref[0] = jnp.dot(_sigmoid(gd).astype(BF16), g2_ref[...], preferred_element_type=F32)

    chunk_order = ((0, 1), (1, 0))
    rw_masks = (_rwkv_masks(False), _rwkv_masks(True))
    hg_masks = (_gla_masks(False), _gla_masks(True))
    chains = []
    for d, t in enumerate((fw, bw)):
        for half in chunk_order[d]:
            rows = slice(half * RW_CHUNK, (half + 1) * RW_CHUNK)
            for p in range(RW_PAIRS):
                sl = slice(p * LANES, (p + 1) * LANES)
                chains.append(_rwkv_pair_chunk(t['r'][rows, sl], t['lw'][rows, sl], t['kd'][rows, sl],
                                               t['v'][rows, sl], t['kk'][rows, sl], t['a'][rows, sl], d == 1,
                                               rw_masks[d]))
    n_rw = len(chains)
    for d, (q_ref, f_ref, v_ref) in enumerate(((qf_ref, ff_ref, if_ref), (qb_ref, fb_ref, ib_ref))):
        for h in range(HG_HEADS):
            sl = slice(h * LANES, (h + 1) * LANES)
            chains.append(_gla_head_chunk(q_ref[0, :, sl], f_ref[0, :, sl], v_ref[0, :, sl],
                                          hglb_ref[d:d + 1, sl], s_scr[d, h], d == 1, hg_masks[d]))
    outs = _interleave(chains, [0] * n_rw + [HG_FIRST_ROUND] * (len(chains) - n_rw))

    states =[[h_scr[d, p] for p in range(RW_PAIRS)] for d in range(2)]
    for step in range(2):
        for d, y_ref in enumerate((yf_ref, yb_ref)):
            half = chunk_order[d][step]
            for p in range(RW_PAIRS):
                lhs, add = outs[(d * 2 + step) * RW_PAIRS + p]
                y, states[d][p] = _rwkv_apply_state(lhs, add, states[d][p])
                y_ref[0, half * RW_CHUNK:(half + 1) * RW_CHUNK, p * LANES:(p + 1) * LANES] = y
    for d in range(2):
        for p in range(RW_PAIRS):
            h_scr[d, p] = states[d][p]

    for d, o_ref in enumerate((of_ref, ob_ref)):
        for h in range(HG_HEADS):
            o, st_new = outs[n_rw + d * HG_HEADS + h]
            o_ref[0, :, h * LANES:(h + 1) * LANES] = o
            s_scr[d, h] = st_new

    @pl.when(i == n - 1)
    def _():
        hfin_ref[0] = h_scr[...]
        sfin_ref[0] = s_scr[...]


def recurrent_scans(proj, vec, w2pad, a2pad, g2, hg_lb, h0, s0t):
    B, T, _ = proj.shape
    n = T // HG_L
    fwd = lambda c: (lambda b, i: (b, i, c))
    bwd = lambda c: (lambda b, i: (b, n - 1 - i, c))
    wide = lambda fn: pl.BlockSpec((1, HG_L, RW_DIM), fn)
    lora = lambda fn: pl.BlockSpec((1, HG_L, RW_LORA_COLS), fn)
    const = lambda shape: pl.BlockSpec(shape, lambda b, i: (0,) * len(shape))
    rw_st = pl.BlockSpec((1, 2, RW_PAIRS, LANES, LANES), lambda b, i: (b, 0, 0, 0, 0))
    hg_st = pl.BlockSpec((1, 2, HG_HEADS, HG_DV, HG_DK), lambda b, i: (b, 0, 0, 0, 0))
    out_sd = jax.ShapeDtypeStruct((B, T, RW_DIM), F32)
    return pl.pallas_call(
        _scan_body,
        grid=(B, n),
        in_specs=[wide(fwd(COL_RW_R)), wide(fwd(COL_RW_K)), wide(fwd(COL_RW_V)), lora(fwd(COL_RW_LORA)),
                  wide(bwd(COL_RW_R)), wide(bwd(COL_RW_K)), wide(bwd(COL_RW_V)), lora(bwd(COL_RW_LORA)),
                  wide(fwd(COL_HG_Q)), wide(fwd(COL_HG_FF)), wide(fwd(COL_HG_I)),
                  wide(bwd(COL_HG_Q)), wide(bwd(COL_HG_FB)), wide(bwd(COL_HG_I)),
                  const((8, RW_DIM)), const((2, LANES, RW_DIM)), const((2, LANES, RW_DIM)),
                  const((LANES, RW_DIM)), const((2, HG_KDIM)), rw_st, hg_st],
        out_specs=[wide(fwd(0)), wide(bwd(0)), wide(fwd(0)), wide(fwd(0)), wide(fwd(0)), wide(bwd(0)),
                   rw_st, hg_st],
        out_shape=[out_sd] * 6 + [jax.ShapeDtypeStruct((B, 2, RW_PAIRS, LANES, LANES), F32),
                                  jax.ShapeDtypeStruct((B, 2, HG_HEADS, HG_DV, HG_DK), F32)],
        scratch_shapes=[pltpu.VMEM((2, RW_PAIRS, LANES, LANES), F32),
                        pltpu.VMEM((2, HG_HEADS, HG_DV, HG_DK), F32)],
        compiler_params=_params(("parallel", "arbitrary")),
        name="recurrent_scans",
    )(*([proj] * 14), vec, w2pad, a2pad, g2, hg_lb, h0, s0t)


def rwkv_pack_params(p):
    z = jnp.zeros((RW_HEAD, RW_DIM), F32)
    w2pad = jnp.stack([jnp.concatenate([p['rw_w2'][0], z], 0), jnp.concatenate([z, p['rw_w2'][1]], 0)]).astype(BF16)
    a2pad = jnp.stack([jnp.concatenate([p['rw_a2'][0], z], 0), jnp.concatenate([z, p['rw_a2'][1]], 0)]).astype(BF16)
    vec = jnp.stack([p['rw_w0'][0], p['rw_w0'][1], p['rw_a0'][0], p['rw_a0'][1], p['rw_k_k'], p['rw_k_a'],
                     p['rw_r_k'].reshape(RW_DIM), jnp.zeros((RW_DIM,), F32)])
    return vec, w2pad, a2pad, p['rw_g2'].astype(BF16)


def rwkv_state_to_blockdiag(s):
    B = s.shape[0]
    h = jnp.swapaxes(s, -1, -2).reshape(B, 2, RW_PAIRS, 2, RW_HEAD, RW_HEAD)
    z = jnp.zeros_like(h[:, :, :, 0])
    top = jnp.concatenate([h[:, :, :, 0], z], axis=-1)
    bot = jnp.concatenate([z, h[:, :, :, 1]], axis=-1)
    return jnp.concatenate([top, bot], axis=-2)


def rwkv_blockdiag_to_state(hb):
    B = hb.shape[0]
    h0 = hb[:, :, :, :RW_HEAD, :RW_HEAD]
    h1 = hb[:, :, :, RW_HEAD:, RW_HEAD:]
    h = jnp.stack([h0, h1], axis=3).reshape(B, 2, RW_HEADS, RW_HEAD, RW_HEAD)
    return jnp.swapaxes(h, -1, -2)


def _gla_masks(reverse):
    L = HG_L
    row = lax.broadcasted_iota(jnp.int32, (L, LANES), 0)
    ri = lax.broadcasted_iota(jnp.int32, (L, L), 0)
    ci = lax.broadcasted_iota(jnp.int32, (L, L), 1)
    levels = []
    half = 1
    while half < L:
        size = 2 * half
        upper = (row & half) != 0
        r_up = (ri & half) != 0
        c_up = (ci & half) != 0
        if reverse:
            is_q = jnp.logical_not(upper)
            pair = jnp.logical_and(jnp.logical_not(r_up), c_up)
        else:
            is_q = upper
            pair = jnp.logical_and(r_up, jnp.logical_not(c_up))
        if size < L:
            pair = jnp.logical_and(pair, (ri // size) == (ci // size))
        levels.append(dict(half=half, upper=upper, is_q=is_q, pair=pair,
                           scale=jnp.where(is_q, LOG2_E, -LOG2_E)))
        half = size
    return dict(eye=ri == ci, levels=levels)


def _gla_head_chunk(q_raw, fl, v, lb, St, reverse, m):
    L = q_raw.shape[0]
    q = q_raw * _sigmoid(q_raw)
    f = lb + (1.0 - lb) * _sigmoid(fl)
    k = 1.0 - f
    g = jnp.log(f)
    b = _cumsum_rows(g, reverse)
    b_last = b[0:1, :] if reverse else b[L - 1:L, :]

    att = jnp.where(m['eye'], _dot_nt(q, k), 0.0)
    fk = b
    for lv in m['levels']:
        half, upper, is_q = lv['half'], lv['upper'], lv['is_q']
        if reverse:
            bref = jnp.where(upper, fk, pltpu.roll(fk, L - half, 0))
        else:
            bref = jnp.where(upper, pltpu.roll(fk, half, 0), fk)
        e = jnp.exp2((b - bref) * lv['scale'])
        x = (jnp.where(is_q, q, k) * e).astype(BF16)
        prod = lax.dot_general(x, x, (((1,), (1,)), ((), ())), preferred_element_type=F32)
        yield
        att = jnp.where(lv['pair'], prod, att)
        if reverse:
            fk = jnp.where(upper, pltpu.roll(fk, half, 0), fk)
        else:
            fk = jnp.where(upper, fk, pltpu.roll(fk, L - half, 0))
    o = _dot(att, v) + _dot_nt(q * jnp.exp(b), St)
    kst = k * jnp.exp(b_last - b)
    st_new = St * jnp.exp(b_last) + _dot(v.T, kst)
    yield
    return o, st_new


def _branch_merge_body(yf_ref, yb_ref, bonus_ref, gr_ref, of_ref, ob_ref, gh_ref, pgr_ref, pgh_ref,
                       vec_ref, wr_ref, wh_ref, o_ref):
    vec = vec_ref[...]
    y = yf_ref[...] + yb_ref[...]
    ones_head = _ones_blocks(RW_HEAD)
    mean = _group_sums(y, ones_head) * (1.0 / RW_HEAD)
    dlt = y - mean
    var = _group_sums(dlt * dlt, ones_head) * (1.0 / RW_HEAD)
    yn = dlt * lax.rsqrt(var + RW_LN_EPS) * vec[0:1, :] + vec[1:2, :]
    out_r = (yn + bonus_ref[...]) * gr_ref[...]
    o = of_ref[...] + ob_ref[...]
    ms = _group_sums(o * o, _ones_blocks(LANES)) * (1.0 / HG_DV)
    gh = gh_ref[...]
    out_h = o * lax.rsqrt(ms + NORM_EPS) * vec[2:3, :] * (gh * _sigmoid(gh))
    br = _dot(out_r, wr_ref[...])
    bh = _dot(out_h, wh_ref[...])
    o_ref[...] = (_sigmoid(pgr_ref[...]) * br + _sigmoid(pgh_ref[...]) * bh).astype(o_ref.dtype)


def branch_merge(yf, yb, bonus, gr, of, ob, proj, vec, w_br_r, w_br_h, *, tm=256):
    M = yf.shape[0]
    row = lambda w, c: pl.BlockSpec((tm, w), lambda i: (i, c))
    const = lambda shape: pl.BlockSpec(shape, lambda i: (0,) * len(shape), pipeline_mode=pl.Buffered(1))
    return pl.pallas_call(
        _branch_merge_body,
        grid=(M // tm,),
        in_specs=[row(RW_DIM, 0), row(RW_DIM, 0), row(RW_DIM, 0), row(RW_DIM, 0),
                  row(HG_VDIM, 0), row(HG_VDIM, 0), row(HG_VDIM, COL_HG_G),
                  row(D_MODEL, COL_GATE_R), row(D_MODEL, COL_GATE_H),
                  const((8, RW_DIM)), const((RW_DIM, D_MODEL)), const((HG_VDIM, D_MODEL))],
        out_specs=pl.BlockSpec((tm, D_MODEL), lambda i: (i, 0)),
        out_shape=jax.ShapeDtypeStruct((M, D_MODEL), BF16),
        compiler_params=_params(("parallel",)),
        name="branch_merge",
    )(yf, yb, bonus, gr, of, ob, proj, proj, proj, vec, w_br_r, w_br_h)


def _out_proj_body(m_ref, x_ref, mod_ref, g_ref, w_ref, wr_ref, x1_ref, h2_ref, lg_ref):
    mod = mod_ref[0]
    D = D_MODEL
    x1 = x_ref[...] + mod[:, 2 * D:3 * D] * jnp.dot(m_ref[...], w_ref[...], preferred_element_type=F32)
    x1_ref[...] = x1
    y = x1 * lax.rsqrt(jnp.mean(x1 * x1, axis=-1, keepdims=True) + NORM_EPS) * g_ref[...]
    h2 = (y * (1.0 + mod[:, 4 * D:5 * D]) + mod[:, 3 * D:4 * D]).astype(BF16)
    h2_ref[...] = h2
    lg_ref[...] = jnp.dot(h2, wr_ref[...], preferred_element_type=F32)


def out_proj(merged, x, mod, norm_g, w_out, w_router, *, rows_per_mod, tm=512):
    M, D = x.shape
    per = rows_per_mod // tm
    return pl.pallas_call(
        _out_proj_body,
        grid=(M // tm,),
        in_specs=[pl.BlockSpec((tm, D), lambda i: (i, 0)),
                  pl.BlockSpec((tm, D), lambda i: (i, 0)),
                  pl.BlockSpec((1, 1, N_MOD * D), lambda i: (i // per, 0, 0)),
                  pl.BlockSpec((1, D), lambda i: (0, 0)),
                  pl.BlockSpec((D, D), lambda i: (0, 0), pipeline_mode=pl.Buffered(1)),
                  pl.BlockSpec((D, LANES), lambda i: (0, 0), pipeline_mode=pl.Buffered(1))],
        out_specs=[pl.BlockSpec((tm, D), lambda i: (i, 0)),
                   pl.BlockSpec((tm, D), lambda i: (i, 0)),
                   pl.BlockSpec((tm, LANES), lambda i: (i, 0))],
        out_shape=[jax.ShapeDtypeStruct((M, D), F32), jax.ShapeDtypeStruct((M, D), BF16),
                   jax.ShapeDtypeStruct((M, LANES), F32)],
        compiler_params=_params(("parallel",)),
        name="out_proj",
    )(merged, x, mod, norm_g, w_out, w_router)


def _experts_body(x_ref, wg_ref, wu_ref, wd_ref, val_ref, o_ref):
    x = x_ref[0]
    gate = jnp.dot(x, wg_ref[0], preferred_element_type=F32)
    up = jnp.dot(x, wu_ref[0], preferred_element_type=F32)
    hid = (gate * _sigmoid(gate) * up).astype(BF16)
    yo = jnp.dot(hid, wd_ref[0], preferred_element_type=F32)
    scale = jnp.concatenate([val_ref[0]] * (D_MODEL // LANES), axis=1)
    o_ref[0] = (yo * scale).astype(o_ref.dtype)


def moe_experts(xe, w_gate, w_up, w_down, vals, *, tm=256):
    E, M, D = xe.shape
    tm = min(tm, M)
    return pl.pallas_call(
        _experts_body,
        grid=(E, M // tm),
        in_specs=[pl.BlockSpec((1, tm, D), lambda e, i: (e, i, 0)),
                  pl.BlockSpec((1, D, EXPERT_FF), lambda e, i: (e, 0, 0)),
                  pl.BlockSpec((1, D, EXPERT_FF), lambda e, i: (e, 0, 0)),
                  pl.BlockSpec((1, EXPERT_FF, D), lambda e, i: (e, 0, 0)),
                  pl.BlockSpec((1, tm, LANES), lambda e, i: (e, i, 0))],
        out_specs=pl.BlockSpec((1, tm, D), lambda e, i: (e, i, 0)),
        out_shape=jax.ShapeDtypeStruct((E, M, D), BF16),
        compiler_params=_params(("parallel", "parallel")),
        name="moe_experts",
    )(xe, w_gate, w_up, w_down, vals)


def _moe_combine_body(yo_ref, idx_ref, x1_ref, gt_ref, o_ref):
    T = x1_ref.shape[1]
    cap = idx_ref.shape[2]
    rows = min(T, 512)
    o_ref[0] = jnp.zeros(o_ref.shape[1:], F32)

    def one_expert(e, carry):
        slot_tok = idx_ref[0, pl.ds(e, 1), :]
        yo_e = yo_ref[e, 0]
        for r in range(T // rows):
            tok = lax.broadcasted_iota(jnp.int32, (rows, cap), 0) + r * rows
            onehot = jnp.where(tok == slot_tok, 1.0, 0.0).astype(BF16)
            o_ref[0, r * rows:(r + 1) * rows, :] += jnp.dot(onehot, yo_e, preferred_element_type=F32)
        return carry

    lax.fori_loop(0, N_EXPERTS, one_expert, 0)
    o_ref[0] = x1_ref[0] + gt_ref[0] * o_ref[0]


def moe_combine(yo, idx, x1, gt2):
    E, B, cap, D = yo.shape
    T = x1.shape[1]
    dn = min(D, COMBINE_TILE_ELEMS // T)
    per = B // gt2.shape[0]
    return pl.pallas_call(
        _moe_combine_body,
        grid=(B, D // dn),
        in_specs=[pl.BlockSpec((E, 1, cap, dn), lambda b, j: (0, b, 0, j)),
                  pl.BlockSpec((1, E, cap), lambda b, j: (b, 0, 0)),
                  pl.BlockSpec((1, T, dn), lambda b, j: (b, 0, j)),
                  pl.BlockSpec((1, 1, dn), lambda b, j: (b // per, 0, j))],
        out_specs=pl.BlockSpec((1, T, dn), lambda b, j: (b, 0, j)),
        out_shape=jax.ShapeDtypeStruct((B, T, D), F32),
        compiler_params=_params(("parallel", "parallel")),
        name="moe_combine",
    )(yo, idx, x1, gt2)


def _expert_choice_moe(h2, logits, x1, gt2, p):
    B, T, D = h2.shape
    cap = CAPACITY_FACTOR * T // N_EXPERTS
    aff = jax.nn.softmax(logits[..., :N_EXPERTS], axis=-1)
    vals, idx = lax.top_k(jnp.swapaxes(aff, 1, 2), cap)
    bidx = jnp.arange(B)[:, None, None]
    xg = h2[bidx, idx]
    xe = jnp.swapaxes(xg, 0, 1).reshape(N_EXPERTS, B * cap, D)
    ve = jnp.broadcast_to(jnp.swapaxes(vals, 0, 1).reshape(N_EXPERTS, B * cap, 1), (N_EXPERTS, B * cap, LANES))
    yo = moe_experts(xe, p['moe_w_gate'], p['moe_w_up'], p['moe_w_down'], ve)
    return moe_combine(yo.reshape(N_EXPERTS, B, cap, D), idx, x1, gt2)


def _layer(x, mod, s_rw, s_hg, width, p):
    B, T, D = x.shape
    rows_per_mod = (B * T) // mod.shape[0]
    xf = x.reshape(B * T, D)
    proj = in_proj(xf, mod, p['norm1_g'], p['w_in'], p['mu'], width=width, rows_per_mod=rows_per_mod)
    proj3 = proj.reshape(B, T, PROJ_COLS)
    yf, yb, bonus, gr, of, ob, hfin, sfin = recurrent_scans(
        proj3, p['rw_vec'], p['rw_w2pad'], p['rw_a2pad'], p['rw_g2'], p['hg_lb'],
        rwkv_state_to_blockdiag(s_rw), jnp.swapaxes(s_hg, -1, -2))
    flat = lambda u: u.reshape(B * T, u.shape[-1])
    merged = branch_merge(flat(yf), flat(yb), flat(bonus), flat(gr), flat(of), flat(ob), proj,
                          p['br_vec'], p['w_br_rwkv'], p['w_br_hgrn'])
    x1, h2, logits = out_proj(merged, xf, mod, p['norm2_g'], p['w_out'], p['moe_router'],
                              rows_per_mod=rows_per_mod)
    x2 = _expert_choice_moe(h2.reshape(B, T, D), logits.reshape(B, T, LANES), x1.reshape(B, T, D),
                            mod[:, :, 5 * D:6 * D], p)
    return x2, rwkv_blockdiag_to_state(hfin), jnp.swapaxes(sfin, -1, -2)


def _rmsnorm(x, g, eps=NORM_EPS):
    return x * lax.rsqrt(jnp.mean(x * x, axis=-1, keepdims=True) + eps) * g


def kernel(x_prompt, x_sample, c, state_rwkv, state_hgrn, c_ctx, norm1_g, norm2_g, final_norm_g, ada_w, ada_b, w_in, rw_mu, rw_w0, rw_w2, rw_a0, rw_a2, rw_g2, rw_k_k, rw_k_a, rw_r_k, rw_ln_w, rw_ln_b, hg_lb, hg_norm_g, w_br_rwkv, w_br_hgrn, w_out, moe_router, moe_w_gate, moe_w_up, moe_w_down):
    depth = norm1_g.shape[0]
    D = D_MODEL
    lb_all = jnp.cumsum(jax.nn.softmax(hg_lb.astype(F32), axis=0), axis=0)
    bp = x_prompt.shape[0]
    bs = x_sample.shape[0]
    xp, xs = x_prompt, x_sample
    new_rw, new_hg = [], []
    for l in range(depth):
        wl = w_in[l]
        w_perm = permute_cast_w_in(wl)
        mu_perm = jnp.concatenate([jnp.zeros((2, GATE_COLS + HG_COLS), F32), rw_mu[l],
                                   jnp.zeros((2, PROJ_COLS - IN_COLS), F32)], axis=1)
        pr = dict(rw_w0=rw_w0[l], rw_w2=rw_w2[l], rw_a0=rw_a0[l], rw_a2=rw_a2[l], rw_g2=rw_g2[l],
                  rw_k_k=rw_k_k[l], rw_k_a=rw_k_a[l], rw_r_k=rw_r_k[l])
        rw_vec, rw_w2pad, rw_a2pad, rw_g2b = rwkv_pack_params(pr)
        zrow = jnp.zeros((RW_DIM,), F32)
        br_vec = jnp.stack([rw_ln_w[l], rw_ln_b[l], hg_norm_g[l], zrow, zrow, zrow, zrow, zrow])
        p = dict(norm1_g=norm1_g[l][None, :], norm2_g=norm2_g[l][None, :], w_in=w_perm, mu=mu_perm,
                 rw_vec=rw_vec, rw_w2pad=rw_w2pad, rw_a2pad=rw_a2pad, rw_g2=rw_g2b,
                 hg_lb=lb_all[l], br_vec=br_vec,
                 w_br_rwkv=w_br_rwkv[l].astype(BF16), w_br_hgrn=w_br_hgrn[l].astype(BF16),
                 w_out=w_out[l].astype(BF16),
                 moe_router=jnp.pad(moe_router[l], ((0, 0), (0, LANES - N_EXPERTS))).astype(BF16),
                 moe_w_gate=moe_w_gate[l].astype(BF16), moe_w_up=moe_w_up[l].astype(BF16),
                 moe_w_down=moe_w_down[l].astype(BF16))
        cond = jnp.concatenate([c_ctx[None, :], c], axis=0)
        cond = jnp.pad(jax.nn.silu(cond), ((0, 16 - 1 - bs), (0, 0)))
        mod = matmul(cond, ada_w[l], tm=16, tn=1024)[:1 + bs] + ada_b[l]
        mod = mod.reshape(1 + bs, 1, N_MOD * D)
        zero_rw = jnp.zeros((bp, 2, RW_HEADS, RW_HEAD, RW_HEAD), F32)
        zero_hg = jnp.zeros((bp, 2, HG_HEADS, HG_DK, HG_DV), F32)
        xp, s_rw, s_hg = _layer(xp, mod[:1], zero_rw, zero_hg, xp.shape[1], p)
        new_rw.append(s_rw)
        new_hg.append(s_hg)
        xs, _, _ = _layer(xs, mod[1:], state_rwkv[:, l].astype(F32), state_hgrn[:, l].astype(F32), GRID_W, p)
    y_prompt = _rmsnorm(xp, final_norm_g)
    y_sample = _rmsnorm(xs, final_norm_g)
    return (y_prompt, y_sample, jnp.stack(new_rw, axis=1), jnp.stack(new_hg, axis=1))
```

```python
import functools

import jax
import jax.numpy as jnp
from jax import lax
from jax.experimental import pallas as pl
from jax.experimental.pallas import tpu as pltpu

F32 = jnp.float32
BF16 = jnp.bfloat16

D_MODEL = 2048
GRID_W = 64
RW_HEADS = 16
RW_HEAD = 64
RW_DIM = RW_HEADS * RW_HEAD
DECAY_LORA = 64
AAA_LORA = 64
GATE_LORA = 128
RW_LORA_COLS = 2 * DECAY_LORA + 2 * AAA_LORA + GATE_LORA
RW_LN_EPS = 64e-5
HG_HEADS = 8
HG_DK = 128
HG_DV = 128
HG_KDIM = HG_HEADS * HG_DK
HG_VDIM = HG_HEADS * HG_DV
N_EXPERTS = 16
EXPERT_FF = 1024
CAPACITY_FACTOR = 2
NORM_EPS = 1e-6
N_MOD = 6
RW_COLS = 3 * RW_DIM + RW_LORA_COLS
HG_COLS = 3 * HG_KDIM + 2 * HG_VDIM
GATE_COLS = 2 * D_MODEL
IN_COLS = RW_COLS + HG_COLS + GATE_COLS

LANES = 128
V7X_VMEM_LIMIT_BYTES = 52 * 1024 * 1024

V7X_MXU_WIDTH = 256
PROJ_TN = 5 * V7X_MXU_WIDTH
PROJ_COLS = -(-IN_COLS // PROJ_TN) * PROJ_TN
PROJ_PLAIN_TILES = (GATE_COLS + HG_COLS) // PROJ_TN
COL_GATE_R, COL_GATE_H = 0, 1
COL_HG_Q, COL_HG_FF, COL_HG_FB, COL_HG_I, COL_HG_G = 4, 5, 6, 7, 8
COL_RW_R, COL_RW_K, COL_RW_V = 9, 10, 11
COL_RW_LORA = (GATE_COLS + HG_COLS + 3 * RW_DIM) // RW_LORA_COLS

RW_CHUNK = 64
RW_PAIRS = RW_DIM // LANES
HG_L = 128
RW_SECOND_ROUND = 0
HG_FIRST_ROUND = 4
COMBINE_ROWS = 256
COMBINE_CHUNK = 256
DECAY_SCALE = 0.6065306597126334
LOG2_E = 1.4426950408889634


def _params(sem):
    return pltpu.CompilerParams(dimension_semantics=sem, vmem_limit_bytes=V7X_VMEM_LIMIT_BYTES)


def _dot(a, b):
    return jnp.dot(a.astype(BF16), b.astype(BF16), preferred_element_type=F32)


def _dot_nt(a, b):
    return lax.dot_general(a.astype(BF16), b.astype(BF16), (((1,), (1,)), ((), ())),
                           preferred_element_type=F32)


def _split_dot(x, w_bf16):
    hi = x.astype(BF16)
    lo = (x - hi.astype(F32)).astype(BF16)
    return (jnp.dot(hi, w_bf16, preferred_element_type=F32)
            + jnp.dot(lo, w_bf16, preferred_element_type=F32))


def _group_sums(x, ones_bd):
    parts = [_split_dot(x[:, p * LANES:(p + 1) * LANES], ones_bd) for p in range(x.shape[1] // LANES)]
    return jnp.concatenate(parts, axis=1)


def _ones_blocks(group):
    ri = lax.broadcasted_iota(jnp.int32, (LANES, LANES), 0)
    ci = lax.broadcasted_iota(jnp.int32, (LANES, LANES), 1)
    return jnp.where((ri // group) == (ci // group), 1.0, 0.0).astype(BF16)


def _sigmoid(x):
    return 1.0 / (1.0 + jnp.exp(-x))


def _cumsum_rows(x, reverse):
    L = x.shape[0]
    row = lax.broadcasted_iota(jnp.int32, x.shape, 0)
    s = 1
    while s < L:
        if reverse:
            x = x + jnp.where(row < L - s, pltpu.roll(x, L - s, 0), 0.0)
        else:
            x = x + jnp.where(row >= s, pltpu.roll(x, s, 0), 0.0)
        s *= 2
    return x


def _interleave(gens, first_round=None):
    results = [None] * len(gens)
    first_round = first_round or [0] * len(gens)
    active = list(enumerate(gens))
    rnd = 0
    while active:
        still = []
        for idx, g in active:
            if rnd < first_round[idx]:
                still.append((idx, g))
                continue
            try:
                next(g)
                still.append((idx, g))
            except StopIteration as e:
                results[idx] = e.value
        active = still
        rnd += 1
    return results


def _mm_body(x_ref, w_ref, o_ref):
    o_ref[...] = _dot(x_ref[...], w_ref[...]).astype(o_ref.dtype)


def matmul(x, w, *, tm, tn, out_dtype=F32):
    M, K = x.shape
    _, N = w.shape
    return pl.pallas_call(
        _mm_body,
        grid=(M // tm, N // tn),
        in_specs=[pl.BlockSpec((tm, K), lambda i, j: (i, 0)),
                  pl.BlockSpec((K, tn), lambda i, j: (0, j))],
        out_specs=pl.BlockSpec((tm, tn), lambda i, j: (i, j)),
        out_shape=jax.ShapeDtypeStruct((M, N), out_dtype),
        compiler_params=_params(("parallel", "parallel")),
        name="matmul",
    )(x, w)


def _cast_body(w_ref, o_ref):
    o_ref[...] = w_ref[...].astype(o_ref.dtype)


def cast_experts_bf16(w):
    E, K, N = w.shape
    return pl.pallas_call(
        _cast_body,
        grid=(E,),
        in_specs=[pl.BlockSpec((1, K, N), lambda e: (e, 0, 0))],
        out_specs=pl.BlockSpec((1, K, N), lambda e: (e, 0, 0)),
        out_shape=jax.ShapeDtypeStruct((E, K, N), BF16),
        compiler_params=_params(("parallel",)),
        name="cast_experts_bf16",
    )(w)


def _in_proj_body(x_ref, mod_ref, g_ref, w_ref, mu_ref, o_ref, h_scr, *, width):
    j = pl.program_id(1)

    @pl.when(j == 0)
    def _():
        x = x_ref[...]
        y = x * lax.rsqrt(jnp.mean(x * x, axis=-1, keepdims=True) + NORM_EPS) * g_ref[...]
        m = mod_ref[0]
        h_scr[...] = (y * (1.0 + m[:, D_MODEL:2 * D_MODEL]) + m[:, 0:D_MODEL]).astype(BF16)

    @pl.when(j < PROJ_PLAIN_TILES)
    def _():
        o_ref[...] = jnp.dot(h_scr[...], w_ref[...], preferred_element_type=F32)

    @pl.when(j >= PROJ_PLAIN_TILES)
    def _():
        acc = jnp.dot(h_scr[...], w_ref[...], preferred_element_type=F32)
        tm = acc.shape[0]
        pos = lax.broadcasted_iota(jnp.int32, acc.shape, 0) & (width - 1)
        prev = jnp.where(pos == 0, 0.0, pltpu.roll(acc, 1, 0))
        nxt = jnp.where(pos == width - 1, 0.0, pltpu.roll(acc, tm - 1, 0))
        mu = mu_ref[...]
        o_ref[...] = acc + mu[0:1, :] * (prev - acc) + mu[1:2, :] * (nxt - acc)


def in_proj(x, mod, norm_g, w_perm, mu_perm, *, width, rows_per_mod, tm=1024):
    M, D = x.shape
    assert tm % width == 0 and rows_per_mod % tm == 0
    per = rows_per_mod // tm
    return pl.pallas_call(
        functools.partial(_in_proj_body, width=width),
        grid=(M // tm, PROJ_COLS // PROJ_TN),
        in_specs=[pl.BlockSpec((tm, D), lambda i, j: (i, 0), pipeline_mode=pl.Buffered(1)),
                  pl.BlockSpec((1, 1, N_MOD * D), lambda i, j: (i // per, 0, 0)),
                  pl.BlockSpec((1, D), lambda i, j: (0, 0)),
                  pl.BlockSpec((D, PROJ_TN), lambda i, j: (0, j)),
                  pl.BlockSpec((2, PROJ_TN), lambda i, j: (0, j))],
        out_specs=pl.BlockSpec((tm, PROJ_TN), lambda i, j: (i, j)),
        out_shape=jax.ShapeDtypeStruct((M, PROJ_COLS), F32),
        scratch_shapes=[pltpu.VMEM((tm, D), BF16)],
        compiler_params=_params(("parallel", "arbitrary")),
        name="in_proj",
    )(x, mod, norm_g, w_perm, mu_perm)


def _block_diag(x, lane_lo):
    return jnp.concatenate([jnp.where(lane_lo, x, 0.0), jnp.where(lane_lo, 0.0, x)], axis=0)


def _rwkv_masks(reverse):
    L = RW_CHUNK
    n2 = 2 * L
    ri = lax.broadcasted_iota(jnp.int32, (n2, n2), 0)
    ci = lax.broadcasted_iota(jnp.int32, (n2, n2), 1)
    rt = ri & (L - 1)
    ct = ci & (L - 1)
    blocks = []
    rb, cb = rt, ct
    s = 1
    while s < L:
        rb = rb >> 1
        cb = cb >> 1
        blocks.append(rb == cb)
        s *= 2
    couplings = [jnp.logical_and(blocks[j + 1], jnp.logical_not(blocks[j])) for j in range(len(blocks) - 1)]
    return dict(
        lane_lo=lax.broadcasted_iota(jnp.int32, (L, LANES), 1) < RW_HEAD,
        eye=ri == ci,
        strict=(rt < ct) if reverse else (rt > ct),
        incl=(rt <= ct) if reverse else (rt >= ct),
        pair_block=blocks[0],
        couplings=couplings)


def _rwkv_pair_chunk(q, lw, kd, v, kk, a, reverse, m):
    L = q.shape[0]
    n2 = 2 * L
    lane_lo = m['lane_lo']
    c = _cumsum_rows(lw, reverse)
    c_last = c[0:1, :] if reverse else c[L - 1:L, :]
    g_inc = jnp.exp(c)
    g_exc = jnp.exp(c - lw)
    g_inv = jnp.exp(-c)
    g_rem = jnp.exp(c_last - c)
    beta = kk * a
    ab = _block_diag(-kk * g_exc, lane_lo)
    qb = _block_diag(q * g_inc, lane_lo)
    bt = _block_diag(beta * g_inv, lane_lo)
    kt = _block_diag(kd * g_inv, lane_lo)
    vb = _block_diag(v, lane_lo)
    bh = _block_diag(beta * g_rem, lane_lo)
    kh = _block_diag(kd * g_rem, lane_lo)

    aa = _dot_nt(jnp.concatenate([ab, qb], axis=0), jnp.concatenate([bt, kt], axis=0))
    yield
    a_ab = jnp.where(m['strict'], aa[:n2, :n2], 0.0)
    a_ak = jnp.where(m['strict'], aa[:n2, n2:], 0.0)
    a_qb = jnp.where(m['incl'], aa[n2:, :n2], 0.0)
    a_qk = jnp.where(m['incl'], aa[n2:, n2:], 0.0)

    P = jnp.where(m['eye'], 1.0, 0.0) + jnp.where(m['pair_block'], a_ab, 0.0)
    av = _dot(a_ak, vb)
    for coupling_mask in m['couplings']:
        pc = _dot(P, jnp.where(coupling_mask, a_ab, 0.0))
        yield
        P = P + _dot(pc, P)
        yield

    wu_u0 = _dot(P, jnp.concatenate([ab, av], axis=1))
    yield
    rhs = jnp.concatenate([wu_u0, jnp.concatenate([jnp.zeros_like(vb), vb], axis=1)], axis=0)
    qy = _dot(jnp.concatenate([a_qb, a_qk], axis=1), rhs)
    st = _dot(jnp.concatenate([bh, kh], axis=0).T, rhs)
    yield
    q_eff = qb + qy[:, :LANES]
    g_last = jnp.exp(jnp.broadcast_to(c_last, (n2, LANES)))
    m_mat = jnp.where(m['eye'], g_last, 0.0) + st[:, :LANES]
    return jnp.concatenate([q_eff, m_mat], axis=0), jnp.concatenate([qy[:, LANES:], st[:, LANES:]], axis=0)


def _rwkv_apply_state(lhs, add, H):
    L = RW_CHUNK
    out = _dot(lhs, H) + add
    return out[:L] + out[L:2 * L], out[2 * L:]


def _rwkv_prep(r, k, v, lora, vec, w2pad, a2pad, ones_bd, d, want_other_kd):
    lora_in_w = lora[:, 0:LANES]
    lora_in_a = lora[:, LANES:2 * LANES]
    kk = k * vec[4:5, :]
    ss = _group_sums(kk * kk, ones_bd)
    kk = kk * lax.rsqrt(jnp.maximum(ss, 1e-24))
    u = vec[d:d + 1, :] + jnp.dot(jnp.tanh(lora_in_w).astype(BF16), w2pad[d], preferred_element_type=F32)
    lw = -DECAY_SCALE * _sigmoid(u)
    a = _sigmoid(vec[2 + d:3 + d, :] + jnp.dot(lora_in_a.astype(BF16), a2pad[d], preferred_element_type=F32))
    kd = k * (1.0 + (a - 1.0) * vec[5:6, :])
    out = dict(r=r, k=k, v=v, kk=kk, lw=lw, a=a, kd=kd)
    if want_other_kd:
        o = 1 - d
        a_o = _sigmoid(vec[2 + o:3 + o, :] + jnp.dot(lora_in_a.astype(BF16), a2pad[o], preferred_element_type=F32))
        out['kd_other'] = k * (1.0 + (a_o - 1.0) * vec[5:6, :])
    return out


def _scan_body(rf_ref, kf_ref, vf_ref, lf_ref, rb_ref, kb_ref, vb_ref, lb_ref,
               qf_ref, ff_ref, if_ref, qb_ref, fb_ref, ib_ref,
               vec_ref, w2_ref, a2_ref, g2_ref, hglb_ref, h0_ref, s0_ref,
               yf_ref, yb_ref, bonus_ref, g_ref, of_ref, ob_ref, hfin_ref, sfin_ref, h_scr, s_scr):
    i = pl.program_id(1)
    n = pl.num_programs(1)

    @pl.when(i == 0)
    def _():
        h_scr[...] = h0_ref[0]
        s_scr[...] = s0_ref[0]

    vec = vec_ref[...]
    ones_bd = _ones_blocks(RW_HEAD)
    fw = _rwkv_prep(rf_ref[0], kf_ref[0], vf_ref[0], lf_ref[0], vec, w2_ref, a2_ref, ones_bd, 0, True)
    bw = _rwkv_prep(rb_ref[0], kb_ref[0], vb_ref[0], lb_ref[0], vec, w2_ref, a2_ref, ones_bd, 1, False)

    rk = fw['r'] * vec[6:7, :]
    bonus_ref[0] = _group_sums(rk * (fw['kd'] + fw['kd_other']), ones_bd) * fw['v']
    gd = lf_ref[0][:, 2 * LANES:3 * LANES]
    g_ref[0] = jnp.dot(_sigmoid(gd).astype(BF16), g2_ref[...], preferred_element_type=F32)

    chunk_order = ((0, 1), (1, 0))
    rw_masks = (_rwkv_masks(False), _rwkv_masks(True))
    hg_masks = (_gla_masks(False), _gla_masks(True))
    chains = []
    for d, t in enumerate((fw, bw)):
        for half in chunk_order[d]:
            rows = slice(half * RW_CHUNK, (half + 1) * RW_CHUNK)
            for p in range(RW_PAIRS):
                sl = slice(p * LANES, (p + 1) * LANES)
                chains.append(_rwkv_pair_chunk(t['r'][rows, sl], t['lw'][rows, sl], t['kd'][rows, sl],
                                               t['v'][rows, sl], t['kk'][rows, sl], t['a'][rows, sl], d == 1,
                                               rw_masks[d]))
    n_rw = len(chains)
    for d, (q_ref, f_ref, v_ref) in enumerate(((qf_ref, ff_ref, if_ref), (qb_ref, fb_ref, ib_ref))):
        for h in range(HG_HEADS):
            sl = slice(h * LANES, (h + 1) * LANES)
            chains.append(_gla_head_chunk(q_ref[0, :, sl], f_ref[0, :, sl], v_ref[0, :, sl],
                                          hglb_ref[d:d + 1, sl], s_scr[d, h], d == 1, hg_masks[d]))
    rw_start = [(0, RW_SECOND_ROUND)[(c // RW_PAIRS) % 2] for c in range(n_rw)]
    outs = _interleave(chains, rw_start + [HG_FIRST_ROUND] * (len(chains) - n_rw))

    states = [[h_scr[d, p] for p in range(RW_PAIRS)] for d in range(2)]
    for step in range(2):
        for d, y_ref in enumerate((yf_ref, yb_ref)):
            half = chunk_order[d][step]
            for p in range(RW_PAIRS):
                lhs, add = outs[(d * 2 + step) * RW_PAIRS + p]
                y, states[d][p] = _rwkv_apply_state(lhs, add, states[d][p])
                y_ref[0, half * RW_CHUNK:(half + 1) * RW_CHUNK, p * LANES:(p + 1) * LANES] = y
    for d in range(2):
        for p in range(RW_PAIRS):
            h_scr[d, p] = states[d][p]

    for d, o_ref in enumerate((of_ref, ob_ref)):
        for h in range(HG_HEADS):
            o, st_new = outs[n_rw + d * HG_HEADS + h]
            o_ref[0, :, h * LANES:(h + 1) * LANES] = o
            s_scr[d, h] = st_new

    @pl.when(i == n - 1)
    def _():
        hfin_ref[0] = h_scr[...]
        sfin_ref[0] = s_scr[...]


def recurrent_scans(proj, vec, w2pad, a2pad, g2, hg_lb, h0, s0t):
    B, T, _ = proj.shape
    n = T // HG_L
    fwd = lambda c: (lambda b, i: (b, i, c))
    bwd = lambda c: (lambda b, i: (b, n - 1 - i, c))
    wide = lambda fn: pl.BlockSpec((1, HG_L, RW_DIM), fn)
    lora = lambda fn: pl.BlockSpec((1, HG_L, RW_LORA_COLS), fn)
    const = lambda shape: pl.BlockSpec(shape, lambda b, i: (0,) * len(shape))
    rw_st = pl.BlockSpec((1, 2, RW_PAIRS, LANES, LANES), lambda b, i: (b, 0, 0, 0, 0))
    hg_st = pl.BlockSpec((1, 2, HG_HEADS, HG_DV, HG_DK), lambda b, i: (b, 0, 0, 0, 0))
    out_sd = jax.ShapeDtypeStruct((B, T, RW_DIM), F32)
    return pl.pallas_call(
        _scan_body,
        grid=(B, n),
        in_specs=[wide(fwd(COL_RW_R)), wide(fwd(COL_RW_K)), wide(fwd(COL_RW_V)), lora(fwd(COL_RW_LORA)),
                  wide(bwd(COL_RW_R)), wide(bwd(COL_RW_K)), wide(bwd(COL_RW_V)), lora(bwd(COL_RW_LORA)),
                  wide(fwd(COL_HG_Q)), wide(fwd(COL_HG_FF)), wide(fwd(COL_HG_I)),
                  wide(bwd(COL_HG_Q)), wide(bwd(COL_HG_FB)), wide(bwd(COL_HG_I)),
                  const((8, RW_DIM)), const((2, LANES, RW_DIM)), const((2, LANES, RW_DIM)),
                  const((LANES, RW_DIM)), const((2, HG_KDIM)), rw_st, hg_st],
        out_specs=[wide(fwd(0)), wide(bwd(0)), wide(fwd(0)), wide(fwd(0)), wide(fwd(0)), wide(bwd(0)),
                   rw_st, hg_st],
        out_shape=[out_sd] * 6 + [jax.ShapeDtypeStruct((B, 2, RW_PAIRS, LANES, LANES), F32),
                                  jax.ShapeDtypeStruct((B, 2, HG_HEADS, HG_DV, HG_DK), F32)],
        scratch_shapes=[pltpu.VMEM((2, RW_PAIRS, LANES, LANES), F32),
                        pltpu.VMEM((2, HG_HEADS, HG_DV, HG_DK), F32)],
        compiler_params=_params(("parallel", "arbitrary")),
        name="recurrent_scans",
    )(*([proj] * 14), vec, w2pad, a2pad, g2, hg_lb, h0, s0t)


def rwkv_pack_params(p):
    z = jnp.zeros((RW_HEAD, RW_DIM), F32)
    w2pad = jnp.stack([jnp.concatenate([p['rw_w2'][0], z], 0), jnp.concatenate([z, p['rw_w2'][1]], 0)]).astype(BF16)
    a2pad = jnp.stack([jnp.concatenate([p['rw_a2'][0], z], 0), jnp.concatenate([z, p['rw_a2'][1]], 0)]).astype(BF16)
    vec = jnp.stack([p['rw_w0'][0], p['rw_w0'][1], p['rw_a0'][0], p['rw_a0'][1], p['rw_k_k'], p['rw_k_a'],
                     p['rw_r_k'].reshape(RW_DIM), jnp.zeros((RW_DIM,), F32)])
    return vec, w2pad, a2pad, p['rw_g2'].astype(BF16)


def rwkv_state_to_blockdiag(s):
    B = s.shape[0]
    h = jnp.swapaxes(s, -1, -2).reshape(B, 2, RW_PAIRS, 2, RW_HEAD, RW_HEAD)
    z = jnp.zeros_like(h[:, :, :, 0])
    top = jnp.concatenate([h[:, :, :, 0], z], axis=-1)
    bot = jnp.concatenate([z, h[:, :, :, 1]], axis=-1)
    return jnp.concatenate([top, bot], axis=-2)


def rwkv_blockdiag_to_state(hb):
    B = hb.shape[0]
    h0 = hb[:, :, :, :RW_HEAD, :RW_HEAD]
    h1 = hb[:, :, :, RW_HEAD:, RW_HEAD:]
    h = jnp.stack([h0, h1], axis=3).reshape(B, 2, RW_HEADS, RW_HEAD, RW_HEAD)
    return jnp.swapaxes(h, -1, -2)


def _gla_masks(reverse):
    L = HG_L
    row = lax.broadcasted_iota(jnp.int32, (L, LANES), 0)
    ri = lax.broadcasted_iota(jnp.int32, (L, L), 0)
    ci = lax.broadcasted_iota(jnp.int32, (L, L), 1)
    levels = []
    half = 1
    while half < L:
        size = 2 * half
        upper = (row & half) != 0
        r_up = (ri & half) != 0
        c_up = (ci & half) != 0
        if reverse:
            is_q = jnp.logical_not(upper)
            pair = jnp.logical_and(jnp.logical_not(r_up), c_up)
        else:
            is_q = upper
            pair = jnp.logical_and(r_up, jnp.logical_not(c_up))
        if size < L:
            pair = jnp.logical_and(pair, (ri // size) == (ci // size))
        levels.append(dict(half=half, upper=upper, is_q=is_q, pair=pair,
                           scale=jnp.where(is_q, LOG2_E, -LOG2_E)))
        half = size
    return dict(eye=ri == ci, levels=levels)


def _gla_head_chunk(q_raw, fl, v, lb, St, reverse, m):
    L = q_raw.shape[0]
    q = q_raw * _sigmoid(q_raw)
    f = lb + (1.0 - lb) * _sigmoid(fl)
    k = 1.0 - f
    g = jnp.log(f)
    b = _cumsum_rows(g, reverse)
    b_last = b[0:1, :] if reverse else b[L - 1:L, :]

    att = jnp.where(m['eye'], _dot_nt(q, k), 0.0)
    fk = b
    for lv in m['levels']:
        half, upper, is_q = lv['half'], lv['upper'], lv['is_q']
        if reverse:
            bref = jnp.where(upper, fk, pltpu.roll(fk, L - half, 0))
        else:
            bref = jnp.where(upper, pltpu.roll(fk, half, 0), fk)
        e = jnp.exp2((b - bref) * lv['scale'])
        x = (jnp.where(is_q, q, k) * e).astype(BF16)
        prod = lax.dot_general(x, x, (((1,), (1,)), ((), ())), preferred_element_type=F32)
        yield
        att = jnp.where(lv['pair'], prod, att)
        if reverse:
            fk = jnp.where(upper, pltpu.roll(fk, half, 0), fk)
        else:
            fk = jnp.where(upper, fk, pltpu.roll(fk, L - half, 0))
    o = _dot(att, v) + _dot_nt(q * jnp.exp(b), St)
    kst = k * jnp.exp(b_last - b)
    st_new = St * jnp.exp(b_last) + _dot(v.T, kst)
    yield
    return o, st_new


def _branch_merge_body(yf_ref, yb_ref, bonus_ref, gr_ref, of_ref, ob_ref, gh_ref, pgr_ref, pgh_ref,
                       vec_ref, wr_ref, wh_ref, o_ref):
    vec = vec_ref[...]
    y = yf_ref[...] + yb_ref[...]
    ones_head = _ones_blocks(RW_HEAD)
    mean = _group_sums(y, ones_head) * (1.0 / RW_HEAD)
    dlt = y - mean
    var = _group_sums(dlt * dlt, ones_head) * (1.0 / RW_HEAD)
    yn = dlt * lax.rsqrt(var + RW_LN_EPS) * vec[0:1, :] + vec[1:2, :]
    out_r = (yn + bonus_ref[...]) * gr_ref[...]
    o = of_ref[...] + ob_ref[...]
    ms = _group_sums(o * o, _ones_blocks(LANES)) * (1.0 / HG_DV)
    gh = gh_ref[...]
    out_h = o * lax.rsqrt(ms + NORM_EPS) * vec[2:3, :] * (gh * _sigmoid(gh))
    br = _dot(out_r, wr_ref[...])
    bh = _dot(out_h, wh_ref[...])
    o_ref[...] = (_sigmoid(pgr_ref[...]) * br + _sigmoid(pgh_ref[...]) * bh).astype(o_ref.dtype)


def branch_merge(yf, yb, bonus, gr, of, ob, proj, vec, w_br_r, w_br_h, *, tm=256):
    M = yf.shape[0]
    row = lambda w, c: pl.BlockSpec((tm, w), lambda i: (i, c))
    const = lambda shape: pl.BlockSpec(shape, lambda i: (0,) * len(shape), pipeline_mode=pl.Buffered(1))
    return pl.pallas_call(
        _branch_merge_body,
        grid=(M // tm,),
        in_specs=[row(RW_DIM, 0), row(RW_DIM, 0), row(RW_DIM, 0), row(RW_DIM, 0),
                  row(HG_VDIM, 0), row(HG_VDIM, 0), row(HG_VDIM, COL_HG_G),
                  row(D_MODEL, COL_GATE_R), row(D_MODEL, COL_GATE_H),
                  const((8, RW_DIM)), const((RW_DIM, D_MODEL)), const((HG_VDIM, D_MODEL))],
        out_specs=pl.BlockSpec((tm, D_MODEL), lambda i: (i, 0)),
        out_shape=jax.ShapeDtypeStruct((M, D_MODEL), BF16),
        compiler_params=_params(("parallel",)),
        name="branch_merge",
    )(yf, yb, bonus, gr, of, ob, proj, proj, proj, vec, w_br_r, w_br_h)


def _out_proj_body(m_ref, x_ref, mod_ref, g_ref, w_ref, wr_ref, x1_ref, h2_ref, lg_ref):
    mod = mod_ref[0]
    D = D_MODEL
    x1 = x_ref[...] + mod[:, 2 * D:3 * D] * jnp.dot(m_ref[...], w_ref[...], preferred_element_type=F32)
    x1_ref[...] = x1
    y = x1 * lax.rsqrt(jnp.mean(x1 * x1, axis=-1, keepdims=True) + NORM_EPS) * g_ref[...]
    h2 = (y * (1.0 + mod[:, 4 * D:5 * D]) + mod[:, 3 * D:4 * D]).astype(BF16)
    h2_ref[...] = h2
    lg_ref[...] = jnp.dot(h2, wr_ref[...], preferred_element_type=F32)


def out_proj(merged, x, mod, norm_g, w_out, w_router, *, rows_per_mod, tm=512):
    M, D = x.shape
    per = rows_per_mod // tm
    return pl.pallas_call(
        _out_proj_body,
        grid=(M // tm,),
        in_specs=[pl.BlockSpec((tm, D), lambda i: (i, 0)),
                  pl.BlockSpec((tm, D), lambda i: (i, 0)),
                  pl.BlockSpec((1, 1, N_MOD * D), lambda i: (i // per, 0, 0)),
                  pl.BlockSpec((1, D), lambda i: (0, 0)),
                  pl.BlockSpec((D, D), lambda i: (0, 0), pipeline_mode=pl.Buffered(1)),
                  pl.BlockSpec((D, LANES), lambda i: (0, 0), pipeline_mode=pl.Buffered(1))],
        out_specs=[pl.BlockSpec((tm, D), lambda i: (i, 0)),
                   pl.BlockSpec((tm, D), lambda i: (i, 0)),
                   pl.BlockSpec((tm, LANES), lambda i: (i, 0))],
        out_shape=[jax.ShapeDtypeStruct((M, D), F32), jax.ShapeDtypeStruct((M, D), BF16),
                   jax.ShapeDtypeStruct((M, LANES), F32)],
        compiler_params=_params(("parallel",)),
        name="out_proj",
    )(merged, x, mod, norm_g, w_out, w_router)


def _experts_body(x_ref, wg_ref, wu_ref, wd_ref, val_ref, o_ref):
    x = x_ref[0]
    gate = jnp.dot(x, wg_ref[0], preferred_element_type=F32)
    up = jnp.dot(x, wu_ref[0], preferred_element_type=F32)
    hid = (gate * _sigmoid(gate) * up).astype(BF16)
    yo = jnp.dot(hid, wd_ref[0], preferred_element_type=F32)
    scale = jnp.concatenate([val_ref[0]] * (D_MODEL // LANES), axis=1)
    o_ref[0] = (yo * scale).astype(o_ref.dtype)


def moe_experts(xe, w_gate, w_up, w_down, vals, *, tm=256):
    E, M, D = xe.shape
    tm = min(tm, M)
    return pl.pallas_call(
        _experts_body,
        grid=(E, M // tm),
        in_specs=[pl.BlockSpec((1, tm, D), lambda e, i: (e, i, 0)),
                  pl.BlockSpec((1, D, EXPERT_FF), lambda e, i: (e, 0, 0)),
                  pl.BlockSpec((1, D, EXPERT_FF), lambda e, i: (e, 0, 0)),
                  pl.BlockSpec((1, EXPERT_FF, D), lambda e, i: (e, 0, 0)),
                  pl.BlockSpec((1, tm, LANES), lambda e, i: (e, i, 0))],
        out_specs=pl.BlockSpec((1, tm, D), lambda e, i: (e, i, 0)),
        out_shape=jax.ShapeDtypeStruct((E, M, D), BF16),
        compiler_params=_params(("parallel", "parallel")),
        name="moe_experts",
    )(xe, w_gate, w_up, w_down, vals)


def _combine_plan(tok_sorted, T):
    B, R = tok_sorted.shape
    n_tiles, n_chunks = T // COMBINE_ROWS, R // COMBINE_CHUNK
    n_pairs = n_tiles + n_chunks
    bounds = jnp.arange(n_tiles + 1, dtype=jnp.int32) * COMBINE_ROWS
    start = jnp.sum((tok_sorted[:, None, :] < bounds[None, :, None]).astype(jnp.int32), axis=-1)
    first_chunk = jnp.minimum(start[:, :-1] // COMBINE_CHUNK, n_chunks - 1)
    last_chunk = jnp.clip((start[:, 1:] - 1) // COMBINE_CHUNK, first_chunk, n_chunks - 1)
    count = last_chunk - first_chunk + 1
    offset = jnp.cumsum(count, axis=-1) - count
    total = offset[:, -1] + count[:, -1]
    pair = jnp.arange(n_pairs, dtype=jnp.int32)
    tile_of = jnp.sum((pair[None, :, None] >= offset[:, None, :]).astype(jnp.int32), axis=-1) - 1
    k_in = pair[None, :] - jnp.take_along_axis(offset, tile_of, axis=1)
    count_t = jnp.take_along_axis(count, tile_of, axis=1)
    chunk_of = jnp.take_along_axis(first_chunk, tile_of, axis=1) + jnp.minimum(k_in, count_t - 1)
    real = pair[None, :] < total[:, None]
    flags = (jnp.logical_and(real, k_in == 0).astype(jnp.int32)
             + 2 * real.astype(jnp.int32)
             + 4 * jnp.logical_and(real, k_in == count_t - 1).astype(jnp.int32))
    return tile_of, chunk_of, flags


def _moe_combine_body(tile_ref, chunk_ref, flag_ref, yo_ref, tok_ref, x1_ref, gt_ref, g_ref, o_ref, acc, *, final_norm):
    b = pl.program_id(0)
    p = pl.program_id(1)
    flags = flag_ref[b, p]
    rows = x1_ref.shape[1]

    @pl.when((flags & 1) != 0)
    def _():
        acc[...] = jnp.zeros(acc.shape, F32)

    @pl.when((flags & 2) != 0)
    def _():
        tok = lax.broadcasted_iota(jnp.int32, (rows, tok_ref.shape[3]), 0) + tile_ref[b, p] * rows
        onehot = jnp.where(tok == tok_ref[0, 0], 1.0, 0.0).astype(BF16)
        acc[...] += jnp.dot(onehot, yo_ref[0], preferred_element_type=F32)

    @pl.when((flags & 4) != 0)
    def _():
        x2 = x1_ref[0] + gt_ref[0] * acc[...]
        if final_norm:
            x2 = x2 * lax.rsqrt(jnp.mean(x2 * x2, axis=-1, keepdims=True) + NORM_EPS) * g_ref[...]
        o_ref[0] = x2


def moe_combine(yo_sorted, tok_sorted, x1, gt2, final_g):
    B, R, D = yo_sorted.shape
    T = x1.shape[1]
    per = B // gt2.shape[0]
    tile_of, chunk_of, flags = _combine_plan(tok_sorted, T)
    g = jnp.ones((1, D), F32) if final_g is None else final_g.reshape(1, D)
    grid_spec = pltpu.PrefetchScalarGridSpec(
        num_scalar_prefetch=3,
        grid=(B, tile_of.shape[1]),
        in_specs=[pl.BlockSpec((1, COMBINE_CHUNK, D), lambda b, p, t, c, f: (b, c[b, p], 0)),
                  pl.BlockSpec((1, 1, 1, COMBINE_CHUNK), lambda b, p, t, c, f: (b, c[b, p], 0, 0)),
                  pl.BlockSpec((1, COMBINE_ROWS, D), lambda b, p, t, c, f: (b, t[b, p], 0)),
                  pl.BlockSpec((1, 1, D), lambda b, p, t, c, f: (b // per, 0, 0)),
                  pl.BlockSpec((1, D), lambda b, p, t, c, f: (0, 0))],
        out_specs=pl.BlockSpec((1, COMBINE_ROWS, D), lambda b, p, t, c, f: (b, t[b, p], 0)),
        scratch_shapes=[pltpu.VMEM((COMBINE_ROWS, D), F32)])
    return pl.pallas_call(
        functools.partial(_moe_combine_body, final_norm=final_g is not None),
        grid_spec=grid_spec,
        out_shape=jax.ShapeDtypeStruct((B, T, D), F32),
        compiler_params=_params(("parallel", "arbitrary")),
        name="moe_combine",
    )(tile_of, chunk_of, flags, yo_sorted, tok_sorted.reshape(B, R // COMBINE_CHUNK, 1, COMBINE_CHUNK), x1, gt2, g)


def _expert_choice_moe(h2, logits, x1, gt2, final_g, p):
    B, T, D = h2.shape
    cap = CAPACITY_FACTOR * T // N_EXPERTS
    aff = jax.nn.softmax(logits[..., :N_EXPERTS], axis=-1)
    vals, idx = lax.top_k(jnp.swapaxes(aff, 1, 2), cap)
    bidx = jnp.arange(B)[:, None, None]
    xg = h2[bidx, idx]
    xe = jnp.swapaxes(xg, 0, 1).reshape(N_EXPERTS, B * cap, D)
    ve = jnp.broadcast_to(jnp.swapaxes(vals, 0, 1).reshape(N_EXPERTS, B * cap, 1), (N_EXPERTS, B * cap, LANES))
    yo = moe_experts(xe, p['moe_w_gate'], p['moe_w_up'], p['moe_w_down'], ve)
    tok = idx.reshape(B, N_EXPERTS * cap)
    order = jnp.argsort(tok, axis=-1)
    tok_sorted = jnp.take_along_axis(tok, order, axis=-1)
    src_row = (order // cap) * (B * cap) + jnp.arange(B, dtype=order.dtype)[:, None] * cap + order % cap
    yo_sorted = yo.reshape(N_EXPERTS * B * cap, D)[src_row]
    return moe_combine(yo_sorted, tok_sorted, x1, gt2, final_g)


def _layer(x, mod, s_rw, s_hg, width, final_g, p):
    B, T, D = x.shape
    rows_per_mod = (B * T) // mod.shape[0]
    xf = x.reshape(B * T, D)
    proj = in_proj(xf, mod, p['norm1_g'], p['w_in'], p['mu'], width=width, rows_per_mod=rows_per_mod)
    proj3 = proj.reshape(B, T, PROJ_COLS)
    yf, yb, bonus, gr, of, ob, hfin, sfin = recurrent_scans(
        proj3, p['rw_vec'], p['rw_w2pad'], p['rw_a2pad'], p['rw_g2'], p['hg_lb'],
        rwkv_state_to_blockdiag(s_rw), jnp.swapaxes(s_hg, -1, -2))
    flat = lambda u: u.reshape(B * T, u.shape[-1])
    merged = branch_merge(flat(yf), flat(yb), flat(bonus), flat(gr), flat(of), flat(ob), proj,
                          p['br_vec'], p['w_br_rwkv'], p['w_br_hgrn'])
    x1, h2, logits = out_proj(merged, xf, mod, p['norm2_g'], p['w_out'], p['moe_router'],
                              rows_per_mod=rows_per_mod)
    x2 = _expert_choice_moe(h2.reshape(B, T, D), logits.reshape(B, T, LANES), x1.reshape(B, T, D),
                            mod[:, :, 5 * D:6 * D], final_g, p)
    return x2, rwkv_blockdiag_to_state(hfin), jnp.swapaxes(sfin, -1, -2)


def kernel(x_prompt, x_sample, c, state_rwkv, state_hgrn, c_ctx, norm1_g, norm2_g, final_norm_g, ada_w, ada_b, w_in, rw_mu, rw_w0, rw_w2, rw_a0, rw_a2, rw_g2, rw_k_k, rw_k_a, rw_r_k, rw_ln_w, rw_ln_b, hg_lb, hg_norm_g, w_br_rwkv, w_br_hgrn, w_out, moe_router, moe_w_gate, moe_w_up, moe_w_down):
    depth = norm1_g.shape[0]
    D = D_MODEL
    lb_all = jnp.cumsum(jax.nn.softmax(hg_lb.astype(F32), axis=0), axis=0)
    bp = x_prompt.shape[0]
    bs = x_sample.shape[0]
    xp, xs = x_prompt, x_sample
    new_rw, new_hg = [], []
    for l in range(depth):
        wl = w_in[l]
        w_perm = jnp.concatenate([wl[:, RW_COLS + HG_COLS:], wl[:, RW_COLS:RW_COLS + HG_COLS], wl[:, :RW_COLS],
                                  jnp.zeros((D, PROJ_COLS - IN_COLS), F32)], axis=1).astype(BF16)
        mu_perm = jnp.concatenate([jnp.zeros((2, GATE_COLS + HG_COLS), F32), rw_mu[l],
                                   jnp.zeros((2, PROJ_COLS - IN_COLS), F32)], axis=1)
        pr = dict(rw_w0=rw_w0[l], rw_w2=rw_w2[l], rw_a0=rw_a0[l], rw_a2=rw_a2[l], rw_g2=rw_g2[l],
                  rw_k_k=rw_k_k[l], rw_k_a=rw_k_a[l], rw_r_k=rw_r_k[l])
        rw_vec, rw_w2pad, rw_a2pad, rw_g2b = rwkv_pack_params(pr)
        zrow = jnp.zeros((RW_DIM,), F32)
        br_vec = jnp.stack([rw_ln_w[l], rw_ln_b[l], hg_norm_g[l], zrow, zrow, zrow, zrow, zrow])
        p = dict(norm1_g=norm1_g[l][None, :], norm2_g=norm2_g[l][None, :], w_in=w_perm, mu=mu_perm,
                 rw_vec=rw_vec, rw_w2pad=rw_w2pad, rw_a2pad=rw_a2pad, rw_g2=rw_g2b,
                 hg_lb=lb_all[l], br_vec=br_vec,
                 w_br_rwkv=w_br_rwkv[l].astype(BF16), w_br_hgrn=w_br_hgrn[l].astype(BF16),
                 w_out=w_out[l].astype(BF16),
                 moe_router=jnp.pad(moe_router[l], ((0, 0), (0, LANES - N_EXPERTS))).astype(BF16),
                 moe_w_gate=cast_experts_bf16(moe_w_gate[l]), moe_w_up=cast_experts_bf16(moe_w_up[l]),
                 moe_w_down=cast_experts_bf16(moe_w_down[l]))
        cond = jnp.concatenate([c_ctx[None, :], c], axis=0)
        cond = jnp.pad(jax.nn.silu(cond), ((0, 16 - 1 - bs), (0, 0)))
        mod = matmul(cond, ada_w[l], tm=16, tn=1024)[:1 + bs] + ada_b[l]
        mod = mod.reshape(1 + bs, 1, N_MOD * D)
        zero_rw = jnp.zeros((bp, 2, RW_HEADS, RW_HEAD, RW_HEAD), F32)
        zero_hg = jnp.zeros((bp, 2, HG_HEADS, HG_DK, HG_DV), F32)
        final_g = final_norm_g if l == depth - 1 else None
        xp, s_rw, s_hg = _layer(xp, mod[:1], zero_rw, zero_hg, xp.shape[1], final_g, p)
        new_rw.append(s_rw)
        new_hg.append(s_hg)
        xs, _, _ = _layer(xs, mod[1:], state_rwkv[:, l].astype(F32), state_hgrn[:, l].astype(F32), GRID_W,
                          final_g, p)
    return (xp, xs, jnp.stack(new_rw, axis=1), jnp.stack(new_hg, axis=1))
```

```python
import functools

import jax
import jax.numpy as jnp
from jax import lax
from jax.experimental import pallas as pl
from jax.experimental.pallas import tpu as pltpu

F32 = jnp.float32
BF16 = jnp.bfloat16

D_MODEL = 2048
GRID_W = 64
RW_HEADS = 16
RW_HEAD = 64
RW_DIM = RW_HEADS * RW_HEAD
DECAY_LORA = 64
AAA_LORA = 64
GATE_LORA = 128
RW_LORA_COLS = 2 * DECAY_LORA + 2 * AAA_LORA + GATE_LORA
RW_LN_EPS = 64e-5
HG_HEADS = 8
HG_DK = 128
HG_DV = 128
HG_KDIM = HG_HEADS * HG_DK
HG_VDIM = HG_HEADS * HG_DV
N_EXPERTS = 16
EXPERT_FF = 1024
CAPACITY_FACTOR = 2
NORM_EPS = 1e-6
N_MOD = 6
RW_COLS = 3 * RW_DIM + RW_LORA_COLS
HG_COLS = 3 * HG_KDIM + 2 * HG_VDIM
GATE_COLS = 2 * D_MODEL
IN_COLS = RW_COLS + HG_COLS + GATE_COLS

LANES = 128
V7X_VMEM_LIMIT_BYTES = 52 * 1024 * 1024

V7X_MXU_WIDTH = 256
PROJ_TN = 5 * V7X_MXU_WIDTH
PROJ_COLS = -(-IN_COLS // PROJ_TN) * PROJ_TN
PROJ_PLAIN_TILES = (GATE_COLS + HG_COLS) // PROJ_TN
COL_GATE_R, COL_GATE_H = 0, 1
COL_HG_Q, COL_HG_FF, COL_HG_FB, COL_HG_I, COL_HG_G = 4, 5, 6, 7, 8
COL_RW_R, COL_RW_K, COL_RW_V = 9, 10, 11
COL_RW_LORA = (GATE_COLS + HG_COLS + 3 * RW_DIM) // RW_LORA_COLS

RW_CHUNK = 64
RW_PAIRS = RW_DIM // LANES
HG_L = 128
RW_SECOND_ROUND = 0
HG_FIRST_ROUND = 4
COMBINE_ROWS = 256
COMBINE_CHUNK = 256
DECAY_SCALE = 0.6065306597126334
LOG2_E = 1.4426950408889634


def _params(sem):
    return pltpu.CompilerParams(dimension_semantics=sem, vmem_limit_bytes=V7X_VMEM_LIMIT_BYTES)


def _dot(a, b):
    return jnp.dot(a.astype(BF16), b.astype(BF16), preferred_element_type=F32)


def _dot_nt(a, b):
    return lax.dot_general(a.astype(BF16), b.astype(BF16), (((1,), (1,)), ((), ())),
                           preferred_element_type=F32)


def _split_dot(x, w_bf16):
    hi = x.astype(BF16)
    lo = (x - hi.astype(F32)).astype(BF16)
    return (jnp.dot(hi, w_bf16, preferred_element_type=F32)
            + jnp.dot(lo, w_bf16, preferred_element_type=F32))


def _group_sums(x, ones_bd):
    parts = [_split_dot(x[:, p * LANES:(p + 1) * LANES], ones_bd) for p in range(x.shape[1] // LANES)]
    return jnp.concatenate(parts, axis=1)


def _ones_blocks(group):
    ri = lax.broadcasted_iota(jnp.int32, (LANES, LANES), 0)
    ci = lax.broadcasted_iota(jnp.int32, (LANES, LANES), 1)
    return jnp.where((ri // group) == (ci // group), 1.0, 0.0).astype(BF16)


def _sigmoid(x):
    return 1.0 / (1.0 + jnp.exp(-x))


def _cumsum_rows(x, reverse):
    L = x.shape[0]
    row = lax.broadcasted_iota(jnp.int32, x.shape, 0)
    s = 1
    while s < L:
        if reverse:
            x = x + jnp.where(row < L - s, pltpu.roll(x, L - s, 0), 0.0)
        else:
            x = x + jnp.where(row >= s, pltpu.roll(x, s, 0), 0.0)
        s *= 2
    return x


def _interleave(gens, first_round=None):
    results = [None] * len(gens)
    first_round = first_round or [0] * len(gens)
    active = list(enumerate(gens))
    rnd = 0
    while active:
        still = []
        for idx, g in active:
            if rnd < first_round[idx]:
                still.append((idx, g))
                continue
            try:
                next(g)
                still.append((idx, g))
            except StopIteration as e:
                results[idx] = e.value
        active = still
        rnd += 1
    return results


def _mm_body(x_ref, w_ref, o_ref):
    o_ref[...] = _dot(x_ref[...], w_ref[...]).astype(o_ref.dtype)


def matmul(x, w, *, tm, tn, out_dtype=F32):
    M, K = x.shape
    _, N = w.shape
    return pl.pallas_call(
        _mm_body,
        grid=(M // tm, N // tn),
        in_specs=[pl.BlockSpec((tm, K), lambda i, j: (i, 0)),
                  pl.BlockSpec((K, tn), lambda i, j: (0, j))],
        out_specs=pl.BlockSpec((tm, tn), lambda i, j: (i, j)),
        out_shape=jax.ShapeDtypeStruct((M, N), out_dtype),
        compiler_params=_params(("parallel", "parallel")),
        name="matmul",
    )(x, w)


def _cast_body(w_ref, o_ref):
    o_ref[...] = w_ref[...].astype(o_ref.dtype)


def cast_experts_bf16(w):
    E, K, N = w.shape
    return pl.pallas_call(
        _cast_body,
        grid=(E,),
        in_specs=[pl.BlockSpec((1, K, N), lambda e: (e, 0, 0))],
        out_specs=pl.BlockSpec((1, K, N), lambda e: (e, 0, 0)),
        out_shape=jax.ShapeDtypeStruct((E, K, N), BF16),
        compiler_params=_params(("parallel",)),
        name="cast_experts_bf16",
    )(w)


def _norm_mod_body(x_ref, mod_ref, g_ref, o_ref):
    x = x_ref[...]
    y = x * lax.rsqrt(jnp.mean(x * x, axis=-1, keepdims=True) + NORM_EPS) * g_ref[...]
    m = mod_ref[0]
    o_ref[...] = (y * (1.0 + m[:, D_MODEL:2 * D_MODEL]) + m[:, 0:D_MODEL]).astype(o_ref.dtype)


def norm_modulate(x, mod, norm_g, *, rows_per_mod, tm=512):
    M, D = x.shape
    per = rows_per_mod // tm
    return pl.pallas_call(
        _norm_mod_body,
        grid=(M // tm,),
        in_specs=[pl.BlockSpec((tm, D), lambda i: (i, 0)),
                  pl.BlockSpec((1, 1, N_MOD * D), lambda i: (i // per, 0, 0)),
                  pl.BlockSpec((1, D), lambda i: (0, 0))],
        out_specs=pl.BlockSpec((tm, D), lambda i: (i, 0)),
        out_shape=jax.ShapeDtypeStruct((M, D), BF16),
        compiler_params=_params(("parallel",)),
        name="norm_modulate",
    )(x, mod, norm_g)


def _in_proj_body(h_ref, w_ref, mu_ref, o_ref, *, width):
    j = pl.program_id(1)

    @pl.when(j < PROJ_PLAIN_TILES)
    def _():
        o_ref[...] = jnp.dot(h_ref[...], w_ref[...], preferred_element_type=F32)

    @pl.when(j >= PROJ_PLAIN_TILES)
    def _():
        acc = jnp.dot(h_ref[...], w_ref[...], preferred_element_type=F32)
        tm = acc.shape[0]
        pos = lax.broadcasted_iota(jnp.int32, acc.shape, 0) & (width - 1)
        prev = jnp.where(pos == 0, 0.0, pltpu.roll(acc, 1, 0))
        nxt = jnp.where(pos == width - 1, 0.0, pltpu.roll(acc, tm - 1, 0))
        mu = mu_ref[...]
        o_ref[...] = acc + mu[0:1, :] * (prev - acc) + mu[1:2, :] * (nxt - acc)


def in_proj(h, w_perm, mu_perm, *, width, tm=1024):
    M, D = h.shape
    assert tm % width == 0
    return pl.pallas_call(
        functools.partial(_in_proj_body, width=width),
        grid=(M // tm, PROJ_COLS // PROJ_TN),
        in_specs=[pl.BlockSpec((tm, D), lambda i, j: (i, 0)),
                  pl.BlockSpec((D, PROJ_TN), lambda i, j: (0, j)),
                  pl.BlockSpec((2, PROJ_TN), lambda i, j: (0, j))],
        out_specs=pl.BlockSpec((tm, PROJ_TN), lambda i, j: (i, j)),
        out_shape=jax.ShapeDtypeStruct((M, PROJ_COLS), F32),
        compiler_params=_params(("parallel", "parallel")),
        name="in_proj",
    )(h, w_perm, mu_perm)


def _block_diag(x, lane_lo):
    return jnp.concatenate([jnp.where(lane_lo, x, 0.0), jnp.where(lane_lo, 0.0, x)], axis=0)


def _rwkv_masks(reverse):
    L = RW_CHUNK
    n2 = 2 * L
    ri = lax.broadcasted_iota(jnp.int32, (n2, n2), 0)
    ci = lax.broadcasted_iota(jnp.int32, (n2, n2), 1)
    rt = ri & (L - 1)
    ct = ci & (L - 1)
    blocks = []
    rb, cb = rt, ct
    s = 1
    while s < L:
        rb = rb >> 1
        cb = cb >> 1
        blocks.append(rb == cb)
        s *= 2
    couplings = [jnp.logical_and(blocks[j + 1], jnp.logical_not(blocks[j])) for j in range(len(blocks) - 1)]
    return dict(
        lane_lo=lax.broadcasted_iota(jnp.int32, (L, LANES), 1) < RW_HEAD,
        eye=ri == ci,
        strict=(rt < ct) if reverse else (rt > ct),
        incl=(rt <= ct) if reverse else (rt >= ct),
        pair_block=blocks[0],
        couplings=couplings)


def _rwkv_pair_chunk(q, lw, kd, v, kk, a, reverse, m):
    L = q.shape[0]
    n2 = 2 * L
    lane_lo = m['lane_lo']
    c = _cumsum_rows(lw, reverse)
    c_last = c[0:1, :] if reverse else c[L - 1:L, :]
    g_inc = jnp.exp(c)
    g_exc = jnp.exp(c - lw)
    g_inv = jnp.exp(-c)
    g_rem = jnp.exp(c_last - c)
    beta = kk * a
    ab = _block_diag(-kk * g_exc, lane_lo)
    qb = _block_diag(q * g_inc, lane_lo)
    bt = _block_diag(beta * g_inv, lane_lo)
    kt = _block_diag(kd * g_inv, lane_lo)
    vb = _block_diag(v, lane_lo)
    bh = _block_diag(beta * g_rem, lane_lo)
    kh = _block_diag(kd * g_rem, lane_lo)

    aa = _dot_nt(jnp.concatenate([ab, qb], axis=0), jnp.concatenate([bt, kt], axis=0))
    yield
    a_ab = jnp.where(m['strict'], aa[:n2, :n2], 0.0)
    a_ak = jnp.where(m['strict'], aa[:n2, n2:], 0.0)
    a_qb = jnp.where(m['incl'], aa[n2:, :n2], 0.0)
    a_qk = jnp.where(m['incl'], aa[n2:, n2:], 0.0)

    P = jnp.where(m['eye'], 1.0, 0.0) + jnp.where(m['pair_block'], a_ab, 0.0)
    av = _dot(a_ak, vb)
    for coupling_mask in m['couplings']:
        pc = _dot(P, jnp.where(coupling_mask, a_ab, 0.0))
        yield
        P = P + _dot(pc, P)
        yield

    wu_u0 = _dot(P, jnp.concatenate([ab, av], axis=1))
    yield
    rhs = jnp.concatenate([wu_u0, jnp.concatenate([jnp.zeros_like(vb), vb], axis=1)], axis=0)
    qy = _dot(jnp.concatenate([a_qb, a_qk], axis=1), rhs)
    st = _dot(jnp.concatenate([bh, kh], axis=0).T, rhs)
    yield
    q_eff = qb + qy[:, :LANES]
    g_last = jnp.exp(jnp.broadcast_to(c_last, (n2, LANES)))
    m_mat = jnp.where(m['eye'], g_last, 0.0) + st[:, :LANES]
    return jnp.concatenate([q_eff, m_mat], axis=0), jnp.concatenate([qy[:, LANES:], st[:, LANES:]], axis=0)


def _rwkv_apply_state(lhs, add, H):
    L = RW_CHUNK
    out = _dot(lhs, H) + add
    return out[:L] + out[L:2 * L], out[2 * L:]


def _rwkv_prep(r, k, v, lora, vec, w2pad, a2pad, ones_bd, d, want_other_kd):
    lora_in_w = lora[:, 0:LANES]
    lora_in_a = lora[:, LANES:2 * LANES]
    kk = k * vec[4:5, :]
    ss = _group_sums(kk * kk, ones_bd)
    kk = kk * lax.rsqrt(jnp.maximum(ss, 1e-24))
    u = vec[d:d + 1, :] + jnp.dot(jnp.tanh(lora_in_w).astype(BF16), w2pad[d], preferred_element_type=F32)
    lw = -DECAY_SCALE * _sigmoid(u)
    a = _sigmoid(vec[2 + d:3 + d, :] + jnp.dot(lora_in_a.astype(BF16), a2pad[d], preferred_element_type=F32))
    kd = k * (1.0 + (a - 1.0) * vec[5:6, :])
    out = dict(r=r, k=k, v=v, kk=kk, lw=lw, a=a, kd=kd)
    if want_other_kd:
        o = 1 - d
        a_o = _sigmoid(vec[2 + o:3 + o, :] + jnp.dot(lora_in_a.astype(BF16), a2pad[o], preferred_element_type=F32))
        out['kd_other'] = k * (1.0 + (a_o - 1.0) * vec[5:6, :])
    return out


def _scan_body(rf_ref, kf_ref, vf_ref, lf_ref, rb_ref, kb_ref, vb_ref, lb_ref,
               qf_ref, ff_ref, if_ref, qb_ref, fb_ref, ib_ref,
               vec_ref, w2_ref, a2_ref, g2_ref, hglb_ref, h0_ref, s0_ref,
               yf_ref, yb_ref, bonus_ref, g_ref, of_ref, ob_ref, hfin_ref, sfin_ref, h_scr, s_scr):
    i = pl.program_id(1)
    n = pl.num_programs(1)

    @pl.when(i == 0)
    def _():
        h_scr[...] = h0_ref[0]
        s_scr[...] = s0_ref[0]

    vec = vec_ref[...]
    ones_bd = _ones_blocks(RW_HEAD)
    fw = _rwkv_prep(rf_ref[0], kf_ref[0], vf_ref[0], lf_ref[0], vec, w2_ref, a2_ref, ones_bd, 0, True)
    bw = _rwkv_prep(rb_ref[0], kb_ref[0], vb_ref[0], lb_ref[0], vec, w2_ref, a2_ref, ones_bd, 1, False)

    rk = fw['r'] * vec[6:7, :]
    bonus_ref[0] = _group_sums(rk * (fw['kd'] + fw['kd_other']), ones_bd) * fw['v']
    gd = lf_ref[0][:, 2 * LANES:3 * LANES]
    g_ref[0] = jnp.dot(_sigmoid(gd).astype(BF16), g2_ref[...], preferred_element_type=F32)

    chunk_order = ((0, 1), (1, 0))
    rw_masks = (_rwkv_masks(False), _rwkv_masks(True))
    hg_masks = (_gla_masks(False), _gla_masks(True))
    chains = []
    for d, t in enumerate((fw, bw)):
        for half in chunk_order[d]:
            rows = slice(half * RW_CHUNK, (half + 1) * RW_CHUNK)
            for p in range(RW_PAIRS):
                sl = slice(p * LANES, (p + 1) * LANES)
                chains.append(_rwkv_pair_chunk(t['r'][rows, sl], t['lw'][rows, sl], t['kd'][rows, sl],
                                               t['v'][rows, sl], t['kk'][rows, sl], t['a'][rows, sl], d == 1,
                                               rw_masks[d]))
    n_rw = len(chains)
    for d, (q_ref, f_ref, v_ref) in enumerate(((qf_ref, ff_ref, if_ref), (qb_ref, fb_ref, ib_ref))):
        for h in range(HG_HEADS):
            sl = slice(h * LANES, (h + 1) * LANES)
            chains.append(_gla_head_chunk(q_ref[0, :, sl], f_ref[0, :, sl], v_ref[0, :, sl],
                                          hglb_ref[d:d + 1, sl], s_scr[d, h], d == 1, hg_masks[d]))
    rw_start = [(0, RW_SECOND_ROUND)[(c // RW_PAIRS) % 2] for c in range(n_rw)]
    outs = _interleave(chains, rw_start + [HG_FIRST_ROUND] * (len(chains) - n_rw))

    states = [[h_scr[d, p] for p in range(RW_PAIRS)] for d in range(2)]
    for step in range(2):
        for d, y_ref in enumerate((yf_ref, yb_ref)):
            half = chunk_order[d][step]
            for p in range(RW_PAIRS):
                lhs, add = outs[(d * 2 + step) * RW_PAIRS + p]
                y, states[d][p] = _rwkv_apply_state(lhs, add, states[d][p])
                y_ref[0, half * RW_CHUNK:(half + 1) * RW_CHUNK, p * LANES:(p + 1) * LANES] = y
    for d in range(2):
        for p in range(RW_PAIRS):
            h_scr[d, p] = states[d][p]

    for d, o_ref in enumerate((of_ref, ob_ref)):
        for h in range(HG_HEADS):
            o, st_new = outs[n_rw + d * HG_HEADS + h]
            o_ref[0, :, h * LANES:(h + 1) * LANES] = o
            s_scr[d, h] = st_new

    @pl.when(i == n - 1)
    def _():
        hfin_ref[0] = h_scr[...]
        sfin_ref[0] = s_scr[...]


def recurrent_scans(proj, vec, w2pad, a2pad, g2, hg_lb, h0, s0t):
    B, T, _ = proj.shape
    n = T // HG_L
    fwd = lambda c: (lambda b, i: (b, i, c))
    bwd = lambda c: (lambda b, i: (b, n - 1 - i, c))
    wide = lambda fn: pl.BlockSpec((1, HG_L, RW_DIM), fn)
    lora = lambda fn: pl.BlockSpec((1, HG_L, RW_LORA_COLS), fn)
    const = lambda shape: pl.BlockSpec(shape, lambda b, i: (0,) * len(shape))
    rw_st = pl.BlockSpec((1, 2, RW_PAIRS, LANES, LANES), lambda b, i: (b, 0, 0, 0, 0))
    hg_st = pl.BlockSpec((1, 2, HG_HEADS, HG_DV, HG_DK), lambda b, i: (b, 0, 0, 0, 0))
    out_sd = jax.ShapeDtypeStruct((B, T, RW_DIM), F32)
    return pl.pallas_call(
        _scan_body,
        grid=(B, n),
        in_specs=[wide(fwd(COL_RW_R)), wide(fwd(COL_RW_K)), wide(fwd(COL_RW_V)), lora(fwd(COL_RW_LORA)),
                  wide(bwd(COL_RW_R)), wide(bwd(COL_RW_K)), wide(bwd(COL_RW_V)), lora(bwd(COL_RW_LORA)),
                  wide(fwd(COL_HG_Q)), wide(fwd(COL_HG_FF)), wide(fwd(COL_HG_I)),
                  wide(bwd(COL_HG_Q)), wide(bwd(COL_HG_FB)), wide(bwd(COL_HG_I)),
                  const((8, RW_DIM)), const((2, LANES, RW_DIM)), const((2, LANES, RW_DIM)),
                  const((LANES, RW_DIM)), const((2, HG_KDIM)), rw_st, hg_st],
        out_specs=[wide(fwd(0)), wide(bwd(0)), wide(fwd(0)), wide(fwd(0)), wide(fwd(0)), wide(bwd(0)),
                   rw_st, hg_st],
        out_shape=[out_sd] * 6 + [jax.ShapeDtypeStruct((B, 2, RW_PAIRS, LANES, LANES), F32),
                                  jax.ShapeDtypeStruct((B, 2, HG_HEADS, HG_DV, HG_DK), F32)],
        scratch_shapes=[pltpu.VMEM((2, RW_PAIRS, LANES, LANES), F32),
                        pltpu.VMEM((2, HG_HEADS, HG_DV, HG_DK), F32)],
        compiler_params=_params(("parallel", "arbitrary")),
        name="recurrent_scans",
    )(*([proj] * 14), vec, w2pad, a2pad, g2, hg_lb, h0, s0t)


def rwkv_pack_params(p):
    z = jnp.zeros((RW_HEAD, RW_DIM), F32)
    w2pad = jnp.stack([jnp.concatenate([p['rw_w2'][0], z], 0), jnp.concatenate([z, p['rw_w2'][1]], 0)]).astype(BF16)
    a2pad = jnp.stack([jnp.concatenate([p['rw_a2'][0], z], 0), jnp.concatenate([z, p['rw_a2'][1]], 0)]).astype(BF16)
    vec = jnp.stack([p['rw_w0'][0], p['rw_w0'][1], p['rw_a0'][0], p['rw_a0'][1], p['rw_k_k'], p['rw_k_a'],
                     p['rw_r_k'].reshape(RW_DIM), jnp.zeros((RW_DIM,), F32)])
    return vec, w2pad, a2pad, p['rw_g2'].astype(BF16)


def rwkv_state_to_blockdiag(s):
    B = s.shape[0]
    h = jnp.swapaxes(s, -1, -2).reshape(B, 2, RW_PAIRS, 2, RW_HEAD, RW_HEAD)
    z = jnp.zeros_like(h[:, :, :, 0])
    top = jnp.concatenate([h[:, :, :, 0], z], axis=-1)
    bot = jnp.concatenate([z, h[:, :, :, 1]], axis=-1)
    return jnp.concatenate([top, bot], axis=-2)


def rwkv_blockdiag_to_state(hb):
    B = hb.shape[0]
    h0 = hb[:, :, :, :RW_HEAD, :RW_HEAD]
    h1 = hb[:, :, :, RW_HEAD:, RW_HEAD:]
    h = jnp.stack([h0, h1], axis=3).reshape(B, 2, RW_HEADS, RW_HEAD, RW_HEAD)
    return jnp.swapaxes(h, -1, -2)


def _gla_masks(reverse):
    L = HG_L
    row = lax.broadcasted_iota(jnp.int32, (L, LANES), 0)
    ri = lax.broadcasted_iota(jnp.int32, (L, L), 0)
    ci = lax.broadcasted_iota(jnp.int32, (L, L), 1)
    levels = []
    half = 1
    while half < L:
        size = 2 * half
        upper = (row & half) != 0
        r_up = (ri & half) != 0
        c_up = (ci & half) != 0
        if reverse:
            is_q = jnp.logical_not(upper)
            pair = jnp.logical_and(jnp.logical_not(r_up), c_up)
        else:
            is_q = upper
            pair = jnp.logical_and(r_up, jnp.logical_not(c_up))
        if size < L:
            pair = jnp.logical_and(pair, (ri // size) == (ci // size))
        levels.append(dict(half=half, upper=upper, is_q=is_q, pair=pair,
                           scale=jnp.where(is_q, LOG2_E, -LOG2_E)))
        half = size
    return dict(eye=ri == ci, levels=levels)


def _gla_head_chunk(q_raw, fl, v, lb, St, reverse, m):
    L = q_raw.shape[0]
    q = q_raw * _sigmoid(q_raw)
    f = lb + (1.0 - lb) * _sigmoid(fl)
    k = 1.0 - f
    g = jnp.log(f)
    b = _cumsum_rows(g, reverse)
    b_last = b[0:1, :] if reverse else b[L - 1:L, :]

    att = jnp.where(m['eye'], _dot_nt(q, k), 0.0)
    fk = b
    for lv in m['levels']:
        half, upper, is_q = lv['half'], lv['upper'], lv['is_q']
        if reverse:
            bref = jnp.where(upper, fk, pltpu.roll(fk, L - half, 0))
        else:
            bref = jnp.where(upper, pltpu.roll(fk, half, 0), fk)
        e = jnp.exp2((b - bref) * lv['scale'])
        x = (jnp.where(is_q, q, k) * e).astype(BF16)
        prod = lax.dot_general(x, x, (((1,), (1,)), ((), ())), preferred_element_type=F32)
        yield
        att = jnp.where(lv['pair'], prod, att)
        if reverse:
            fk = jnp.where(upper, pltpu.roll(fk, half, 0), fk)
        else:
            fk = jnp.where(upper, fk, pltpu.roll(fk, L - half, 0))
    o = _dot(att, v) + _dot_nt(q * jnp.exp(b), St)
    kst = k * jnp.exp(b_last - b)
    st_new = St * jnp.exp(b_last) + _dot(v.T, kst)
    yield
    return o, st_new


def _branch_merge_body(yf_ref, yb_ref, bonus_ref, gr_ref, of_ref, ob_ref, gh_ref, pgr_ref, pgh_ref,
                       vec_ref, wr_ref, wh_ref, o_ref):
    vec = vec_ref[...]
    y = yf_ref[...] + yb_ref[...]
    ones_head = _ones_blocks(RW_HEAD)
    mean = _group_sums(y, ones_head) * (1.0 / RW_HEAD)
    dlt = y - mean
    var = _group_sums(dlt * dlt, ones_head) * (1.0 / RW_HEAD)
    yn = dlt * lax.rsqrt(var + RW_LN_EPS) * vec[0:1, :] + vec[1:2, :]
    out_r = (yn + bonus_ref[...]) * gr_ref[...]
    o = of_ref[...] + ob_ref[...]
    ms = _group_sums(o * o, _ones_blocks(LANES)) * (1.0 / HG_DV)
    gh = gh_ref[...]
    out_h = o * lax.rsqrt(ms + NORM_EPS) * vec[2:3, :] * (gh * _sigmoid(gh))
    br = _dot(out_r, wr_ref[...])
    bh = _dot(out_h, wh_ref[...])
    o_ref[...] = (_sigmoid(pgr_ref[...]) * br + _sigmoid(pgh_ref[...]) * bh).astype(o_ref.dtype)


def branch_merge(yf, yb, bonus, gr, of, ob, proj, vec, w_br_r, w_br_h, *, tm=256):
    M = yf.shape[0]
    row = lambda w, c: pl.BlockSpec((tm, w), lambda i: (i, c))
    const = lambda shape: pl.BlockSpec(shape, lambda i: (0,) * len(shape), pipeline_mode=pl.Buffered(1))
    return pl.pallas_call(
        _branch_merge_body,
        grid=(M // tm,),
        in_specs=[row(RW_DIM, 0), row(RW_DIM, 0), row(RW_DIM, 0), row(RW_DIM, 0),
                  row(HG_VDIM, 0), row(HG_VDIM, 0), row(HG_VDIM, COL_HG_G),
                  row(D_MODEL, COL_GATE_R), row(D_MODEL, COL_GATE_H),
                  const((8, RW_DIM)), const((RW_DIM, D_MODEL)), const((HG_VDIM, D_MODEL))],
        out_specs=pl.BlockSpec((tm, D_MODEL), lambda i: (i, 0)),
        out_shape=jax.ShapeDtypeStruct((M, D_MODEL), BF16),
        compiler_params=_params(("parallel",)),
        name="branch_merge",
    )(yf, yb, bonus, gr, of, ob, proj, proj, proj, vec, w_br_r, w_br_h)


def _out_proj_body(m_ref, x_ref, mod_ref, g_ref, w_ref, wr_ref, x1_ref, h2_ref, lg_ref):
    mod = mod_ref[0]
    D = D_MODEL
    x1 = x_ref[...] + mod[:, 2 * D:3 * D] * jnp.dot(m_ref[...], w_ref[...], preferred_element_type=F32)
    x1_ref[...] = x1
    y = x1 * lax.rsqrt(jnp.mean(x1 * x1, axis=-1, keepdims=True) + NORM_EPS) * g_ref[...]
    h2 = (y * (1.0 + mod[:, 4 * D:5 * D]) + mod[:, 3 * D:4 * D]).astype(BF16)
    h2_ref[...] = h2
    lg_ref[...] = jnp.dot(h2, wr_ref[...], preferred_element_type=F32)


def out_proj(merged, x, mod, norm_g, w_out, w_router, *, rows_per_mod, tm=512):
    M, D = x.shape
    per = rows_per_mod // tm
    return pl.pallas_call(
        _out_proj_body,
        grid=(M // tm,),
        in_specs=[pl.BlockSpec((tm, D), lambda i: (i, 0)),
                  pl.BlockSpec((tm, D), lambda i: (i, 0)),
                  pl.BlockSpec((1, 1, N_MOD * D), lambda i: (i // per, 0, 0)),
                  pl.BlockSpec((1, D), lambda i: (0, 0)),
                  pl.BlockSpec((D, D), lambda i: (0, 0), pipeline_mode=pl.Buffered(1)),
                  pl.BlockSpec((D, LANES), lambda i: (0, 0), pipeline_mode=pl.Buffered(1))],
        out_specs=[pl.BlockSpec((tm, D), lambda i: (i, 0)),
                   pl.BlockSpec((tm, D), lambda i: (i, 0)),
                   pl.BlockSpec((tm, LANES), lambda i: (i, 0))],
        out_shape=[jax.ShapeDtypeStruct((M, D), F32), jax.ShapeDtypeStruct((M, D), BF16),
                   jax.ShapeDtypeStruct((M, LANES), F32)],
        compiler_params=_params(("parallel",)),
        name="out_proj",
    )(merged, x, mod, norm_g, w_out, w_router)


def _experts_body(x_ref, wg_ref, wu_ref, wd_ref, val_ref, o_ref):
    x = x_ref[0]
    gate = jnp.dot(x, wg_ref[0], preferred_element_type=F32)
    up = jnp.dot(x, wu_ref[0], preferred_element_type=F32)
    hid = (gate * _sigmoid(gate) * up).astype(BF16)
    yo = jnp.dot(hid, wd_ref[0], preferred_element_type=F32)
    scale = jnp.concatenate([val_ref[0]] * (D_MODEL // LANES), axis=1)
    o_ref[0] = (yo * scale).astype(o_ref.dtype)


def moe_experts(xe, w_gate, w_up, w_down, vals, *, tm=512):
    E, M, D = xe.shape
    tm = min(tm, M)
    return pl.pallas_call(
        _experts_body,
        grid=(E, M // tm),
        in_specs=[pl.BlockSpec((1, tm, D), lambda e, i: (e, i, 0)),
                  pl.BlockSpec((1, D, EXPERT_FF), lambda e, i: (e, 0, 0)),
                  pl.BlockSpec((1, D, EXPERT_FF), lambda e, i: (e, 0, 0)),
                  pl.BlockSpec((1, EXPERT_FF, D), lambda e, i: (e, 0, 0)),
                  pl.BlockSpec((1, tm, LANES), lambda e, i: (e, i, 0))],
        out_specs=pl.BlockSpec((1, tm, D), lambda e, i: (e, i, 0)),
        out_shape=jax.ShapeDtypeStruct((E, M, D), BF16),
        compiler_params=_params(("parallel", "parallel")),
        name="moe_experts",
    )(xe, w_gate, w_up, w_down, vals)


def _combine_plan(tok_sorted, T):
    B, R = tok_sorted.shape
    n_tiles, n_chunks = T // COMBINE_ROWS, R // COMBINE_CHUNK
    n_pairs = n_tiles + n_chunks
    bounds = jnp.arange(n_tiles + 1, dtype=jnp.int32) * COMBINE_ROWS
    start = jnp.sum((tok_sorted[:, None, :] < bounds[None, :, None]).astype(jnp.int32), axis=-1)
    first_chunk = jnp.minimum(start[:, :-1] // COMBINE_CHUNK, n_chunks - 1)
    last_chunk = jnp.clip((start[:, 1:] - 1) // COMBINE_CHUNK, first_chunk, n_chunks - 1)
    count = last_chunk - first_chunk + 1
    offset = jnp.cumsum(count, axis=-1) - count
    total = offset[:, -1] + count[:, -1]
    pair = jnp.arange(n_pairs, dtype=jnp.int32)
    tile_of = jnp.sum((pair[None, :, None] >= offset[:, None, :]).astype(jnp.int32), axis=-1) - 1
    k_in = pair[None, :] - jnp.take_along_axis(offset, tile_of, axis=1)
    count_t = jnp.take_along_axis(count, tile_of, axis=1)
    chunk_of = jnp.take_along_axis(first_chunk, tile_of, axis=1) + jnp.minimum(k_in, count_t - 1)
    real = pair[None, :] < total[:, None]
    flags = (jnp.logical_and(real, k_in == 0).astype(jnp.int32)
             + 2 * real.astype(jnp.int32)
             + 4 * jnp.logical_and(real, k_in == count_t - 1).astype(jnp.int32))
    return tile_of, chunk_of, flags


def _moe_combine_body(tile_ref, chunk_ref, flag_ref, yo_ref, tok_ref, x1_ref, gt_ref, g_ref, o_ref, acc, *, final_norm):
    b = pl.program_id(0)
    p = pl.program_id(1)
    flags = flag_ref[b, p]
    rows = x1_ref.shape[1]

    @pl.when((flags & 1) != 0)
    def _():
        acc[...] = jnp.zeros(acc.shape, F32)

    @pl.when((flags & 2) != 0)
    def _():
        tok = lax.broadcasted_iota(jnp.int32, (rows, tok_ref.shape[3]), 0) + tile_ref[b, p] * rows
        onehot = jnp.where(tok == tok_ref[0, 0], 1.0, 0.0).astype(BF16)
        acc[...] += jnp.dot(onehot, yo_ref[0], preferred_element_type=F32)

    @pl.when((flags & 4) != 0)
    def _():
        x2 = x1_ref[0] + gt_ref[0] * acc[...]
        if final_norm:
            x2 = x2 * lax.rsqrt(jnp.mean(x2 * x2, axis=-1, keepdims=True) + NORM_EPS) * g_ref[...]
        o_ref[0] = x2


def moe_combine(yo_sorted, tok_sorted, x1, gt2, final_g):
    B, R, D = yo_sorted.shape
    T = x1.shape[1]
    per = B // gt2.shape[0]
    tile_of, chunk_of, flags = _combine_plan(tok_sorted, T)
    g = jnp.ones((1, D), F32) if final_g is None else final_g.reshape(1, D)
    grid_spec = pltpu.PrefetchScalarGridSpec(
        num_scalar_prefetch=3,
        grid=(B, tile_of.shape[1]),
        in_specs=[pl.BlockSpec((1, COMBINE_CHUNK, D), lambda b, p, t, c, f: (b, c[b, p], 0)),
                  pl.BlockSpec((1, 1, 1, COMBINE_CHUNK), lambda b, p, t, c, f: (b, c[b, p], 0, 0)),
                  pl.BlockSpec((1, COMBINE_ROWS, D), lambda b, p, t, c, f: (b, t[b, p], 0)),
                  pl.BlockSpec((1, 1, D), lambda b, p, t, c, f: (b // per, 0, 0)),
                  pl.BlockSpec((1, D), lambda b, p, t, c, f: (0, 0))],
        out_specs=pl.BlockSpec((1, COMBINE_ROWS, D), lambda b, p, t, c, f: (b, t[b, p], 0)),
        scratch_shapes=[pltpu.VMEM((COMBINE_ROWS, D), F32)])
    return pl.pallas_call(
        functools.partial(_moe_combine_body, final_norm=final_g is not None),
        grid_spec=grid_spec,
        out_shape=jax.ShapeDtypeStruct((B, T, D), F32),
        compiler_params=_params(("parallel", "arbitrary")),
        name="moe_combine",
    )(tile_of, chunk_of, flags, yo_sorted, tok_sorted.reshape(B, R // COMBINE_CHUNK, 1, COMBINE_CHUNK), x1, gt2, g)


def _expert_choice_moe(h2, logits, x1, gt2, final_g, p):
    B, T, D = h2.shape
    cap = CAPACITY_FACTOR * T // N_EXPERTS
    aff = jax.nn.softmax(logits[..., :N_EXPERTS], axis=-1)
    vals, idx = lax.top_k(jnp.swapaxes(aff, 1, 2), cap)
    bidx = jnp.arange(B)[:, None, None]
    xg = h2[bidx, idx]
    xe = jnp.swapaxes(xg, 0, 1).reshape(N_EXPERTS, B * cap, D)
    ve = jnp.broadcast_to(jnp.swapaxes(vals, 0, 1).reshape(N_EXPERTS, B * cap, 1), (N_EXPERTS, B * cap, LANES))
    yo = moe_experts(xe, p['moe_w_gate'], p['moe_w_up'], p['moe_w_down'], ve)
    tok = idx.reshape(B, N_EXPERTS * cap)
    order = jnp.argsort(tok, axis=-1)
    tok_sorted = jnp.take_along_axis(tok, order, axis=-1)
    src_row = (order // cap) * (B * cap) + jnp.arange(B, dtype=order.dtype)[:, None] * cap + order % cap
    yo_sorted = yo.reshape(N_EXPERTS * B * cap, D)[src_row]
    return moe_combine(yo_sorted, tok_sorted, x1, gt2, final_g)


def _layer(x, mod, s_rw, s_hg, width, final_g, p):
    B, T, D = x.shape
    rows_per_mod = (B * T) // mod.shape[0]
    xf = x.reshape(B * T, D)
    h = norm_modulate(xf, mod, p['norm1_g'], rows_per_mod=rows_per_mod)
    proj = in_proj(h, p['w_in'], p['mu'], width=width)
    proj3 = proj.reshape(B, T, PROJ_COLS)
    yf, yb, bonus, gr, of, ob, hfin, sfin = recurrent_scans(
        proj3, p['rw_vec'], p['rw_w2pad'], p['rw_a2pad'], p['rw_g2'], p['hg_lb'],
        rwkv_state_to_blockdiag(s_rw), jnp.swapaxes(s_hg, -1, -2))
    flat = lambda u: u.reshape(B * T, u.shape[-1])
    merged = branch_merge(flat(yf), flat(yb), flat(bonus), flat(gr), flat(of), flat(ob), proj,
                          p['br_vec'], p['w_br_rwkv'], p['w_br_hgrn'])
    x1, h2, logits = out_proj(merged, xf, mod, p['norm2_g'], p['w_out'], p['moe_router'],
                              rows_per_mod=rows_per_mod)
    x2 = _expert_choice_moe(h2.reshape(B, T, D), logits.reshape(B, T, LANES), x1.reshape(B, T, D),
                            mod[:, :, 5 * D:6 * D], final_g, p)
    return x2, rwkv_blockdiag_to_state(hfin), jnp.swapaxes(sfin, -1, -2)


def kernel(x_prompt, x_sample, c, state_rwkv, state_hgrn, c_ctx, norm1_g, norm2_g, final_norm_g, ada_w, ada_b, w_in, rw_mu, rw_w0, rw_w2, rw_a0, rw_a2, rw_g2, rw_k_k, rw_k_a, rw_r_k, rw_ln_w, rw_ln_b, hg_lb, hg_norm_g, w_br_rwkv, w_br_hgrn, w_out, moe_router, moe_w_gate, moe_w_up, moe_w_down):
    depth = norm1_g.shape[0]
    D = D_MODEL
    lb_all = jnp.cumsum(jax.nn.softmax(hg_lb.astype(F32), axis=0), axis=0)
    bp = x_prompt.shape[0]
    bs = x_sample.shape[0]
    xp, xs = x_prompt, x_sample
    new_rw, new_hg = [], []
    for l in range(depth):
        wl = w_in[l]
        w_perm = jnp.concatenate([wl[:, RW_COLS + HG_COLS:], wl[:, RW_COLS:RW_COLS + HG_COLS], wl[:, :RW_COLS],
                                  jnp.zeros((D, PROJ_COLS - IN_COLS), F32)], axis=1).astype(BF16)
        mu_perm = jnp.concatenate([jnp.zeros((2, GATE_COLS + HG_COLS), F32), rw_mu[l],
                                   jnp.zeros((2, PROJ_COLS - IN_COLS), F32)], axis=1)
        pr = dict(rw_w0=rw_w0[l], rw_w2=rw_w2[l], rw_a0=rw_a0[l], rw_a2=rw_a2[l], rw_g2=rw_g2[l],
                  rw_k_k=rw_k_k[l], rw_k_a=rw_k_a[l], rw_r_k=rw_r_k[l])
        rw_vec, rw_w2pad, rw_a2pad, rw_g2b = rwkv_pack_params(pr)
        zrow = jnp.zeros((RW_DIM,), F32)
        br_vec = jnp.stack([rw_ln_w[l], rw_ln_b[l], hg_norm_g[l], zrow, zrow, zrow, zrow, zrow])
        p = dict(norm1_g=norm1_g[l][None, :], norm2_g=norm2_g[l][None, :], w_in=w_perm, mu=mu_perm,
                 rw_vec=rw_vec, rw_w2pad=rw_w2pad, rw_a2pad=rw_a2pad, rw_g2=rw_g2b,
                 hg_lb=lb_all[l], br_vec=br_vec,
                 w_br_rwkv=w_br_rwkv[l].astype(BF16), w_br_hgrn=w_br_hgrn[l].astype(BF16),
                 w_out=w_out[l].astype(BF16),
                 moe_router=jnp.pad(moe_router[l], ((0, 0), (0, LANES - N_EXPERTS))).astype(BF16),
                 moe_w_gate=cast_experts_bf16(moe_w_gate[l]), moe_w_up=cast_experts_bf16(moe_w_up[l]),
                 moe_w_down=cast_experts_bf16(moe_w_down[l]))
        cond = jnp.concatenate([c_ctx[None, :], c], axis=0)
        cond = jnp.pad(jax.nn.silu(cond), ((0, 16 - 1 - bs), (0, 0)))
        mod = matmul(cond, ada_w[l], tm=16, tn=1024)[:1 + bs] + ada_b[l]
        mod = mod.reshape(1 + bs, 1, N_MOD * D)
        zero_rw = jnp.zeros((bp, 2, RW_HEADS, RW_HEAD, RW_HEAD), F32)
        zero_hg = jnp.zeros((bp, 2, HG_HEADS, HG_DK, HG_DV), F32)
        final_g = final_norm_g if l == depth - 1 else None
        xp, s_rw, s_hg = _layer(xp, mod[:1], zero_rw, zero_hg, xp.shape[1], final_g, p)
        new_rw.append(s_rw)
        new_hg.append(s_hg)
        xs, _, _ = _layer(xs, mod[1:], state_rwkv[:, l].astype(F32), state_hgrn[:, l].astype(F32), GRID_W,
                          final_g, p)
    return (xp, xs, jnp.stack(new_rw, axis=1), jnp.stack(new_hg, axis=1))
```

```python
import functools

import jax
import jax.numpy as jnp
from jax import lax
from jax.experimental import pallas as pl
from jax.experimental.pallas import tpu as pltpu

F32 = jnp.float32
BF16 = jnp.bfloat16

D_MODEL = 2048
GRID_W = 64
RW_HEADS = 16
RW_HEAD = 64
RW_DIM = RW_HEADS * RW_HEAD
DECAY_LORA = 64
AAA_LORA = 64
GATE_LORA = 128
RW_LORA_COLS = 2 * DECAY_LORA + 2 * AAA_LORA + GATE_LORA
RW_LN_EPS = 64e-5
HG_HEADS = 8
HG_DK = 128
HG_DV = 128
HG_KDIM = HG_HEADS * HG_DK
HG_VDIM = HG_HEADS * HG_DV
N_EXPERTS = 16
EXPERT_FF = 1024
CAPACITY_FACTOR = 2
NORM_EPS = 1e-6
N_MOD = 6
RW_COLS = 3 * RW_DIM + RW_LORA_COLS
HG_COLS = 3 * HG_KDIM + 2 * HG_VDIM
GATE_COLS = 2 * D_MODEL
IN_COLS = RW_COLS + HG_COLS + GATE_COLS

LANES = 128
BF16_SUBLANES = 16
V7X_VMEM_LIMIT_BYTES = 52 * 1024 * 1024

NORM_ROWS = 512
PROJ_ROWS = 1024
BRANCH_ROWS = 256
OUT_ROWS = 512
EXPERT_ROWS = 512
ADA_COLS = 1024

V7X_MXU_WIDTH = 256
PROJ_TN = 5 * V7X_MXU_WIDTH
PROJ_COLS = -(-IN_COLS // PROJ_TN) * PROJ_TN
PROJ_PLAIN_TILES = (GATE_COLS + HG_COLS) // PROJ_TN
COL_GATE_R, COL_GATE_H = 0, 1
COL_HG_Q, COL_HG_FF, COL_HG_FB, COL_HG_I, COL_HG_G = 4, 5, 6, 7, 8
COL_RW_R, COL_RW_K, COL_RW_V = 9, 10, 11
COL_RW_LORA = (GATE_COLS + HG_COLS + 3 * RW_DIM) // RW_LORA_COLS

RW_CHUNK = 64
RW_PAIRS = RW_DIM // LANES
HG_L = 128
RW_SECOND_ROUND = 0
HG_FIRST_ROUND = 4
COMBINE_ROWS = 256
COMBINE_CHUNK = 256
DECAY_SCALE = 0.6065306597126334
LOG2_E = 1.4426950408889634


def _params(sem):
    return pltpu.CompilerParams(dimension_semantics=sem, vmem_limit_bytes=V7X_VMEM_LIMIT_BYTES)


def _dot(a, b):
    return jnp.dot(a.astype(BF16), b.astype(BF16), preferred_element_type=F32)


def _dot_nt(a, b):
    return lax.dot_general(a.astype(BF16), b.astype(BF16), (((1,), (1,)), ((), ())),
                           preferred_element_type=F32)


def _split_dot(x, w_bf16):
    hi = x.astype(BF16)
    lo = (x - hi.astype(F32)).astype(BF16)
    return (jnp.dot(hi, w_bf16, preferred_element_type=F32)
            + jnp.dot(lo, w_bf16, preferred_element_type=F32))


def _group_sums(x, ones_bd):
    parts = [_split_dot(x[:, p * LANES:(p + 1) * LANES], ones_bd) for p in range(x.shape[1] // LANES)]
    return jnp.concatenate(parts, axis=1)


def _ones_blocks(group):
    ri = lax.broadcasted_iota(jnp.int32, (LANES, LANES), 0)
    ci = lax.broadcasted_iota(jnp.int32, (LANES, LANES), 1)
    return jnp.where((ri // group) == (ci // group), 1.0, 0.0).astype(BF16)


def _sigmoid(x):
    return 1.0 / (1.0 + jnp.exp(-x))


def _cumsum_rows(x, reverse):
    L = x.shape[0]
    row = lax.broadcasted_iota(jnp.int32, x.shape, 0)
    s = 1
    while s < L:
        if reverse:
            x = x + jnp.where(row < L - s, pltpu.roll(x, L - s, 0), 0.0)
        else:
            x = x + jnp.where(row >= s, pltpu.roll(x, s, 0), 0.0)
        s *= 2
    return x


def _interleave(gens, first_round=None):
    results = [None] * len(gens)
    first_round = first_round or [0] * len(gens)
    active = list(enumerate(gens))
    rnd = 0
    while active:
        still = []
        for idx, g in active:
            if rnd < first_round[idx]:
                still.append((idx, g))
                continue
            try:
                next(g)
                still.append((idx, g))
            except StopIteration as e:
                results[idx] = e.value
        active = still
        rnd += 1
    return results


def _mm_body(x_ref, w_ref, o_ref):
    o_ref[...] = _dot(x_ref[...], w_ref[...]).astype(o_ref.dtype)


def matmul(x, w, *, tm, tn, out_dtype=F32):
    M, K = x.shape
    _, N = w.shape
    return pl.pallas_call(
        _mm_body,
        grid=(M // tm, N // tn),
        in_specs=[pl.BlockSpec((tm, K), lambda i, j: (i, 0)),
                  pl.BlockSpec((K, tn), lambda i, j: (0, j))],
        out_specs=pl.BlockSpec((tm, tn), lambda i, j: (i, j)),
        out_shape=jax.ShapeDtypeStruct((M, N), out_dtype),
        compiler_params=_params(("parallel", "parallel")),
        name="matmul",
    )(x, w)


def _cast_body(w_ref, o_ref):
    o_ref[...] = w_ref[...].astype(o_ref.dtype)


def cast_experts_bf16(w):
    E, K, N = w.shape
    return pl.pallas_call(
        _cast_body,
        grid=(E,),
        in_specs=[pl.BlockSpec((1, K, N), lambda e: (e, 0, 0))],
        out_specs=pl.BlockSpec((1, K, N), lambda e: (e, 0, 0)),
        out_shape=jax.ShapeDtypeStruct((E, K, N), BF16),
        compiler_params=_params(("parallel",)),
        name="cast_experts_bf16",
    )(w)


def _norm_mod_body(x_ref, mod_ref, g_ref, o_ref):
    x = x_ref[...]
    y = x * lax.rsqrt(jnp.mean(x * x, axis=-1, keepdims=True) + NORM_EPS) * g_ref[...]
    m = mod_ref[0]
    o_ref[...] = (y * (1.0 + m[:, D_MODEL:2 * D_MODEL]) + m[:, 0:D_MODEL]).astype(o_ref.dtype)


def norm_modulate(x, mod, norm_g, *, rows_per_mod, tm=NORM_ROWS):
    M, D = x.shape
    per = rows_per_mod // tm
    return pl.pallas_call(
        _norm_mod_body,
        grid=(M // tm,),
        in_specs=[pl.BlockSpec((tm, D), lambda i: (i, 0)),
                  pl.BlockSpec((1, 1, N_MOD * D), lambda i: (i // per, 0, 0)),
                  pl.BlockSpec((1, D), lambda i: (0, 0))],
        out_specs=pl.BlockSpec((tm, D), lambda i: (i, 0)),
        out_shape=jax.ShapeDtypeStruct((M, D), BF16),
        compiler_params=_params(("parallel",)),
        name="norm_modulate",
    )(x, mod, norm_g)


def _in_proj_body(h_ref, w_ref, mu_ref, o_ref, *, width):
    j = pl.program_id(1)

    @pl.when(j < PROJ_PLAIN_TILES)
    def _():
        o_ref[...] = jnp.dot(h_ref[...], w_ref[...], preferred_element_type=F32)

    @pl.when(j >= PROJ_PLAIN_TILES)
    def _():
        acc = jnp.dot(h_ref[...], w_ref[...], preferred_element_type=F32)
        tm = acc.shape[0]
        pos = lax.broadcasted_iota(jnp.int32, acc.shape, 0) & (width - 1)
        prev = jnp.where(pos == 0, 0.0, pltpu.roll(acc, 1, 0))
        nxt = jnp.where(pos == width - 1, 0.0, pltpu.roll(acc, tm - 1, 0))
        mu = mu_ref[...]
        o_ref[...] = acc + mu[0:1, :] * (prev - acc) + mu[1:2, :] * (nxt - acc)


def in_proj(h, w_perm, mu_perm, *, width, tm=PROJ_ROWS):
    M, D = h.shape
    assert tm % width == 0
    return pl.pallas_call(
        functools.partial(_in_proj_body, width=width),
        grid=(M // tm, PROJ_COLS // PROJ_TN),
        in_specs=[pl.BlockSpec((tm, D), lambda i, j: (i, 0)),
                  pl.BlockSpec((D, PROJ_TN), lambda i, j: (0, j)),
                  pl.BlockSpec((2, PROJ_TN), lambda i, j: (0, j))],
        out_specs=pl.BlockSpec((tm, PROJ_TN), lambda i, j: (i, j)),
        out_shape=jax.ShapeDtypeStruct((M, PROJ_COLS), F32),
        compiler_params=_params(("parallel", "parallel")),
        name="in_proj",
    )(h, w_perm, mu_perm)


def _block_diag(x, lane_lo):
    return jnp.concatenate([jnp.where(lane_lo, x, 0.0), jnp.where(lane_lo, 0.0, x)], axis=0)


def _rwkv_masks(reverse):
    L = RW_CHUNK
    n2 = 2 * L
    ri = lax.broadcasted_iota(jnp.int32, (n2, n2), 0)
    ci = lax.broadcasted_iota(jnp.int32, (n2, n2), 1)
    rt = ri & (L - 1)
    ct = ci & (L - 1)
    blocks = []
    rb, cb = rt, ct
    s = 1
    while s < L:
        rb = rb >> 1
        cb = cb >> 1
        blocks.append(rb == cb)
        s *= 2
    couplings = [jnp.logical_and(blocks[j + 1], jnp.logical_not(blocks[j])) for j in range(len(blocks) - 1)]
    return dict(
        lane_lo=lax.broadcasted_iota(jnp.int32, (L, LANES), 1) < RW_HEAD,
        eye=ri == ci,
        strict=(rt < ct) if reverse else (rt > ct),
        incl=(rt <= ct) if reverse else (rt >= ct),
        pair_block=blocks[0],
        couplings=couplings)


def _rwkv_pair_chunk(q, lw, kd, v, kk, a, reverse, m):
    L = q.shape[0]
    n2 = 2 * L
    lane_lo = m['lane_lo']
    c = _cumsum_rows(lw, reverse)
    c_last = c[0:1, :] if reverse else c[L - 1:L, :]
    g_inc = jnp.exp(c)
    g_exc = jnp.exp(c - lw)
    g_inv = jnp.exp(-c)
    g_rem = jnp.exp(c_last - c)
    beta = kk * a
    ab = _block_diag(-kk * g_exc, lane_lo)
    qb = _block_diag(q * g_inc, lane_lo)
    bt = _block_diag(beta * g_inv, lane_lo)
    kt = _block_diag(kd * g_inv, lane_lo)
    vb = _block_diag(v, lane_lo)
    bh = _block_diag(beta * g_rem, lane_lo)
    kh = _block_diag(kd * g_rem, lane_lo)

    aa = _dot_nt(jnp.concatenate([ab, qb], axis=0), jnp.concatenate([bt, kt], axis=0))
    yield
    a_ab = jnp.where(m['strict'], aa[:n2, :n2], 0.0)
    a_ak = jnp.where(m['strict'], aa[:n2, n2:], 0.0)
    a_qb = jnp.where(m['incl'], aa[n2:, :n2], 0.0)
    a_qk = jnp.where(m['incl'], aa[n2:, n2:], 0.0)

    P = jnp.where(m['eye'], 1.0, 0.0) + jnp.where(m['pair_block'], a_ab, 0.0)
    av = _dot(a_ak, vb)
    for coupling_mask in m['couplings']:
        pc = _dot(P, jnp.where(coupling_mask, a_ab, 0.0))
        yield
        P = P + _dot(pc, P)
        yield

    wu_u0 = _dot(P, jnp.concatenate([ab, av], axis=1))
    yield
    rhs = jnp.concatenate([wu_u0, jnp.concatenate([jnp.zeros_like(vb), vb], axis=1)], axis=0)
    qy = _dot(jnp.concatenate([a_qb, a_qk], axis=1), rhs)
    st = _dot(jnp.concatenate([bh, kh], axis=0).T, rhs)
    yield
    q_eff = qb + qy[:, :LANES]
    g_last = jnp.exp(jnp.broadcast_to(c_last, (n2, LANES)))
    m_mat = jnp.where(m['eye'], g_last, 0.0) + st[:, :LANES]
    return jnp.concatenate([q_eff, m_mat], axis=0), jnp.concatenate([qy[:, LANES:], st[:, LANES:]], axis=0)


def _rwkv_apply_state(lhs, add, H):
    L = RW_CHUNK
    out = _dot(lhs, H) + add
    return out[:L] + out[L:2 * L], out[2 * L:]


def _rwkv_prep(r, k, v, lora, vec, w2pad, a2pad, ones_bd, d, want_other_kd):
    lora_in_w = lora[:, 0:LANES]
    lora_in_a = lora[:, LANES:2 * LANES]
    kk = k * vec[4:5, :]
    ss = _group_sums(kk * kk, ones_bd)
    kk = kk * lax.rsqrt(jnp.maximum(ss, 1e-24))
    u = vec[d:d + 1, :] + jnp.dot(jnp.tanh(lora_in_w).astype(BF16), w2pad[d], preferred_element_type=F32)
    lw = -DECAY_SCALE * _sigmoid(u)
    a = _sigmoid(vec[2 + d:3 + d, :] + jnp.dot(lora_in_a.astype(BF16), a2pad[d], preferred_element_type=F32))
    kd = k * (1.0 + (a - 1.0) * vec[5:6, :])
    out = dict(r=r, k=k, v=v, kk=kk, lw=lw, a=a, kd=kd)
    if want_other_kd:
        o = 1 - d
        a_o = _sigmoid(vec[2 + o:3 + o, :] + jnp.dot(lora_in_a.astype(BF16), a2pad[o], preferred_element_type=F32))
        out['kd_other'] = k * (1.0 + (a_o - 1.0) * vec[5:6, :])
    return out


def _scan_body(rf_ref, kf_ref, vf_ref, lf_ref, rb_ref, kb_ref, vb_ref, lb_ref,
               qf_ref, ff_ref, if_ref, qb_ref, fb_ref, ib_ref,
               vec_ref, w2_ref, a2_ref, g2_ref, hglb_ref, h0_ref, s0_ref,
               yf_ref, yb_ref, bonus_ref, g_ref, of_ref, ob_ref, hfin_ref, sfin_ref, h_scr, s_scr):
    i = pl.program_id(1)
    n = pl.num_programs(1)

    @pl.when(i == 0)
    def _():
        h_scr[...] = h0_ref[0]
        s_scr[...] = s0_ref[0]

    vec = vec_ref[...]
    ones_bd = _ones_blocks(RW_HEAD)
    fw = _rwkv_prep(rf_ref[0], kf_ref[0], vf_ref[0], lf_ref[0], vec, w2_ref, a2_ref, ones_bd, 0, True)
    bw = _rwkv_prep(rb_ref[0], kb_ref[0], vb_ref[0], lb_ref[0], vec, w2_ref, a2_ref, ones_bd, 1, False)

    rk = fw['r'] * vec[6:7, :]
    bonus_ref[0] = _group_sums(rk * (fw['kd'] + fw['kd_other']), ones_bd) * fw['v']
    gd = lf_ref[0][:, 2 * LANES:3 * LANES]
    g_ref[0] = jnp.dot(_sigmoid(gd).astype(BF16), g2_ref[...], preferred_element_type=F32)

    chunk_order = ((0, 1), (1, 0))
    rw_masks = (_rwkv_masks(False), _rwkv_masks(True))
    hg_masks = (_gla_masks(False), _gla_masks(True))
    chains = []
    for d, t in enumerate((fw, bw)):
        for half in chunk_order[d]:
            rows = slice(half * RW_CHUNK, (half + 1) * RW_CHUNK)
            for p in range(RW_PAIRS):
                sl = slice(p * LANES, (p + 1) * LANES)
                chains.append(_rwkv_pair_chunk(t['r'][rows, sl], t['lw'][rows, sl], t['kd'][rows, sl],
                                               t['v'][rows, sl], t['kk'][rows, sl], t['a'][rows, sl], d == 1,
                                               rw_masks[d]))
    n_rw = len(chains)
    for d, (q_ref, f_ref, v_ref) in enumerate(((qf_ref, ff_ref, if_ref), (qb_ref, fb_ref, ib_ref))):
        for h in range(HG_HEADS):
            sl = slice(h * LANES, (h + 1) * LANES)
            chains.append(_gla_head_chunk(q_ref[0, :, sl], f_ref[0, :, sl], v_ref[0, :, sl],
                                          hglb_ref[d:d + 1, sl], s_scr[d, h], d == 1, hg_masks[d]))
    rw_start = [(0, RW_SECOND_ROUND)[(c // RW_PAIRS) % 2] for c in range(n_rw)]
    outs = _interleave(chains, rw_start + [HG_FIRST_ROUND] * (len(chains) - n_rw))

    states = [[h_scr[d, p] for p in range(RW_PAIRS)] for d in range(2)]
    for step in range(2):
        for d, y_ref in enumerate((yf_ref, yb_ref)):
            half = chunk_order[d][step]
            for p in range(RW_PAIRS):
                lhs, add = outs[(d * 2 + step) * RW_PAIRS + p]
                y, states[d][p] = _rwkv_apply_state(lhs, add, states[d][p])
                y_ref[0, half * RW_CHUNK:(half + 1) * RW_CHUNK, p * LANES:(p + 1) * LANES] = y
    for d in range(2):
        for p in range(RW_PAIRS):
            h_scr[d, p] = states[d][p]

    for d, o_ref in enumerate((of_ref, ob_ref)):
        for h in range(HG_HEADS):
            o, st_new = outs[n_rw + d * HG_HEADS + h]
            o_ref[0, :, h * LANES:(h + 1) * LANES] = o
            s_scr[d, h] = st_new

    @pl.when(i == n - 1)
    def _():
        hfin_ref[0] = h_scr[...]
        sfin_ref[0] = s_scr[...]


def recurrent_scans(proj, vec, w2pad, a2pad, g2, hg_lb, h0, s0t):
    B, T, _ = proj.shape
    n = T // HG_L
    fwd = lambda c: (lambda b, i: (b, i, c))
    bwd = lambda c: (lambda b, i: (b, n - 1 - i, c))
    wide = lambda fn: pl.BlockSpec((1, HG_L, RW_DIM), fn)
    lora = lambda fn: pl.BlockSpec((1, HG_L, RW_LORA_COLS), fn)
    const = lambda shape: pl.BlockSpec(shape, lambda b, i: (0,) * len(shape))
    rw_st = pl.BlockSpec((1, 2, RW_PAIRS, LANES, LANES), lambda b, i: (b, 0, 0, 0, 0))
    hg_st = pl.BlockSpec((1, 2, HG_HEADS, HG_DV, HG_DK), lambda b, i: (b, 0, 0, 0, 0))
    out_sd = jax.ShapeDtypeStruct((B, T, RW_DIM), F32)
    return pl.pallas_call(
        _scan_body,
        grid=(B, n),
        in_specs=[wide(fwd(COL_RW_R)), wide(fwd(COL_RW_K)), wide(fwd(COL_RW_V)), lora(fwd(COL_RW_LORA)),
                  wide(bwd(COL_RW_R)), wide(bwd(COL_RW_K)), wide(bwd(COL_RW_V)), lora(bwd(COL_RW_LORA)),
                  wide(fwd(COL_HG_Q)), wide(fwd(COL_HG_FF)), wide(fwd(COL_HG_I)),
                  wide(bwd(COL_HG_Q)), wide(bwd(COL_HG_FB)), wide(bwd(COL_HG_I)),
                  const((8, RW_DIM)), const((2, LANES, RW_DIM)), const((2, LANES, RW_DIM)),
                  const((LANES, RW_DIM)), const((2, HG_KDIM)), rw_st, hg_st],
        out_specs=[wide(fwd(0)), wide(bwd(0)), wide(fwd(0)), wide(fwd(0)), wide(fwd(0)), wide(bwd(0)),
                   rw_st, hg_st],
        out_shape=[out_sd] * 6 + [jax.ShapeDtypeStruct((B, 2, RW_PAIRS, LANES, LANES), F32),
                                  jax.ShapeDtypeStruct((B, 2, HG_HEADS, HG_DV, HG_DK), F32)],
        scratch_shapes=[pltpu.VMEM((2, RW_PAIRS, LANES, LANES), F32),
                        pltpu.VMEM((2, HG_HEADS, HG_DV, HG_DK), F32)],
        compiler_params=_params(("parallel", "arbitrary")),
        name="recurrent_scans",
    )(*([proj] * 14), vec, w2pad, a2pad, g2, hg_lb, h0, s0t)


def rwkv_pack_params(p):
    z = jnp.zeros((RW_HEAD, RW_DIM), F32)
    w2pad = jnp.stack([jnp.concatenate([p['rw_w2'][0], z], 0), jnp.concatenate([z, p['rw_w2'][1]], 0)]).astype(BF16)
    a2pad = jnp.stack([jnp.concatenate([p['rw_a2'][0], z], 0), jnp.concatenate([z, p['rw_a2'][1]], 0)]).astype(BF16)
    vec = jnp.stack([p['rw_w0'][0], p['rw_w0'][1], p['rw_a0'][0], p['rw_a0'][1], p['rw_k_k'], p['rw_k_a'],
                     p['rw_r_k'].reshape(RW_DIM), jnp.zeros((RW_DIM,), F32)])
    return vec, w2pad, a2pad, p['rw_g2'].astype(BF16)


def rwkv_state_to_blockdiag(s):
    B = s.shape[0]
    h = jnp.swapaxes(s, -1, -2).reshape(B, 2, RW_PAIRS, 2, RW_HEAD, RW_HEAD)
    z = jnp.zeros_like(h[:, :, :, 0])
    top = jnp.concatenate([h[:, :, :, 0], z], axis=-1)
    bot = jnp.concatenate([z, h[:, :, :, 1]], axis=-1)
    return jnp.concatenate([top, bot], axis=-2)


def rwkv_blockdiag_to_state(hb):
    B = hb.shape[0]
    h0 = hb[:, :, :, :RW_HEAD, :RW_HEAD]
    h1 = hb[:, :, :, RW_HEAD:, RW_HEAD:]
    h = jnp.stack([h0, h1], axis=3).reshape(B, 2, RW_HEADS, RW_HEAD, RW_HEAD)
    return jnp.swapaxes(h, -1, -2)


def _gla_masks(reverse):
    L = HG_L
    row = lax.broadcasted_iota(jnp.int32, (L, LANES), 0)
    ri = lax.broadcasted_iota(jnp.int32, (L, L), 0)
    ci = lax.broadcasted_iota(jnp.int32, (L, L), 1)
    levels = []
    half = 1
    while half < L:
        size = 2 * half
        upper = (row & half) != 0
        r_up = (ri & half) != 0
        c_up = (ci & half) != 0
        if reverse:
            is_q = jnp.logical_not(upper)
            pair = jnp.logical_and(jnp.logical_not(r_up), c_up)
        else:
            is_q = upper
            pair = jnp.logical_and(r_up, jnp.logical_not(c_up))
        if size < L:
            pair = jnp.logical_and(pair, (ri // size) == (ci // size))
        levels.append(dict(half=half, upper=upper, is_q=is_q, pair=pair,
                           scale=jnp.where(is_q, LOG2_E, -LOG2_E)))
        half = size
    return dict(eye=ri == ci, levels=levels)


def _gla_head_chunk(q_raw, fl, v, lb, St, reverse, m):
    L = q_raw.shape[0]
    q = q_raw * _sigmoid(q_raw)
    f = lb + (1.0 - lb) * _sigmoid(fl)
    k = 1.0 - f
    g = jnp.log(f)
    b = _cumsum_rows(g, reverse)
    b_last = b[0:1, :] if reverse else b[L - 1:L, :]

    att = jnp.where(m['eye'], _dot_nt(q, k), 0.0)
    fk = b
    for lv in m['levels']:
        half, upper, is_q = lv['half'], lv['upper'], lv['is_q']
        if reverse:
            bref = jnp.where(upper, fk, pltpu.roll(fk, L - half, 0))
        else:
            bref = jnp.where(upper, pltpu.roll(fk, half, 0), fk)
        e = jnp.exp2((b - bref) * lv['scale'])
        x = (jnp.where(is_q, q, k) * e).astype(BF16)
        prod = lax.dot_general(x, x, (((1,), (1,)), ((), ())), preferred_element_type=F32)
        yield
        att = jnp.where(lv['pair'], prod, att)
        if reverse:
            fk = jnp.where(upper, pltpu.roll(fk, half, 0), fk)
        else:
            fk = jnp.where(upper, fk, pltpu.roll(fk, L - half, 0))
    o = _dot(att, v) + _dot_nt(q * jnp.exp(b), St)
    kst = k * jnp.exp(b_last - b)
    st_new = St * jnp.exp(b_last) + _dot(v.T, kst)
    yield
    return o, st_new


def _branch_merge_body(yf_ref, yb_ref, bonus_ref, gr_ref, of_ref, ob_ref, gh_ref, pgr_ref, pgh_ref,
                       vec_ref, wr_ref, wh_ref, o_ref):
    vec = vec_ref[...]
    y = yf_ref[...] + yb_ref[...]
    ones_head = _ones_blocks(RW_HEAD)
    mean = _group_sums(y, ones_head) * (1.0 / RW_HEAD)
    dlt = y - mean
    var = _group_sums(dlt * dlt, ones_head) * (1.0 / RW_HEAD)
    yn = dlt * lax.rsqrt(var + RW_LN_EPS) * vec[0:1, :] + vec[1:2, :]
    out_r = (yn + bonus_ref[...]) * gr_ref[...]
    o = of_ref[...] + ob_ref[...]
    ms = _group_sums(o * o, _ones_blocks(LANES)) * (1.0 / HG_DV)
    gh = gh_ref[...]
    out_h = o * lax.rsqrt(ms + NORM_EPS) * vec[2:3, :] * (gh * _sigmoid(gh))
    br = _dot(out_r, wr_ref[...])
    bh = _dot(out_h, wh_ref[...])
    o_ref[...] = (_sigmoid(pgr_ref[...]) * br + _sigmoid(pgh_ref[...]) * bh).astype(o_ref.dtype)


def branch_merge(yf, yb, bonus, gr, of, ob, proj, vec, w_br_r, w_br_h, *, tm=BRANCH_ROWS):
    M = yf.shape[0]
    row = lambda w, c: pl.BlockSpec((tm, w), lambda i: (i, c))
    const = lambda shape: pl.BlockSpec(shape, lambda i: (0,) * len(shape), pipeline_mode=pl.Buffered(1))
    return pl.pallas_call(
        _branch_merge_body,
        grid=(M // tm,),
        in_specs=[row(RW_DIM, 0), row(RW_DIM, 0), row(RW_DIM, 0), row(RW_DIM, 0),
                  row(HG_VDIM, 0), row(HG_VDIM, 0), row(HG_VDIM, COL_HG_G),
                  row(D_MODEL, COL_GATE_R), row(D_MODEL, COL_GATE_H),
                  const((8, RW_DIM)), const((RW_DIM, D_MODEL)), const((HG_VDIM, D_MODEL))],
        out_specs=pl.BlockSpec((tm, D_MODEL), lambda i: (i, 0)),
        out_shape=jax.ShapeDtypeStruct((M, D_MODEL), BF16),
        compiler_params=_params(("parallel",)),
        name="branch_merge",
    )(yf, yb, bonus, gr, of, ob, proj, proj, proj, vec, w_br_r, w_br_h)


def _out_proj_body(m_ref, x_ref, mod_ref, g_ref, w_ref, wr_ref, x1_ref, h2_ref, lg_ref):
    mod = mod_ref[0]
    D = D_MODEL
    x1 = x_ref[...] + mod[:, 2 * D:3 * D] * jnp.dot(m_ref[...], w_ref[...], preferred_element_type=F32)
    x1_ref[...] = x1
    y = x1 * lax.rsqrt(jnp.mean(x1 * x1, axis=-1, keepdims=True) + NORM_EPS) * g_ref[...]
    h2 = (y * (1.0 + mod[:, 4 * D:5 * D]) + mod[:, 3 * D:4 * D]).astype(BF16)
    h2_ref[...] = h2
    lg_ref[...] = jnp.dot(h2, wr_ref[...], preferred_element_type=F32)


def out_proj(merged, x, mod, norm_g, w_out, w_router, *, rows_per_mod, tm=OUT_ROWS):
    M, D = x.shape
    per = rows_per_mod // tm
    return pl.pallas_call(
        _out_proj_body,
        grid=(M // tm,),
        in_specs=[pl.BlockSpec((tm, D), lambda i: (i, 0)),
                  pl.BlockSpec((tm, D), lambda i: (i, 0)),
                  pl.BlockSpec((1, 1, N_MOD * D), lambda i: (i // per, 0, 0)),
                  pl.BlockSpec((1, D), lambda i: (0, 0)),
                  pl.BlockSpec((D, D), lambda i: (0, 0), pipeline_mode=pl.Buffered(1)),
                  pl.BlockSpec((D, LANES), lambda i: (0, 0), pipeline_mode=pl.Buffered(1))],
        out_specs=[pl.BlockSpec((tm, D), lambda i: (i, 0)),
                   pl.BlockSpec((tm, D), lambda i: (i, 0)),
                   pl.BlockSpec((tm, LANES), lambda i: (i, 0))],
        out_shape=[jax.ShapeDtypeStruct((M, D), F32), jax.ShapeDtypeStruct((M, D), BF16),
                   jax.ShapeDtypeStruct((M, LANES), F32)],
        compiler_params=_params(("parallel",)),
        name="out_proj",
    )(merged, x, mod, norm_g, w_out, w_router)


def _experts_body(x_ref, wg_ref, wu_ref, wd_ref, val_ref, o_ref):
    x = x_ref[0]
    gate = jnp.dot(x, wg_ref[0], preferred_element_type=F32)
    up = jnp.dot(x, wu_ref[0], preferred_element_type=F32)
    hid = (gate * _sigmoid(gate) * up).astype(BF16)
    yo = jnp.dot(hid, wd_ref[0], preferred_element_type=F32)
    scale = jnp.concatenate([val_ref[0]] * (D_MODEL // LANES), axis=1)
    o_ref[0] = (yo * scale).astype(o_ref.dtype)


def moe_experts(xe, w_gate, w_up, w_down, vals, *, tm=EXPERT_ROWS):
    E, M, D = xe.shape
    tm = min(tm, M)
    return pl.pallas_call(
        _experts_body,
        grid=(E, M // tm),
        in_specs=[pl.BlockSpec((1, tm, D), lambda e, i: (e, i, 0)),
                  pl.BlockSpec((1, D, EXPERT_FF), lambda e, i: (e, 0, 0)),
                  pl.BlockSpec((1, D, EXPERT_FF), lambda e, i: (e, 0, 0)),
                  pl.BlockSpec((1, EXPERT_FF, D), lambda e, i: (e, 0, 0)),
                  pl.BlockSpec((1, tm, LANES), lambda e, i: (e, i, 0))],
        out_specs=pl.BlockSpec((1, tm, D), lambda e, i: (e, i, 0)),
        out_shape=jax.ShapeDtypeStruct((E, M, D), BF16),
        compiler_params=_params(("parallel", "parallel")),
        name="moe_experts",
    )(xe, w_gate, w_up, w_down, vals)


def _combine_plan(tok_sorted, T):
    B, R = tok_sorted.shape
    n_tiles, n_chunks = T // COMBINE_ROWS, R // COMBINE_CHUNK
    n_pairs = n_tiles + n_chunks
    bounds = jnp.arange(n_tiles + 1, dtype=jnp.int32) * COMBINE_ROWS
    start = jnp.sum((tok_sorted[:, None, :] < bounds[None, :, None]).astype(jnp.int32), axis=-1)
    first_chunk = jnp.minimum(start[:, :-1] // COMBINE_CHUNK, n_chunks - 1)
    last_chunk = jnp.clip((start[:, 1:] - 1) // COMBINE_CHUNK, first_chunk, n_chunks - 1)
    count = last_chunk - first_chunk + 1
    offset = jnp.cumsum(count, axis=-1) - count
    total = offset[:, -1] + count[:, -1]
    pair = jnp.arange(n_pairs, dtype=jnp.int32)
    tile_of = jnp.sum((pair[None, :, None] >= offset[:, None, :]).astype(jnp.int32), axis=-1) - 1
    k_in = pair[None, :] - jnp.take_along_axis(offset, tile_of, axis=1)
    count_t = jnp.take_along_axis(count, tile_of, axis=1)
    chunk_of = jnp.take_along_axis(first_chunk, tile_of, axis=1) + jnp.minimum(k_in, count_t - 1)
    real = pair[None, :] < total[:, None]
    flags = (jnp.logical_and(real, k_in == 0).astype(jnp.int32)
             + 2 * real.astype(jnp.int32)
             + 4 * jnp.logical_and(real, k_in == count_t - 1).astype(jnp.int32))
    return tile_of, chunk_of, flags


def _moe_combine_body(tile_ref, chunk_ref, flag_ref, yo_ref, tok_ref, x1_ref, gt_ref, g_ref, o_ref, acc, *, final_norm):
    b = pl.program_id(0)
    p = pl.program_id(1)
    flags = flag_ref[b, p]
    rows = x1_ref.shape[1]

    @pl.when((flags & 1) != 0)
    def _():
        acc[...] = jnp.zeros(acc.shape, F32)

    @pl.when((flags & 2) != 0)
    def _():
        tok = lax.broadcasted_iota(jnp.int32, (rows, tok_ref.shape[3]), 0) + tile_ref[b, p] * rows
        onehot = jnp.where(tok == tok_ref[0, 0], 1.0, 0.0).astype(BF16)
        acc[...] += jnp.dot(onehot, yo_ref[0], preferred_element_type=F32)

    @pl.when((flags & 4) != 0)
    def _():
        x2 = x1_ref[0] + gt_ref[0] * acc[...]
        if final_norm:
            x2 = x2 * lax.rsqrt(jnp.mean(x2 * x2, axis=-1, keepdims=True) + NORM_EPS) * g_ref[...]
        o_ref[0] = x2


def moe_combine(yo_sorted, tok_sorted, x1, gt2, final_g):
    B, R, D = yo_sorted.shape
    T = x1.shape[1]
    per = B // gt2.shape[0]
    tile_of, chunk_of, flags = _combine_plan(tok_sorted, T)
    g = jnp.ones((1, D), F32) if final_g is None else final_g.reshape(1, D)
    grid_spec = pltpu.PrefetchScalarGridSpec(
        num_scalar_prefetch=3,
        grid=(B, tile_of.shape[1]),
        in_specs=[pl.BlockSpec((1, COMBINE_CHUNK, D), lambda b, p, t, c, f: (b, c[b, p], 0)),
                  pl.BlockSpec((1, 1, 1, COMBINE_CHUNK), lambda b, p, t, c, f: (b, c[b, p], 0, 0)),
                  pl.BlockSpec((1, COMBINE_ROWS, D), lambda b, p, t, c, f: (b, t[b, p], 0)),
                  pl.BlockSpec((1, 1, D), lambda b, p, t, c, f: (b // per, 0, 0)),
                  pl.BlockSpec((1, D), lambda b, p, t, c, f: (0, 0))],
        out_specs=pl.BlockSpec((1, COMBINE_ROWS, D), lambda b, p, t, c, f: (b, t[b, p], 0)),
        scratch_shapes=[pltpu.VMEM((COMBINE_ROWS, D), F32)])
    return pl.pallas_call(
        functools.partial(_moe_combine_body, final_norm=final_g is not None),
        grid_spec=grid_spec,
        out_shape=jax.ShapeDtypeStruct((B, T, D), F32),
        compiler_params=_params(("parallel", "arbitrary")),
        name="moe_combine",
    )(tile_of, chunk_of, flags, yo_sorted, tok_sorted.reshape(B, R // COMBINE_CHUNK, 1, COMBINE_CHUNK), x1, gt2, g)


def _expert_choice_moe(h2, logits, x1, gt2, final_g, p):
    B, T, D = h2.shape
    cap = CAPACITY_FACTOR * T // N_EXPERTS
    aff = jax.nn.softmax(logits[..., :N_EXPERTS], axis=-1)
    vals, idx = lax.top_k(jnp.swapaxes(aff, 1, 2), cap)
    bidx = jnp.arange(B)[:, None, None]
    xg = h2[bidx, idx]
    xe = jnp.swapaxes(xg, 0, 1).reshape(N_EXPERTS, B * cap, D)
    ve = jnp.broadcast_to(jnp.swapaxes(vals, 0, 1).reshape(N_EXPERTS, B * cap, 1), (N_EXPERTS, B * cap, LANES))
    yo = moe_experts(xe, p['moe_w_gate'], p['moe_w_up'], p['moe_w_down'], ve)
    tok = idx.reshape(B, N_EXPERTS * cap)
    order = jnp.argsort(tok, axis=-1)
    tok_sorted = jnp.take_along_axis(tok, order, axis=-1)
    src_row = (order // cap) * (B * cap) + jnp.arange(B, dtype=order.dtype)[:, None] * cap + order % cap
    yo_sorted = yo.reshape(N_EXPERTS * B * cap, D)[src_row]
    return moe_combine(yo_sorted, tok_sorted, x1, gt2, final_g)


def _layer(x, mod, s_rw, s_hg, width, final_g, p):
    B, T, D = x.shape
    rows_per_mod = (B * T) // mod.shape[0]
    xf = x.reshape(B * T, D)
    h = norm_modulate(xf, mod, p['norm1_g'], rows_per_mod=rows_per_mod)
    proj = in_proj(h, p['w_in'], p['mu'], width=width)
    proj3 = proj.reshape(B, T, PROJ_COLS)
    yf, yb, bonus, gr, of, ob, hfin, sfin = recurrent_scans(
        proj3, p['rw_vec'], p['rw_w2pad'], p['rw_a2pad'], p['rw_g2'], p['hg_lb'],
        rwkv_state_to_blockdiag(s_rw), jnp.swapaxes(s_hg, -1, -2))
    flat = lambda u: u.reshape(B * T, u.shape[-1])
    merged = branch_merge(flat(yf), flat(yb), flat(bonus), flat(gr), flat(of), flat(ob), proj,
                          p['br_vec'], p['w_br_rwkv'], p['w_br_hgrn'])
    x1, h2, logits = out_proj(merged, xf, mod, p['norm2_g'], p['w_out'], p['moe_router'],
                              rows_per_mod=rows_per_mod)
    x2 = _expert_choice_moe(h2.reshape(B, T, D), logits.reshape(B, T, LANES), x1.reshape(B, T, D),
                            mod[:, :, 5 * D:6 * D], final_g, p)
    return x2, rwkv_blockdiag_to_state(hfin), jnp.swapaxes(sfin, -1, -2)


def kernel(x_prompt, x_sample, c, state_rwkv, state_hgrn, c_ctx, norm1_g, norm2_g, final_norm_g, ada_w, ada_b, w_in, rw_mu, rw_w0, rw_w2, rw_a0, rw_a2, rw_g2, rw_k_k, rw_k_a, rw_r_k, rw_ln_w, rw_ln_b, hg_lb, hg_norm_g, w_br_rwkv, w_br_hgrn, w_out, moe_router, moe_w_gate, moe_w_up, moe_w_down):
    depth = norm1_g.shape[0]
    D = D_MODEL
    lb_all = jnp.cumsum(jax.nn.softmax(hg_lb.astype(F32), axis=0), axis=0)
    bp = x_prompt.shape[0]
    bs = x_sample.shape[0]
    xp, xs = x_prompt, x_sample
    new_rw, new_hg = [], []
    for l in range(depth):
        wl = w_in[l]
        w_perm = jnp.concatenate([wl[:, RW_COLS + HG_COLS:], wl[:, RW_COLS:RW_COLS + HG_COLS], wl[:, :RW_COLS],
                                  jnp.zeros((D, PROJ_COLS - IN_COLS), F32)], axis=1).astype(BF16)
        mu_perm = jnp.concatenate([jnp.zeros((2, GATE_COLS + HG_COLS), F32), rw_mu[l],
                                   jnp.zeros((2, PROJ_COLS - IN_COLS), F32)], axis=1)
        pr = dict(rw_w0=rw_w0[l], rw_w2=rw_w2[l], rw_a0=rw_a0[l], rw_a2=rw_a2[l], rw_g2=rw_g2[l],
                  rw_k_k=rw_k_k[l], rw_k_a=rw_k_a[l], rw_r_k=rw_r_k[l])
        rw_vec, rw_w2pad, rw_a2pad, rw_g2b = rwkv_pack_params(pr)
        zrow = jnp.zeros((RW_DIM,), F32)
        br_vec = jnp.stack([rw_ln_w[l], rw_ln_b[l], hg_norm_g[l], zrow, zrow, zrow, zrow, zrow])
        p = dict(norm1_g=norm1_g[l][None, :], norm2_g=norm2_g[l][None, :], w_in=w_perm, mu=mu_perm,
                 rw_vec=rw_vec, rw_w2pad=rw_w2pad, rw_a2pad=rw_a2pad, rw_g2=rw_g2b,
                 hg_lb=lb_all[l], br_vec=br_vec,
                 w_br_rwkv=w_br_rwkv[l].astype(BF16), w_br_hgrn=w_br_hgrn[l].astype(BF16),
                 w_out=w_out[l].astype(BF16),
                 moe_router=jnp.pad(moe_router[l], ((0, 0), (0, LANES - N_EXPERTS))).astype(BF16),
                 moe_w_gate=cast_experts_bf16(moe_w_gate[l]), moe_w_up=cast_experts_bf16(moe_w_up[l]),
                 moe_w_down=cast_experts_bf16(moe_w_down[l]))
        cond = jnp.concatenate([c_ctx[None, :], c], axis=0)
        rows = -(-(1 + bs) // BF16_SUBLANES) * BF16_SUBLANES
        cond = jnp.pad(jax.nn.silu(cond), ((0, rows - 1 - bs), (0, 0)))
        mod = matmul(cond, ada_w[l], tm=rows, tn=ADA_COLS)[:1 + bs] + ada_b[l]
        mod = mod.reshape(1 + bs, 1, N_MOD * D)
        zero_rw = jnp.zeros((bp, 2, RW_HEADS, RW_HEAD, RW_HEAD), F32)
        zero_hg = jnp.zeros((bp, 2, HG_HEADS, HG_DK, HG_DV), F32)
        final_g = final_norm_g if l == depth - 1 else None
        xp, s_rw, s_hg = _layer(xp, mod[:1], zero_rw, zero_hg, xp.shape[1], final_g, p)
        new_rw.append(s_rw)
        new_hg.append(s_hg)
        xs, _, _ = _layer(xs, mod[1:], state_rwkv[:, l].astype(F32), state_hgrn[:, l].astype(F32), GRID_W,
                          final_g, p)
    return (xp, xs, jnp.stack(new_rw, axis=1), jnp.stack(new_hg, axis=1))
```

```python
import functools

import jax
import jax.numpy as jnp
from jax import lax
from jax.experimental import pallas as pl
from jax.experimental.pallas import tpu as pltpu

F32 = jnp.float32
BF16 = jnp.bfloat16

D_MODEL = 2048
GRID_W = 64
RW_HEADS = 16
RW_HEAD = 64
RW_DIM = RW_HEADS * RW_HEAD
DECAY_LORA = 64
AAA_LORA = 64
GATE_LORA = 128
RW_LORA_COLS = 2 * DECAY_LORA + 2 * AAA_LORA + GATE_LORA
RW_LN_EPS = 64e-5
HG_HEADS = 8
HG_DK = 128
HG_DV = 128
HG_KDIM = HG_HEADS * HG_DK
HG_VDIM = HG_HEADS * HG_DV
N_EXPERTS = 16
EXPERT_FF = 1024
CAPACITY_FACTOR = 2
NORM_EPS = 1e-6
N_MOD = 6
RW_COLS = 3 * RW_DIM + RW_LORA_COLS
HG_COLS = 3 * HG_KDIM + 2 * HG_VDIM
GATE_COLS = 2 * D_MODEL
IN_COLS = RW_COLS + HG_COLS + GATE_COLS

LANES = 128
BF16_SUBLANES = 16
V7X_VMEM_LIMIT_BYTES = 52 * 1024 * 1024

NORM_ROWS = 512
PROJ_ROWS = 1024
BRANCH_ROWS = 256
OUT_ROWS = 512
EXPERT_ROWS = 512
ADA_COLS = 1024

V7X_MXU_WIDTH = 256
PROJ_TN = 5 * V7X_MXU_WIDTH
PROJ_COLS = -(-IN_COLS // PROJ_TN) * PROJ_TN
PROJ_PLAIN_TILES = (GATE_COLS + HG_COLS) // PROJ_TN
COL_GATE_R, COL_GATE_H = 0, 1
COL_HG_Q, COL_HG_FF, COL_HG_FB, COL_HG_I, COL_HG_G = 4, 5, 6, 7, 8
COL_RW_R, COL_RW_K, COL_RW_V = 9, 10, 11
COL_RW_LORA = (GATE_COLS + HG_COLS + 3 * RW_DIM) // RW_LORA_COLS

RW_CHUNK = 64
RW_PAIRS = RW_DIM // LANES
HG_L = 128
RW_SECOND_ROUND = 0
HG_FIRST_ROUND = 4
COMBINE_ROWS = 512
COMBINE_CHUNK = 512
DECAY_SCALE = 0.6065306597126334
LOG2_E = 1.4426950408889634


def _params(sem):
    return pltpu.CompilerParams(dimension_semantics=sem, vmem_limit_bytes=V7X_VMEM_LIMIT_BYTES)


def _dot(a, b):
    return jnp.dot(a.astype(BF16), b.astype(BF16), preferred_element_type=F32)


def _dot_nt(a, b):
    return lax.dot_general(a.astype(BF16), b.astype(BF16), (((1,), (1,)), ((), ())),
                           preferred_element_type=F32)


def _split_dot(x, w_bf16):
    hi = x.astype(BF16)
    lo = (x - hi.astype(F32)).astype(BF16)
    return (jnp.dot(hi, w_bf16, preferred_element_type=F32)
            + jnp.dot(lo, w_bf16, preferred_element_type=F32))


def _group_sums(x, ones_bd):
    parts = [_split_dot(x[:, p * LANES:(p + 1) * LANES], ones_bd) for p in range(x.shape[1] // LANES)]
    return jnp.concatenate(parts, axis=1)


def _ones_blocks(group):
    ri = lax.broadcasted_iota(jnp.int32, (LANES, LANES), 0)
    ci = lax.broadcasted_iota(jnp.int32, (LANES, LANES), 1)
    return jnp.where((ri // group) == (ci // group), 1.0, 0.0).astype(BF16)


def _sigmoid(x):
    return 1.0 / (1.0 + jnp.exp(-x))


def _cumsum_rows(x, reverse):
    L = x.shape[0]
    row = lax.broadcasted_iota(jnp.int32, x.shape, 0)
    s = 1
    while s < L:
        if reverse:
            x = x + jnp.where(row < L - s, pltpu.roll(x, L - s, 0), 0.0)
        else:
            x = x + jnp.where(row >= s, pltpu.roll(x, s, 0), 0.0)
        s *= 2
    return x


def _interleave(gens, first_round=None):
    results = [None] * len(gens)
    first_round = first_round or [0] * len(gens)
    active = list(enumerate(gens))
    rnd = 0
    while active:
        still = []
        for idx, g in active:
            if rnd < first_round[idx]:
                still.append((idx, g))
                continue
            try:
                next(g)
                still.append((idx, g))
            except StopIteration as e:
                results[idx] = e.value
        active = still
        rnd += 1
    return results


def _mm_body(x_ref, w_ref, o_ref):
    o_ref[...] = _dot(x_ref[...], w_ref[...]).astype(o_ref.dtype)


def matmul(x, w, *, tm, tn, out_dtype=F32):
    M, K = x.shape
    _, N = w.shape
    return pl.pallas_call(
        _mm_body,
        grid=(M // tm, N // tn),
        in_specs=[pl.BlockSpec((tm, K), lambda i, j: (i, 0)),
                  pl.BlockSpec((K, tn), lambda i, j: (0, j))],
        out_specs=pl.BlockSpec((tm, tn), lambda i, j: (i, j)),
        out_shape=jax.ShapeDtypeStruct((M, N), out_dtype),
        compiler_params=_params(("parallel", "parallel")),
        name="matmul",
    )(x, w)


def _cast_body(w_ref, o_ref):
    o_ref[...] = w_ref[...].astype(o_ref.dtype)


def cast_experts_bf16(w):
    E, K, N = w.shape
    return pl.pallas_call(
        _cast_body,
        grid=(E,),
        in_specs=[pl.BlockSpec((1, K, N), lambda e: (e, 0, 0))],
        out_specs=pl.BlockSpec((1, K, N), lambda e: (e, 0, 0)),
        out_shape=jax.ShapeDtypeStruct((E, K, N), BF16),
        compiler_params=_params(("parallel",)),
        name="cast_experts_bf16",
    )(w)


def _norm_mod_body(x_ref, mod_ref, g_ref, o_ref):
    x = x_ref[...]
    y = x * lax.rsqrt(jnp.mean(x * x, axis=-1, keepdims=True) + NORM_EPS) * g_ref[...]
    m = mod_ref[0]
    o_ref[...] = (y * (1.0 + m[:, D_MODEL:2 * D_MODEL]) + m[:, 0:D_MODEL]).astype(o_ref.dtype)


def norm_modulate(x, mod, norm_g, *, rows_per_mod, tm=NORM_ROWS):
    M, D = x.shape
    per = rows_per_mod // tm
    return pl.pallas_call(
        _norm_mod_body,
        grid=(M // tm,),
        in_specs=[pl.BlockSpec((tm, D), lambda i: (i, 0)),
                  pl.BlockSpec((1, 1, N_MOD * D), lambda i: (i // per, 0, 0)),
                  pl.BlockSpec((1, D), lambda i: (0, 0))],
        out_specs=pl.BlockSpec((tm, D), lambda i: (i, 0)),
        out_shape=jax.ShapeDtypeStruct((M, D), BF16),
        compiler_params=_params(("parallel",)),
        name="norm_modulate",
    )(x, mod, norm_g)


def _in_proj_body(h_ref, w_ref, mu_ref, o_ref, *, width):
    j = pl.program_id(1)

    @pl.when(j < PROJ_PLAIN_TILES)
    def _():
        o_ref[...] = jnp.dot(h_ref[...], w_ref[...], preferred_element_type=F32)

    @pl.when(j >= PROJ_PLAIN_TILES)
    def _():
        acc = jnp.dot(h_ref[...], w_ref[...], preferred_element_type=F32)
        tm = acc.shape[0]
        pos = lax.broadcasted_iota(jnp.int32, acc.shape, 0) & (width - 1)
        prev = jnp.where(pos == 0, 0.0, pltpu.roll(acc, 1, 0))
        nxt = jnp.where(pos == width - 1, 0.0, pltpu.roll(acc, tm - 1, 0))
        mu = mu_ref[...]
        o_ref[...] = acc + mu[0:1, :] * (prev - acc) + mu[1:2, :] * (nxt - acc)


def in_proj(h, w_perm, mu_perm, *, width, tm=PROJ_ROWS):
    M, D = h.shape
    assert tm % width == 0
    return pl.pallas_call(
        functools.partial(_in_proj_body, width=width),
        grid=(M // tm, PROJ_COLS // PROJ_TN),
        in_specs=[pl.BlockSpec((tm, D), lambda i, j: (i, 0)),
                  pl.BlockSpec((D, PROJ_TN), lambda i, j: (0, j)),
                  pl.BlockSpec((2, PROJ_TN), lambda i, j: (0, j))],
        out_specs=pl.BlockSpec((tm, PROJ_TN), lambda i, j: (i, j)),
        out_shape=jax.ShapeDtypeStruct((M, PROJ_COLS), F32),
        compiler_params=_params(("parallel", "parallel")),
        name="in_proj",
    )(h, w_perm, mu_perm)


def _block_diag(x, lane_lo):
    return jnp.concatenate([jnp.where(lane_lo, x, 0.0), jnp.where(lane_lo, 0.0, x)], axis=0)


def _rwkv_masks(reverse):
    L = RW_CHUNK
    n2 = 2 * L
    ri = lax.broadcasted_iota(jnp.int32, (n2, n2), 0)
    ci = lax.broadcasted_iota(jnp.int32, (n2, n2), 1)
    rt = ri & (L - 1)
    ct = ci & (L - 1)
    blocks = []
    rb, cb = rt, ct
    s = 1
    while s < L:
        rb = rb >> 1
        cb = cb >> 1
        blocks.append(rb == cb)
        s *= 2
    couplings = [jnp.logical_and(blocks[j + 1], jnp.logical_not(blocks[j])) for j in range(len(blocks) - 1)]
    return dict(
        lane_lo=lax.broadcasted_iota(jnp.int32, (L, LANES), 1) < RW_HEAD,
        eye=ri == ci,
        strict=(rt < ct) if reverse else (rt > ct),
        incl=(rt <= ct) if reverse else (rt >= ct),
        pair_block=blocks[0],
        couplings=couplings)


def _rwkv_pair_chunk(q, lw, kd, v, kk, a, reverse, m):
    L = q.shape[0]
    n2 = 2 * L
    lane_lo = m['lane_lo']
    c = _cumsum_rows(lw, reverse)
    c_last = c[0:1, :] if reverse else c[L - 1:L, :]
    g_inc = jnp.exp(c)
    g_exc = jnp.exp(c - lw)
    g_inv = jnp.exp(-c)
    g_rem = jnp.exp(c_last - c)
    beta = kk * a
    ab = _block_diag(-kk * g_exc, lane_lo)
    qb = _block_diag(q * g_inc, lane_lo)
    bt = _block_diag(beta * g_inv, lane_lo)
    kt = _block_diag(kd * g_inv, lane_lo)
    vb = _block_diag(v, lane_lo)
    bh = _block_diag(beta * g_rem, lane_lo)
    kh = _block_diag(kd * g_rem, lane_lo)

    aa = _dot_nt(jnp.concatenate([ab, qb], axis=0), jnp.concatenate([bt, kt], axis=0))
    yield
    a_ab = jnp.where(m['strict'], aa[:n2, :n2], 0.0)
    a_ak = jnp.where(m['strict'], aa[:n2, n2:], 0.0)
    a_qb = jnp.where(m['incl'], aa[n2:, :n2], 0.0)
    a_qk = jnp.where(m['incl'], aa[n2:, n2:], 0.0)

    P = jnp.where(m['eye'], 1.0, 0.0) + jnp.where(m['pair_block'], a_ab, 0.0)
    av = _dot(a_ak, vb)
    for coupling_mask in m['couplings']:
        pc = _dot(P, jnp.where(coupling_mask, a_ab, 0.0))
        yield
        P = P + _dot(pc, P)
        yield

    wu_u0 = _dot(P, jnp.concatenate([ab, av], axis=1))
    yield
    rhs = jnp.concatenate([wu_u0, jnp.concatenate([jnp.zeros_like(vb), vb], axis=1)], axis=0)
    qy = _dot(jnp.concatenate([a_qb, a_qk], axis=1), rhs)
    st = _dot(jnp.concatenate([bh, kh], axis=0).T, rhs)
    yield
    q_eff = qb + qy[:, :LANES]
    g_last = jnp.exp(jnp.broadcast_to(c_last, (n2, LANES)))
    m_mat = jnp.where(m['eye'], g_last, 0.0) + st[:, :LANES]
    return jnp.concatenate([q_eff, m_mat], axis=0), jnp.concatenate([qy[:, LANES:], st[:, LANES:]], axis=0)


def _rwkv_apply_state(lhs, add, H):
    L = RW_CHUNK
    out = _dot(lhs, H) + add
    return out[:L] + out[L:2 * L], out[2 * L:]


def _rwkv_prep(r, k, v, lora, vec, w2pad, a2pad, ones_bd, d, want_other_kd):
    lora_in_w = lora[:, 0:LANES]
    lora_in_a = lora[:, LANES:2 * LANES]
    kk = k * vec[4:5, :]
    ss = _group_sums(kk * kk, ones_bd)
    kk = kk * lax.rsqrt(jnp.maximum(ss, 1e-24))
    u = vec[d:d + 1, :] + jnp.dot(jnp.tanh(lora_in_w).astype(BF16), w2pad[d], preferred_element_type=F32)
    lw = -DECAY_SCALE * _sigmoid(u)
    a = _sigmoid(vec[2 + d:3 + d, :] + jnp.dot(lora_in_a.astype(BF16), a2pad[d], preferred_element_type=F32))
    kd = k * (1.0 + (a - 1.0) * vec[5:6, :])
    out = dict(r=r, k=k, v=v, kk=kk, lw=lw, a=a, kd=kd)
    if want_other_kd:
        o = 1 - d
        a_o = _sigmoid(vec[2 + o:3 + o, :] + jnp.dot(lora_in_a.astype(BF16), a2pad[o], preferred_element_type=F32))
        out['kd_other'] = k * (1.0 + (a_o - 1.0) * vec[5:6, :])
    return out


def _scan_body(rf_ref, kf_ref, vf_ref, lf_ref, rb_ref, kb_ref, vb_ref, lb_ref,
               qf_ref, ff_ref, if_ref, qb_ref, fb_ref, ib_ref,
               vec_ref, w2_ref, a2_ref, g2_ref, hglb_ref, h0_ref, s0_ref,
               yf_ref, yb_ref, bonus_ref, g_ref, of_ref, ob_ref, hfin_ref, sfin_ref, h_scr, s_scr):
    i = pl.program_id(1)
    n = pl.num_programs(1)

    @pl.when(i == 0)
    def _():
        h_scr[...] = h0_ref[0]
        s_scr[...] = s0_ref[0]

    vec = vec_ref[...]
    ones_bd = _ones_blocks(RW_HEAD)
    fw = _rwkv_prep(rf_ref[0], kf_ref[0], vf_ref[0], lf_ref[0], vec, w2_ref, a2_ref, ones_bd, 0, True)
    bw = _rwkv_prep(rb_ref[0], kb_ref[0], vb_ref[0], lb_ref[0], vec, w2_ref, a2_ref, ones_bd, 1, False)

    rk = fw['r'] * vec[6:7, :]
    bonus_ref[0] = _group_sums(rk * (fw['kd'] + fw['kd_other']), ones_bd) * fw['v']
    gd = lf_ref[0][:, 2 * LANES:3 * LANES]
    g_ref[0] = jnp.dot(_sigmoid(gd).astype(BF16), g2_ref[...], preferred_element_type=F32)

    chunk_order = ((0, 1), (1, 0))
    rw_masks = (_rwkv_masks(False), _rwkv_masks(True))
    hg_masks = (_gla_masks(False), _gla_masks(True))
    chains = []
    for d, t in enumerate((fw, bw)):
        for half in chunk_order[d]:
            rows = slice(half * RW_CHUNK, (half + 1) * RW_CHUNK)
            for p in range(RW_PAIRS):
                sl = slice(p * LANES, (p + 1) * LANES)
                chains.append(_rwkv_pair_chunk(t['r'][rows, sl], t['lw'][rows, sl], t['kd'][rows, sl],
                                               t['v'][rows, sl], t['kk'][rows, sl], t['a'][rows, sl], d == 1,
                                               rw_masks[d]))
    n_rw = len(chains)
    for d, (q_ref, f_ref, v_ref) in enumerate(((qf_ref, ff_ref, if_ref), (qb_ref, fb_ref, ib_ref))):
        for h in range(HG_HEADS):
            sl = slice(h * LANES, (h + 1) * LANES)
            chains.append(_gla_head_chunk(q_ref[0, :, sl], f_ref[0, :, sl], v_ref[0, :, sl],
                                          hglb_ref[d:d + 1, sl], s_scr[d, h], d == 1, hg_masks[d]))
    rw_start = [(0, RW_SECOND_ROUND)[(c // RW_PAIRS) % 2] for c in range(n_rw)]
    outs = _interleave(chains, rw_start + [HG_FIRST_ROUND] * (len(chains) - n_rw))

    states = [[h_scr[d, p] for p in range(RW_PAIRS)] for d in range(2)]
    for step in range(2):
        for d, y_ref in enumerate((yf_ref, yb_ref)):
            half = chunk_order[d][step]
            for p in range(RW_PAIRS):
                lhs, add = outs[(d * 2 + step) * RW_PAIRS + p]
                y, states[d][p] = _rwkv_apply_state(lhs, add, states[d][p])
                y_ref[0, half * RW_CHUNK:(half + 1) * RW_CHUNK, p * LANES:(p + 1) * LANES] = y
    for d in range(2):
        for p in range(RW_PAIRS):
            h_scr[d, p] = states[d][p]

    for d, o_ref in enumerate((of_ref, ob_ref)):
        for h in range(HG_HEADS):
            o, st_new = outs[n_rw + d * HG_HEADS + h]
            o_ref[0, :, h * LANES:(h + 1) * LANES] = o
            s_scr[d, h] = st_new

    @pl.when(i == n - 1)
    def _():
        hfin_ref[0] = h_scr[...]
        sfin_ref[0] = s_scr[...]


def recurrent_scans(proj, vec, w2pad, a2pad, g2, hg_lb, h0, s0t):
    B, T, _ = proj.shape
    n = T // HG_L
    fwd = lambda c: (lambda b, i: (b, i, c))
    bwd = lambda c: (lambda b, i: (b, n - 1 - i, c))
    wide = lambda fn: pl.BlockSpec((1, HG_L, RW_DIM), fn)
    lora = lambda fn: pl.BlockSpec((1, HG_L, RW_LORA_COLS), fn)
    const = lambda shape: pl.BlockSpec(shape, lambda b, i: (0,) * len(shape))
    rw_st = pl.BlockSpec((1, 2, RW_PAIRS, LANES, LANES), lambda b, i: (b, 0, 0, 0, 0))
    hg_st = pl.BlockSpec((1, 2, HG_HEADS, HG_DV, HG_DK), lambda b, i: (b, 0, 0, 0, 0))
    out_sd = jax.ShapeDtypeStruct((B, T, RW_DIM), F32)
    return pl.pallas_call(
        _scan_body,
        grid=(B, n),
        in_specs=[wide(fwd(COL_RW_R)), wide(fwd(COL_RW_K)), wide(fwd(COL_RW_V)), lora(fwd(COL_RW_LORA)),
                  wide(bwd(COL_RW_R)), wide(bwd(COL_RW_K)), wide(bwd(COL_RW_V)), lora(bwd(COL_RW_LORA)),
                  wide(fwd(COL_HG_Q)), wide(fwd(COL_HG_FF)), wide(fwd(COL_HG_I)),
                  wide(bwd(COL_HG_Q)), wide(bwd(COL_HG_FB)), wide(bwd(COL_HG_I)),
                  const((8, RW_DIM)), const((2, LANES, RW_DIM)), const((2, LANES, RW_DIM)),
                  const((LANES, RW_DIM)), const((2, HG_KDIM)), rw_st, hg_st],
        out_specs=[wide(fwd(0)), wide(bwd(0)), wide(fwd(0)), wide(fwd(0)), wide(fwd(0)), wide(bwd(0)),
                   rw_st, hg_st],
        out_shape=[out_sd] * 6 + [jax.ShapeDtypeStruct((B, 2, RW_PAIRS, LANES, LANES), F32),
                                  jax.ShapeDtypeStruct((B, 2, HG_HEADS, HG_DV, HG_DK), F32)],
        scratch_shapes=[pltpu.VMEM((2, RW_PAIRS, LANES, LANES), F32),
                        pltpu.VMEM((2, HG_HEADS, HG_DV, HG_DK), F32)],
        compiler_params=_params(("parallel", "arbitrary")),
        name="recurrent_scans",
    )(*([proj] * 14), vec, w2pad, a2pad, g2, hg_lb, h0, s0t)


def rwkv_pack_params(p):
    z = jnp.zeros((RW_HEAD, RW_DIM), F32)
    w2pad = jnp.stack([jnp.concatenate([p['rw_w2'][0], z], 0), jnp.concatenate([z, p['rw_w2'][1]], 0)]).astype(BF16)
    a2pad = jnp.stack([jnp.concatenate([p['rw_a2'][0], z], 0), jnp.concatenate([z, p['rw_a2'][1]], 0)]).astype(BF16)
    vec = jnp.stack([p['rw_w0'][0], p['rw_w0'][1], p['rw_a0'][0], p['rw_a0'][1], p['rw_k_k'], p['rw_k_a'],
                     p['rw_r_k'].reshape(RW_DIM), jnp.zeros((RW_DIM,), F32)])
    return vec, w2pad, a2pad, p['rw_g2'].astype(BF16)


def rwkv_state_to_blockdiag(s):
    B = s.shape[0]
    h = jnp.swapaxes(s, -1, -2).reshape(B, 2, RW_PAIRS, 2, RW_HEAD, RW_HEAD)
    z = jnp.zeros_like(h[:, :, :, 0])
    top = jnp.concatenate([h[:, :, :, 0], z], axis=-1)
    bot = jnp.concatenate([z, h[:, :, :, 1]], axis=-1)
    return jnp.concatenate([top, bot], axis=-2)


def rwkv_blockdiag_to_state(hb):
    B = hb.shape[0]
    h0 = hb[:, :, :, :RW_HEAD, :RW_HEAD]
    h1 = hb[:, :, :, RW_HEAD:, RW_HEAD:]
    h = jnp.stack([h0, h1], axis=3).reshape(B, 2, RW_HEADS, RW_HEAD, RW_HEAD)
    return jnp.swapaxes(h, -1, -2)


def _gla_masks(reverse):
    L = HG_L
    row = lax.broadcasted_iota(jnp.int32, (L, LANES), 0)
    ri = lax.broadcasted_iota(jnp.int32, (L, L), 0)
    ci = lax.broadcasted_iota(jnp.int32, (L, L), 1)
    levels = []
    half = 1
    while half < L:
        size = 2 * half
        upper = (row & half) != 0
        r_up = (ri & half) != 0
        c_up = (ci & half) != 0
        if reverse:
            is_q = jnp.logical_not(upper)
            pair = jnp.logical_and(jnp.logical_not(r_up), c_up)
        else:
            is_q = upper
            pair = jnp.logical_and(r_up, jnp.logical_not(c_up))
        if size < L:
            pair = jnp.logical_and(pair, (ri // size) == (ci // size))
        levels.append(dict(half=half, upper=upper, is_q=is_q, pair=pair,
                           scale=jnp.where(is_q, LOG2_E, -LOG2_E)))
        half = size
    return dict(eye=ri == ci, levels=levels)


def _gla_head_chunk(q_raw, fl, v, lb, St, reverse, m):
    L = q_raw.shape[0]
    q = q_raw * _sigmoid(q_raw)
    f = lb + (1.0 - lb) * _sigmoid(fl)
    k = 1.0 - f
    g = jnp.log(f)
    b = _cumsum_rows(g, reverse)
    b_last = b[0:1, :] if reverse else b[L - 1:L, :]

    att = jnp.where(m['eye'], _dot_nt(q, k), 0.0)
    fk = b
    for lv in m['levels']:
        half, upper, is_q = lv['half'], lv['upper'], lv['is_q']
        if reverse:
            bref = jnp.where(upper, fk, pltpu.roll(fk, L - half, 0))
        else:
            bref = jnp.where(upper, pltpu.roll(fk, half, 0), fk)
        e = jnp.exp2((b - bref) * lv['scale'])
        x = (jnp.where(is_q, q, k) * e).astype(BF16)
        prod = lax.dot_general(x, x, (((1,), (1,)), ((), ())), preferred_element_type=F32)
        yield
        att = jnp.where(lv['pair'], prod, att)
        if reverse:
            fk = jnp.where(upper, pltpu.roll(fk, half, 0), fk)
        else:
            fk = jnp.where(upper, fk, pltpu.roll(fk, L - half, 0))
    o = _dot(att, v) + _dot_nt(q * jnp.exp(b), St)
    kst = k * jnp.exp(b_last - b)
    st_new = St * jnp.exp(b_last) + _dot(v.T, kst)
    yield
    return o, st_new


def _branch_merge_body(yf_ref, yb_ref, bonus_ref, gr_ref, of_ref, ob_ref, gh_ref, pgr_ref, pgh_ref,
                       vec_ref, wr_ref, wh_ref, o_ref):
    vec = vec_ref[...]
    y = yf_ref[...] + yb_ref[...]
    ones_head = _ones_blocks(RW_HEAD)
    mean = _group_sums(y, ones_head) * (1.0 / RW_HEAD)
    dlt = y - mean
    var = _group_sums(dlt * dlt, ones_head) * (1.0 / RW_HEAD)
    yn = dlt * lax.rsqrt(var + RW_LN_EPS) * vec[0:1, :] + vec[1:2, :]
    out_r = (yn + bonus_ref[...]) * gr_ref[...]
    o = of_ref[...] + ob_ref[...]
    ms = _group_sums(o * o, _ones_blocks(LANES)) * (1.0 / HG_DV)
    gh = gh_ref[...]
    out_h = o * lax.rsqrt(ms + NORM_EPS) * vec[2:3, :] * (gh * _sigmoid(gh))
    br = _dot(out_r, wr_ref[...])
    bh = _dot(out_h, wh_ref[...])
    o_ref[...] = (_sigmoid(pgr_ref[...]) * br + _sigmoid(pgh_ref[...]) * bh).astype(o_ref.dtype)


def branch_merge(yf, yb, bonus, gr, of, ob, proj, vec, w_br_r, w_br_h, *, tm=BRANCH_ROWS):
    M = yf.shape[0]
    row = lambda w, c: pl.BlockSpec((tm, w), lambda i: (i, c))
    const = lambda shape: pl.BlockSpec(shape, lambda i: (0,) * len(shape), pipeline_mode=pl.Buffered(1))
    return pl.pallas_call(
        _branch_merge_body,
        grid=(M // tm,),
        in_specs=[row(RW_DIM, 0), row(RW_DIM, 0), row(RW_DIM, 0), row(RW_DIM, 0),
                  row(HG_VDIM, 0), row(HG_VDIM, 0), row(HG_VDIM, COL_HG_G),
                  row(D_MODEL, COL_GATE_R), row(D_MODEL, COL_GATE_H),
                  const((8, RW_DIM)), const((RW_DIM, D_MODEL)), const((HG_VDIM, D_MODEL))],
        out_specs=pl.BlockSpec((tm, D_MODEL), lambda i: (i, 0)),
        out_shape=jax.ShapeDtypeStruct((M, D_MODEL), BF16),
        compiler_params=_params(("parallel",)),
        name="branch_merge",
    )(yf, yb, bonus, gr, of, ob, proj, proj, proj, vec, w_br_r, w_br_h)


def _out_proj_body(m_ref, x_ref, mod_ref, g_ref, w_ref, wr_ref, x1_ref, h2_ref, lg_ref):
    mod = mod_ref[0]
    D = D_MODEL
    x1 = x_ref[...] + mod[:, 2 * D:3 * D] * jnp.dot(m_ref[...], w_ref[...], preferred_element_type=F32)
    x1_ref[...] = x1
    y = x1 * lax.rsqrt(jnp.mean(x1 * x1, axis=-1, keepdims=True) + NORM_EPS) * g_ref[...]
    h2 = (y * (1.0 + mod[:, 4 * D:5 * D]) + mod[:, 3 * D:4 * D]).astype(BF16)
    h2_ref[...] = h2
    lg_ref[...] = jnp.dot(h2, wr_ref[...], preferred_element_type=F32)


def out_proj(merged, x, mod, norm_g, w_out, w_router, *, rows_per_mod, tm=OUT_ROWS):
    M, D = x.shape
    per = rows_per_mod // tm
    return pl.pallas_call(
        _out_proj_body,
        grid=(M // tm,),
        in_specs=[pl.BlockSpec((tm, D), lambda i: (i, 0)),
                  pl.BlockSpec((tm, D), lambda i: (i, 0)),
                  pl.BlockSpec((1, 1, N_MOD * D), lambda i: (i // per, 0, 0)),
                  pl.BlockSpec((1, D), lambda i: (0, 0)),
                  pl.BlockSpec((D, D), lambda i: (0, 0), pipeline_mode=pl.Buffered(1)),
                  pl.BlockSpec((D, LANES), lambda i: (0, 0), pipeline_mode=pl.Buffered(1))],
        out_specs=[pl.BlockSpec((tm, D), lambda i: (i, 0)),
                   pl.BlockSpec((tm, D), lambda i: (i, 0)),
                   pl.BlockSpec((tm, LANES), lambda i: (i, 0))],
        out_shape=[jax.ShapeDtypeStruct((M, D), F32), jax.ShapeDtypeStruct((M, D), BF16),
                   jax.ShapeDtypeStruct((M, LANES), F32)],
        compiler_params=_params(("parallel",)),
        name="out_proj",
    )(merged, x, mod, norm_g, w_out, w_router)


def _experts_body(x_ref, wg_ref, wu_ref, wd_ref, val_ref, o_ref):
    x = x_ref[0]
    gate = jnp.dot(x, wg_ref[0], preferred_element_type=F32)
    up = jnp.dot(x, wu_ref[0], preferred_element_type=F32)
    hid = (gate * _sigmoid(gate) * up).astype(BF16)
    yo = jnp.dot(hid, wd_ref[0], preferred_element_type=F32)
    scale = jnp.concatenate([val_ref[0]] * (D_MODEL // LANES), axis=1)
    o_ref[0] = (yo * scale).astype(o_ref.dtype)


def moe_experts(xe, w_gate, w_up, w_down, vals, *, tm=EXPERT_ROWS):
    E, M, D = xe.shape
    tm = min(tm, M)
    return pl.pallas_call(
        _experts_body,
        grid=(E, M // tm),
        in_specs=[pl.BlockSpec((1, tm, D), lambda e, i: (e, i, 0)),
                  pl.BlockSpec((1, D, EXPERT_FF), lambda e, i: (e, 0, 0)),
                  pl.BlockSpec((1, D, EXPERT_FF), lambda e, i: (e, 0, 0)),
                  pl.BlockSpec((1, EXPERT_FF, D), lambda e, i: (e, 0, 0)),
                  pl.BlockSpec((1, tm, LANES), lambda e, i: (e, i, 0))],
        out_specs=pl.BlockSpec((1, tm, D), lambda e, i: (e, i, 0)),
        out_shape=jax.ShapeDtypeStruct((E, M, D), BF16),
        compiler_params=_params(("parallel", "parallel")),
        name="moe_experts",
    )(xe, w_gate, w_up, w_down, vals)


def _combine_plan(tok_sorted, T, rows, chunk):
    B, R = tok_sorted.shape
    n_tiles, n_chunks = T // rows, R // chunk
    n_pairs = n_tiles + n_chunks
    bounds = jnp.arange(n_tiles + 1, dtype=jnp.int32) * rows
    start = jnp.sum((tok_sorted[:, None, :] < bounds[None, :, None]).astype(jnp.int32), axis=-1)
    first_chunk = jnp.minimum(start[:, :-1] // chunk, n_chunks - 1)
    last_chunk = jnp.clip((start[:, 1:] - 1) // chunk, first_chunk, n_chunks - 1)
    count = last_chunk - first_chunk + 1
    offset = jnp.cumsum(count, axis=-1) - count
    total = offset[:, -1] + count[:, -1]
    pair = jnp.arange(n_pairs, dtype=jnp.int32)
    tile_of = jnp.sum((pair[None, :, None] >= offset[:, None, :]).astype(jnp.int32), axis=-1) - 1
    k_in = pair[None, :] - jnp.take_along_axis(offset, tile_of, axis=1)
    count_t = jnp.take_along_axis(count, tile_of, axis=1)
    chunk_of = jnp.take_along_axis(first_chunk, tile_of, axis=1) + jnp.minimum(k_in, count_t - 1)
    real = pair[None, :] < total[:, None]
    flags = (jnp.logical_and(real, k_in == 0).astype(jnp.int32)
             + 2 * real.astype(jnp.int32)
             + 4 * jnp.logical_and(real, k_in == count_t - 1).astype(jnp.int32))
    return tile_of, chunk_of, flags


def _moe_combine_body(tile_ref, chunk_ref, flag_ref, yo_ref, tok_ref, x1_ref, gt_ref, g_ref, o_ref, acc, *, final_norm):
    b = pl.program_id(0)
    p = pl.program_id(1)
    flags = flag_ref[b, p]
    rows = x1_ref.shape[1]

    @pl.when((flags & 1) != 0)
    def _():
        acc[...] = jnp.zeros(acc.shape, F32)

    @pl.when((flags & 2) != 0)
    def _():
        tok = lax.broadcasted_iota(jnp.int32, (rows, tok_ref.shape[3]), 0) + tile_ref[b, p] * rows
        onehot = jnp.where(tok == tok_ref[0, 0], 1.0, 0.0).astype(BF16)
        acc[...] += jnp.dot(onehot, yo_ref[0], preferred_element_type=F32)

    @pl.when((flags & 4) != 0)
    def _():
        x2 = x1_ref[0] + gt_ref[0] * acc[...]
        if final_norm:
            x2 = x2 * lax.rsqrt(jnp.mean(x2 * x2, axis=-1, keepdims=True) + NORM_EPS) * g_ref[...]
        o_ref[0] = x2


def moe_combine(yo_sorted, tok_sorted, x1, gt2, final_g):
    B, R, D = yo_sorted.shape
    T = x1.shape[1]
    per = B // gt2.shape[0]
    rows, chunk = min(T, COMBINE_ROWS), min(R, COMBINE_CHUNK)
    tile_of, chunk_of, flags = _combine_plan(tok_sorted, T, rows, chunk)
    g = jnp.ones((1, D), F32) if final_g is None else final_g.reshape(1, D)
    grid_spec = pltpu.PrefetchScalarGridSpec(
        num_scalar_prefetch=3,
        grid=(B, tile_of.shape[1]),
        in_specs=[pl.BlockSpec((1, chunk, D), lambda b, p, t, c, f: (b, c[b, p], 0)),
                  pl.BlockSpec((1, 1, 1, chunk), lambda b, p, t, c, f: (b, c[b, p], 0, 0)),
                  pl.BlockSpec((1, rows, D), lambda b, p, t, c, f: (b, t[b, p], 0)),
                  pl.BlockSpec((1, 1, D), lambda b, p, t, c, f: (b // per, 0, 0)),
                  pl.BlockSpec((1, D), lambda b, p, t, c, f: (0, 0))],
        out_specs=pl.BlockSpec((1, rows, D), lambda b, p, t, c, f: (b, t[b, p], 0)),
        scratch_shapes=[pltpu.VMEM((rows, D), F32)])
    return pl.pallas_call(
        functools.partial(_moe_combine_body, final_norm=final_g is not None),
        grid_spec=grid_spec,
        out_shape=jax.ShapeDtypeStruct((B, T, D), F32),
        compiler_params=_params(("parallel", "arbitrary")),
        name="moe_combine",
    )(tile_of, chunk_of, flags, yo_sorted, tok_sorted.reshape(B, R // chunk, 1, chunk), x1, gt2, g)


def _expert_choice_moe(h2, logits, x1, gt2, final_g, p):
    B, T, D = h2.shape
    cap = CAPACITY_FACTOR * T // N_EXPERTS
    aff = jax.nn.softmax(logits[..., :N_EXPERTS], axis=-1)
    vals, idx = lax.top_k(jnp.swapaxes(aff, 1, 2), cap)
    bidx = jnp.arange(B)[:, None, None]
    xg = h2[bidx, idx]
    xe = jnp.swapaxes(xg, 0, 1).reshape(N_EXPERTS, B * cap, D)
    ve = jnp.broadcast_to(jnp.swapaxes(vals, 0, 1).reshape(N_EXPERTS, B * cap, 1), (N_EXPERTS, B * cap, LANES))
    yo = moe_experts(xe, p['moe_w_gate'], p['moe_w_up'], p['moe_w_down'], ve)
    tok = idx.reshape(B, N_EXPERTS * cap)
    order = jnp.argsort(tok, axis=-1)
    tok_sorted = jnp.take_along_axis(tok, order, axis=-1)
    src_row = (order // cap) * (B * cap) + jnp.arange(B, dtype=order.dtype)[:, None] * cap + order % cap
    yo_sorted = yo.reshape(N_EXPERTS * B * cap, D)[src_row]
    return moe_combine(yo_sorted, tok_sorted, x1, gt2, final_g)


def _layer(x, mod, s_rw, s_hg, width, final_g, p):
    B, T, D = x.shape
    rows_per_mod = (B * T) // mod.shape[0]
    xf = x.reshape(B * T, D)
    h = norm_modulate(xf, mod, p['norm1_g'], rows_per_mod=rows_per_mod)
    proj = in_proj(h, p['w_in'], p['mu'], width=width)
    proj3 = proj.reshape(B, T, PROJ_COLS)
    yf, yb, bonus, gr, of, ob, hfin, sfin = recurrent_scans(
        proj3, p['rw_vec'], p['rw_w2pad'], p['rw_a2pad'], p['rw_g2'], p['hg_lb'],
        rwkv_state_to_blockdiag(s_rw), jnp.swapaxes(s_hg, -1, -2))
    flat = lambda u: u.reshape(B * T, u.shape[-1])
    merged = branch_merge(flat(yf), flat(yb), flat(bonus), flat(gr), flat(of), flat(ob), proj,
                          p['br_vec'], p['w_br_rwkv'], p['w_br_hgrn'])
    x1, h2, logits = out_proj(merged, xf, mod, p['norm2_g'], p['w_out'], p['moe_router'],
                              rows_per_mod=rows_per_mod)
    x2 = _expert_choice_moe(h2.reshape(B, T, D), logits.reshape(B, T, LANES), x1.reshape(B, T, D),
                            mod[:, :, 5 * D:6 * D], final_g, p)
    return x2, rwkv_blockdiag_to_state(hfin), jnp.swapaxes(sfin, -1, -2)


def kernel(x_prompt, x_sample, c, state_rwkv, state_hgrn, c_ctx, norm1_g, norm2_g, final_norm_g, ada_w, ada_b, w_in, rw_mu, rw_w0, rw_w2, rw_a0, rw_a2, rw_g2, rw_k_k, rw_k_a, rw_r_k, rw_ln_w, rw_ln_b, hg_lb, hg_norm_g, w_br_rwkv, w_br_hgrn, w_out, moe_router, moe_w_gate, moe_w_up, moe_w_down):
    depth = norm1_g.shape[0]
    D = D_MODEL
    lb_all = jnp.cumsum(jax.nn.softmax(hg_lb.astype(F32), axis=0), axis=0)
    bp = x_prompt.shape[0]
    bs = x_sample.shape[0]
    xp, xs = x_prompt, x_sample
    new_rw, new_hg = [], []
    for l in range(depth):
        wl = w_in[l]
        w_perm = jnp.concatenate([wl[:, RW_COLS + HG_COLS:], wl[:, RW_COLS:RW_COLS + HG_COLS], wl[:, :RW_COLS],
                                  jnp.zeros((D, PROJ_COLS - IN_COLS), F32)], axis=1).astype(BF16)
        mu_perm = jnp.concatenate([jnp.zeros((2, GATE_COLS + HG_COLS), F32), rw_mu[l],
                                   jnp.zeros((2, PROJ_COLS - IN_COLS), F32)], axis=1)
        pr = dict(rw_w0=rw_w0[l], rw_w2=rw_w2[l], rw_a0=rw_a0[l], rw_a2=rw_a2[l], rw_g2=rw_g2[l],
                  rw_k_k=rw_k_k[l], rw_k_a=rw_k_a[l], rw_r_k=rw_r_k[l])
        rw_vec, rw_w2pad, rw_a2pad, rw_g2b = rwkv_pack_params(pr)
        zrow = jnp.zeros((RW_DIM,), F32)
        br_vec = jnp.stack([rw_ln_w[l], rw_ln_b[l], hg_norm_g[l], zrow, zrow, zrow, zrow, zrow])
        p = dict(norm1_g=norm1_g[l][None, :], norm2_g=norm2_g[l][None, :], w_in=w_perm, mu=mu_perm,
                 rw_vec=rw_vec, rw_w2pad=rw_w2pad, rw_a2pad=rw_a2pad, rw_g2=rw_g2b,
                 hg_lb=lb_all[l], br_vec=br_vec,
                 w_br_rwkv=w_br_rwkv[l].astype(BF16), w_br_hgrn=w_br_hgrn[l].astype(BF16),
                 w_out=w_out[l].astype(BF16),
                 moe_router=jnp.pad(moe_router[l], ((0, 0), (0, LANES - N_EXPERTS))).astype(BF16),
                 moe_w_gate=cast_experts_bf16(moe_w_gate[l]), moe_w_up=cast_experts_bf16(moe_w_up[l]),
                 moe_w_down=cast_experts_bf16(moe_w_down[l]))
        cond = jnp.concatenate([c_ctx[None, :], c], axis=0)
        rows = -(-(1 + bs) // BF16_SUBLANES) * BF16_SUBLANES
        cond = jnp.pad(jax.nn.silu(cond), ((0, rows - 1 - bs), (0, 0)))
        mod = matmul(cond, ada_w[l], tm=rows, tn=ADA_COLS)[:1 + bs] + ada_b[l]
        mod = mod.reshape(1 + bs, 1, N_MOD * D)
        zero_rw = jnp.zeros((bp, 2, RW_HEADS, RW_HEAD, RW_HEAD), F32)
        zero_hg = jnp.zeros((bp, 2, HG_HEADS, HG_DK, HG_DV), F32)
        final_g = final_norm_g if l == depth - 1 else None
        xp, s_rw, s_hg = _layer(xp, mod[:1], zero_rw, zero_hg, xp.shape[1], final_g, p)
        new_rw.append(s_rw)
        new_hg.append(s_hg)
        xs, _, _ = _layer(xs, mod[1:], state_rwkv[:, l].astype(F32), state_hgrn[:, l].astype(F32), GRID_W,
                          final_g, p)
    return (xp, xs, jnp.stack(new_rw, axis=1), jnp.stack(new_hg, axis=1))
```

```python
import functools

import jax
import jax.numpy as jnp
from jax import lax
from jax.experimental import pallas as pl
from jax.experimental.pallas import tpu as pltpu

F32 = jnp.float32
BF16 = jnp.bfloat16

D_MODEL = 2048
GRID_W = 64
RW_HEADS = 16
RW_HEAD = 64
RW_DIM = RW_HEADS * RW_HEAD
DECAY_LORA = 64
AAA_LORA = 64
GATE_LORA = 128
RW_LORA_COLS = 2 * DECAY_LORA + 2 * AAA_LORA + GATE_LORA
RW_LN_EPS = 64e-5
HG_HEADS = 8
HG_DK = 128
HG_DV = 128
HG_KDIM = HG_HEADS * HG_DK
HG_VDIM = HG_HEADS * HG_DV
N_EXPERTS = 16
EXPERT_FF = 1024
CAPACITY_FACTOR = 2
NORM_EPS = 1e-6
N_MOD = 6
RW_COLS = 3 * RW_DIM + RW_LORA_COLS
HG_COLS = 3 * HG_KDIM + 2 * HG_VDIM
GATE_COLS = 2 * D_MODEL
IN_COLS = RW_COLS + HG_COLS + GATE_COLS

LANES = 128
BF16_SUBLANES = 16
V7X_VMEM_LIMIT_BYTES = 52 * 1024 * 1024

NORM_ROWS = 512
PROJ_ROWS = 1024
BRANCH_ROWS = 256
OUT_ROWS = 512
EXPERT_ROWS = 512
ADA_COLS = 1024

V7X_MXU_WIDTH = 256
PROJ_TN = 5 * V7X_MXU_WIDTH
PROJ_COLS = -(-IN_COLS // PROJ_TN) * PROJ_TN
PROJ_PLAIN_TILES = (GATE_COLS + HG_COLS) // PROJ_TN
COL_GATE_R, COL_GATE_H = 0, 1
COL_HG_Q, COL_HG_FF, COL_HG_FB, COL_HG_I, COL_HG_G = (GATE_COLS // HG_KDIM + j for j in range(5))
COL_RW_R, COL_RW_K, COL_RW_V = ((GATE_COLS + HG_COLS) // RW_DIM + j for j in range(3))
COL_RW_LORA = (GATE_COLS + HG_COLS + 3 * RW_DIM) // RW_LORA_COLS

RW_CHUNK = 64
RW_PAIRS = RW_DIM // LANES
HG_L = 128
HG_FIRST_ROUND = 4
COMBINE_ROWS = 512
COMBINE_CHUNK = 512
DECAY_SCALE = 0.6065306597126334
LOG2_E = 1.4426950408889634


def _params(sem):
    return pltpu.CompilerParams(dimension_semantics=sem, vmem_limit_bytes=V7X_VMEM_LIMIT_BYTES)


def _dot(a, b):
    return jnp.dot(a.astype(BF16), b.astype(BF16), preferred_element_type=F32)


def _dot_nt(a, b):
    return lax.dot_general(a.astype(BF16), b.astype(BF16), (((1,), (1,)), ((), ())),
                           preferred_element_type=F32)


def _split_dot(x, w_bf16):
    hi = x.astype(BF16)
    lo = (x - hi.astype(F32)).astype(BF16)
    return (jnp.dot(hi, w_bf16, preferred_element_type=F32)
            + jnp.dot(lo, w_bf16, preferred_element_type=F32))


def _group_sums(x, ones_bd):
    parts = [_split_dot(x[:, p * LANES:(p + 1) * LANES], ones_bd) for p in range(x.shape[1] // LANES)]
    return jnp.concatenate(parts, axis=1)


def _ones_blocks(group):
    ri = lax.broadcasted_iota(jnp.int32, (LANES, LANES), 0)
    ci = lax.broadcasted_iota(jnp.int32, (LANES, LANES), 1)
    return jnp.where((ri // group) == (ci // group), 1.0, 0.0).astype(BF16)


def _sigmoid(x):
    return 1.0 / (1.0 + jnp.exp(-x))


def _cumsum_rows(x, reverse):
    L = x.shape[0]
    row = lax.broadcasted_iota(jnp.int32, x.shape, 0)
    s = 1
    while s < L:
        if reverse:
            x = x + jnp.where(row < L - s, pltpu.roll(x, L - s, 0), 0.0)
        else:
            x = x + jnp.where(row >= s, pltpu.roll(x, s, 0), 0.0)
        s *= 2
    return x


def _interleave(gens, first_round=None):
    results = [None] * len(gens)
    first_round = first_round or [0] * len(gens)
    active = list(enumerate(gens))
    rnd = 0
    while active:
        still = []
        for idx, g in active:
            if rnd < first_round[idx]:
                still.append((idx, g))
                continue
            try:
                next(g)
                still.append((idx, g))
            except StopIteration as e:
                results[idx] = e.value
        active = still
        rnd += 1
    return results


def _mm_body(x_ref, w_ref, o_ref):
    o_ref[...] = _dot(x_ref[...], w_ref[...]).astype(o_ref.dtype)


def matmul(x, w, *, tm, tn, out_dtype=F32):
    M, K = x.shape
    _, N = w.shape
    return pl.pallas_call(
        _mm_body,
        grid=(M // tm, N // tn),
        in_specs=[pl.BlockSpec((tm, K), lambda i, j: (i, 0)),
                  pl.BlockSpec((K, tn), lambda i, j: (0, j))],
        out_specs=pl.BlockSpec((tm, tn), lambda i, j: (i, j)),
        out_shape=jax.ShapeDtypeStruct((M, N), out_dtype),
        compiler_params=_params(("parallel", "parallel")),
        name="matmul",
    )(x, w)


def _cast_body(w_ref, o_ref):
    o_ref[...] = w_ref[...].astype(o_ref.dtype)


def cast_experts_bf16(w):
    E, K, N = w.shape
    return pl.pallas_call(
        _cast_body,
        grid=(E,),
        in_specs=[pl.BlockSpec((1, K, N), lambda e: (e, 0, 0))],
        out_specs=pl.BlockSpec((1, K, N), lambda e: (e, 0, 0)),
        out_shape=jax.ShapeDtypeStruct((E, K, N), BF16),
        compiler_params=_params(("parallel",)),
        name="cast_experts_bf16",
    )(w)


def _norm_mod_body(x_ref, mod_ref, g_ref, o_ref):
    x = x_ref[...]
    y = x * lax.rsqrt(jnp.mean(x * x, axis=-1, keepdims=True) + NORM_EPS) * g_ref[...]
    m = mod_ref[0]
    o_ref[...] = (y * (1.0 + m[:, D_MODEL:2 * D_MODEL]) + m[:, 0:D_MODEL]).astype(o_ref.dtype)


def norm_modulate(x, mod, norm_g, *, rows_per_mod, tm=NORM_ROWS):
    M, D = x.shape
    assert M % tm == 0 and rows_per_mod % tm == 0
    per = rows_per_mod // tm
    return pl.pallas_call(
        _norm_mod_body,
        grid=(M // tm,),
        in_specs=[pl.BlockSpec((tm, D), lambda i: (i, 0)),
                  pl.BlockSpec((1, 1, N_MOD * D), lambda i: (i // per, 0, 0)),
                  pl.BlockSpec((1, D), lambda i: (0, 0))],
        out_specs=pl.BlockSpec((tm, D), lambda i: (i, 0)),
        out_shape=jax.ShapeDtypeStruct((M, D), BF16),
        compiler_params=_params(("parallel",)),
        name="norm_modulate",
    )(x, mod, norm_g)


def _in_proj_body(h_ref, w_ref, mu_ref, o_ref, *, width):
    j = pl.program_id(1)

    @pl.when(j < PROJ_PLAIN_TILES)
    def _():
        o_ref[...] = jnp.dot(h_ref[...], w_ref[...], preferred_element_type=F32)

    @pl.when(j >= PROJ_PLAIN_TILES)
    def _():
        acc = jnp.dot(h_ref[...], w_ref[...], preferred_element_type=F32)
        tm = acc.shape[0]
        pos = lax.broadcasted_iota(jnp.int32, acc.shape, 0) & (width - 1)
        prev = jnp.where(pos == 0, 0.0, pltpu.roll(acc, 1, 0))
        nxt = jnp.where(pos == width - 1, 0.0, pltpu.roll(acc, tm - 1, 0))
        mu = mu_ref[...]
        o_ref[...] = acc + mu[0:1, :] * (prev - acc) + mu[1:2, :] * (nxt - acc)


def in_proj(h, w_perm, mu_perm, *, width, tm=PROJ_ROWS):
    M, D = h.shape
    assert M % tm == 0 and tm % width == 0
    return pl.pallas_call(
        functools.partial(_in_proj_body, width=width),
        grid=(M // tm, PROJ_COLS // PROJ_TN),
        in_specs=[pl.BlockSpec((tm, D), lambda i, j: (i, 0)),
                  pl.BlockSpec((D, PROJ_TN), lambda i, j: (0, j)),
                  pl.BlockSpec((2, PROJ_TN), lambda i, j: (0, j))],
        out_specs=pl.BlockSpec((tm, PROJ_TN), lambda i, j: (i, j)),
        out_shape=jax.ShapeDtypeStruct((M, PROJ_COLS), F32),
        compiler_params=_params(("parallel", "parallel")),
        name="in_proj",
    )(h, w_perm, mu_perm)


def _block_diag(x, lane_lo):
    return jnp.concatenate([jnp.where(lane_lo, x, 0.0), jnp.where(lane_lo, 0.0, x)], axis=0)


def _rwkv_masks(reverse):
    L = RW_CHUNK
    n2 = 2 * L
    ri = lax.broadcasted_iota(jnp.int32, (n2, n2), 0)
    ci = lax.broadcasted_iota(jnp.int32, (n2, n2), 1)
    rt = ri & (L - 1)
    ct = ci & (L - 1)
    blocks = []
    rb, cb = rt, ct
    s = 1
    while s < L:
        rb = rb >> 1
        cb = cb >> 1
        blocks.append(rb == cb)
        s *= 2
    couplings = [jnp.logical_and(blocks[j + 1], jnp.logical_not(blocks[j])) for j in range(len(blocks) - 1)]
    return dict(
        lane_lo=lax.broadcasted_iota(jnp.int32, (L, LANES), 1) < RW_HEAD,
        eye=ri == ci,
        strict=(rt < ct) if reverse else (rt > ct),
        incl=(rt <= ct) if reverse else (rt >= ct),
        pair_block=blocks[0],
        couplings=couplings)


def _rwkv_pair_chunk(q, lw, kd, v, kk, a, reverse, m):
    L = q.shape[0]
    n2 = 2 * L
    lane_lo = m['lane_lo']
    c = _cumsum_rows(lw, reverse)
    c_last = c[0:1, :] if reverse else c[L - 1:L, :]
    g_inc = jnp.exp(c)
    g_exc = jnp.exp(c - lw)
    g_inv = jnp.exp(-c)
    g_rem = jnp.exp(c_last - c)
    beta = kk * a
    ab = _block_diag(-kk * g_exc, lane_lo)
    qb = _block_diag(q * g_inc, lane_lo)
    bt = _block_diag(beta * g_inv, lane_lo)
    kt = _block_diag(kd * g_inv, lane_lo)
    vb = _block_diag(v, lane_lo)
    bh = _block_diag(beta * g_rem, lane_lo)
    kh = _block_diag(kd * g_rem, lane_lo)

    aa = _dot_nt(jnp.concatenate([ab, qb], axis=0), jnp.concatenate([bt, kt], axis=0))
    yield
    a_ab = jnp.where(m['strict'], aa[:n2, :n2], 0.0)
    a_ak = jnp.where(m['strict'], aa[:n2, n2:], 0.0)
    a_qb = jnp.where(m['incl'], aa[n2:, :n2], 0.0)
    a_qk = jnp.where(m['incl'], aa[n2:, n2:], 0.0)

    P = jnp.where(m['eye'], 1.0, 0.0) + jnp.where(m['pair_block'], a_ab, 0.0)
    av = _dot(a_ak, vb)
    for coupling_mask in m['couplings']:
        pc = _dot(P, jnp.where(coupling_mask, a_ab, 0.0))
        yield
        P = P + _dot(pc, P)
        yield

    wu_u0 = _dot(P, jnp.concatenate([ab, av], axis=1))
    yield
    rhs = jnp.concatenate([wu_u0, jnp.concatenate([jnp.zeros_like(vb), vb], axis=1)], axis=0)
    qy = _dot(jnp.concatenate([a_qb, a_qk], axis=1), rhs)
    st = _dot(jnp.concatenate([bh, kh], axis=0).T, rhs)
    yield
    q_eff = qb + qy[:, :LANES]
    g_last = jnp.exp(jnp.broadcast_to(c_last, (n2, LANES)))
    m_mat = jnp.where(m['eye'], g_last, 0.0) + st[:, :LANES]
    return jnp.concatenate([q_eff, m_mat], axis=0), jnp.concatenate([qy[:, LANES:], st[:, LANES:]], axis=0)


def _rwkv_apply_state(lhs, add, H):
    L = RW_CHUNK
    out = _dot(lhs, H) + add
    return out[:L] + out[L:2 * L], out[2 * L:]


def _rwkv_prep(r, k, v, lora, vec, w2pad, a2pad, ones_bd, d, want_other_kd):
    lora_in_w = lora[:, 0:LANES]
    lora_in_a = lora[:, LANES:2 * LANES]
    kk = k * vec[4:5, :]
    ss = _group_sums(kk * kk, ones_bd)
    kk = kk * lax.rsqrt(jnp.maximum(ss, 1e-24))
    u = vec[d:d + 1, :] + jnp.dot(jnp.tanh(lora_in_w).astype(BF16), w2pad[d], preferred_element_type=F32)
    lw = -DECAY_SCALE * _sigmoid(u)
    a = _sigmoid(vec[2 + d:3 + d, :] + jnp.dot(lora_in_a.astype(BF16), a2pad[d], preferred_element_type=F32))
    kd = k * (1.0 + (a - 1.0) * vec[5:6, :])
    out = dict(r=r, k=k, v=v, kk=kk, lw=lw, a=a, kd=kd)
    if want_other_kd:
        o = 1 - d
        a_o = _sigmoid(vec[2 + o:3 + o, :] + jnp.dot(lora_in_a.astype(BF16), a2pad[o], preferred_element_type=F32))
        out['kd_other'] = k * (1.0 + (a_o - 1.0) * vec[5:6, :])
    return out


def _scan_body(rf_ref, kf_ref, vf_ref, lf_ref, rb_ref, kb_ref, vb_ref, lb_ref,
               qf_ref, ff_ref, if_ref, qb_ref, fb_ref, ib_ref,
               vec_ref, w2_ref, a2_ref, g2_ref, hglb_ref, h0_ref, s0_ref,
               yf_ref, yb_ref, bonus_ref, g_ref, of_ref, ob_ref, hfin_ref, sfin_ref, h_scr, s_scr):
    i = pl.program_id(1)
    n = pl.num_programs(1)

    @pl.when(i == 0)
    def _():
        h_scr[...] = h0_ref[0]
        s_scr[...] = s0_ref[0]

    vec = vec_ref[...]
    ones_bd = _ones_blocks(RW_HEAD)
    fw = _rwkv_prep(rf_ref[0], kf_ref[0], vf_ref[0], lf_ref[0], vec, w2_ref, a2_ref, ones_bd, 0, True)
    bw = _rwkv_prep(rb_ref[0], kb_ref[0], vb_ref[0], lb_ref[0], vec, w2_ref, a2_ref, ones_bd, 1, False)

    rk = fw['r'] * vec[6:7, :]
    bonus_ref[0] = _group_sums(rk * (fw['kd'] + fw['kd_other']), ones_bd) * fw['v']
    gd = lf_ref[0][:, 2 * LANES:3 * LANES]
    g_ref[0] = jnp.dot(_sigmoid(gd).astype(BF16), g2_ref[...], preferred_element_type=F32)

    chunk_order = ((0, 1), (1, 0))
    rw_masks = (_rwkv_masks(False), _rwkv_masks(True))
    hg_masks = (_gla_masks(False), _gla_masks(True))
    chains = []
    for d, t in enumerate((fw, bw)):
        for half in chunk_order[d]:
            rows = slice(half * RW_CHUNK, (half + 1) * RW_CHUNK)
            for p in range(RW_PAIRS):
                sl = slice(p * LANES, (p + 1) * LANES)
                chains.append(_rwkv_pair_chunk(t['r'][rows, sl], t['lw'][rows, sl], t['kd'][rows, sl],
                                               t['v'][rows, sl], t['kk'][rows, sl], t['a'][rows, sl], d == 1,
                                               rw_masks[d]))
    n_rw = len(chains)
    for d, (q_ref, f_ref, v_ref) in enumerate(((qf_ref, ff_ref, if_ref), (qb_ref, fb_ref, ib_ref))):
        for h in range(HG_HEADS):
            sl = slice(h * LANES, (h + 1) * LANES)
            chains.append(_gla_head_chunk(q_ref[0, :, sl], f_ref[0, :, sl], v_ref[0, :, sl],
                                          hglb_ref[d:d + 1, sl], s_scr[d, h], d == 1, hg_masks[d]))
    outs = _interleave(chains, [0] * n_rw + [HG_FIRST_ROUND] * (len(chains) - n_rw))

    states = [[h_scr[d, p] for p in range(RW_PAIRS)] for d in range(2)]
    for step in range(2):
        for d, y_ref in enumerate((yf_ref, yb_ref)):
            half = chunk_order[d][step]
            for p in range(RW_PAIRS):
                lhs, add = outs[(d * 2 + step) * RW_PAIRS + p]
                y, states[d][p] = _rwkv_apply_state(lhs, add, states[d][p])
                y_ref[0, half * RW_CHUNK:(half + 1) * RW_CHUNK, p * LANES:(p + 1) * LANES] = y
    for d in range(2):
        for p in range(RW_PAIRS):
            h_scr[d, p] = states[d][p]

    for d, o_ref in enumerate((of_ref, ob_ref)):
        for h in range(HG_HEADS):
            o, st_new = outs[n_rw + d * HG_HEADS + h]
            o_ref[0, :, h * LANES:(h + 1) * LANES] = o
            s_scr[d, h] = st_new

    @pl.when(i == n - 1)
    def _():
        hfin_ref[0] = h_scr[...]
        sfin_ref[0] = s_scr[...]


def recurrent_scans(proj, vec, w2pad, a2pad, g2, hg_lb, h0, s0t):
    B, T, _ = proj.shape
    assert T % HG_L == 0 and HG_L == 2 * RW_CHUNK
    n = T // HG_L
    fwd = lambda c: (lambda b, i: (b, i, c))
    bwd = lambda c: (lambda b, i: (b, n - 1 - i, c))
    wide = lambda fn: pl.BlockSpec((1, HG_L, RW_DIM), fn)
    lora = lambda fn: pl.BlockSpec((1, HG_L, RW_LORA_COLS), fn)
    const = lambda shape: pl.BlockSpec(shape, lambda b, i: (0,) * len(shape))
    rw_st = pl.BlockSpec((1, 2, RW_PAIRS, LANES, LANES), lambda b, i: (b, 0, 0, 0, 0))
    hg_st = pl.BlockSpec((1, 2, HG_HEADS, HG_DV, HG_DK), lambda b, i: (b, 0, 0, 0, 0))
    out_sd = jax.ShapeDtypeStruct((B, T, RW_DIM), F32)
    return pl.pallas_call(
        _scan_body,
        grid=(B, n),
        in_specs=[wide(fwd(COL_RW_R)), wide(fwd(COL_RW_K)), wide(fwd(COL_RW_V)), lora(fwd(COL_RW_LORA)),
                  wide(bwd(COL_RW_R)), wide(bwd(COL_RW_K)), wide(bwd(COL_RW_V)), lora(bwd(COL_RW_LORA)),
                  wide(fwd(COL_HG_Q)), wide(fwd(COL_HG_FF)), wide(fwd(COL_HG_I)),
                  wide(bwd(COL_HG_Q)), wide(bwd(COL_HG_FB)), wide(bwd(COL_HG_I)),
                  const((8, RW_DIM)), const((2, LANES, RW_DIM)), const((2, LANES, RW_DIM)),
                  const((LANES, RW_DIM)), const((2, HG_KDIM)), rw_st, hg_st],
        out_specs=[wide(fwd(0)), wide(bwd(0)), wide(fwd(0)), wide(fwd(0)), wide(fwd(0)), wide(bwd(0)),
                   rw_st, hg_st],
        out_shape=[out_sd] * 6 + [jax.ShapeDtypeStruct((B, 2, RW_PAIRS, LANES, LANES), F32),
                                  jax.ShapeDtypeStruct((B, 2, HG_HEADS, HG_DV, HG_DK), F32)],
        scratch_shapes=[pltpu.VMEM((2, RW_PAIRS, LANES, LANES), F32),
                        pltpu.VMEM((2, HG_HEADS, HG_DV, HG_DK), F32)],
        compiler_params=_params(("parallel", "arbitrary")),
        name="recurrent_scans",
    )(*([proj] * 14), vec, w2pad, a2pad, g2, hg_lb, h0, s0t)


def rwkv_pack_params(p):
    z = jnp.zeros((RW_HEAD, RW_DIM), F32)
    w2pad = jnp.stack([jnp.concatenate([p['rw_w2'][0], z], 0), jnp.concatenate([z, p['rw_w2'][1]], 0)]).astype(BF16)
    a2pad = jnp.stack([jnp.concatenate([p['rw_a2'][0], z], 0), jnp.concatenate([z, p['rw_a2'][1]], 0)]).astype(BF16)
    vec = jnp.stack([p['rw_w0'][0], p['rw_w0'][1], p['rw_a0'][0], p['rw_a0'][1], p['rw_k_k'], p['rw_k_a'],
                     p['rw_r_k'].reshape(RW_DIM), jnp.zeros((RW_DIM,), F32)])
    return vec, w2pad, a2pad, p['rw_g2'].astype(BF16)


def rwkv_state_to_blockdiag(s):
    B = s.shape[0]
    h = jnp.swapaxes(s, -1, -2).reshape(B, 2, RW_PAIRS, 2, RW_HEAD, RW_HEAD)
    z = jnp.zeros_like(h[:, :, :, 0])
    top = jnp.concatenate([h[:, :, :, 0], z], axis=-1)
    bot = jnp.concatenate([z, h[:, :, :, 1]], axis=-1)
    return jnp.concatenate([top, bot], axis=-2)


def rwkv_blockdiag_to_state(hb):
    B = hb.shape[0]
    h0 = hb[:, :, :, :RW_HEAD, :RW_HEAD]
    h1 = hb[:, :, :, RW_HEAD:, RW_HEAD:]
    h = jnp.stack([h0, h1], axis=3).reshape(B, 2, RW_HEADS, RW_HEAD, RW_HEAD)
    return jnp.swapaxes(h, -1, -2)


def _gla_masks(reverse):
    L = HG_L
    row = lax.broadcasted_iota(jnp.int32, (L, LANES), 0)
    ri = lax.broadcasted_iota(jnp.int32, (L, L), 0)
    ci = lax.broadcasted_iota(jnp.int32, (L, L), 1)
    levels = []
    half = 1
    while half < L:
        size = 2 * half
        upper = (row & half) != 0
        r_up = (ri & half) != 0
        c_up = (ci & half) != 0
        if reverse:
            is_q = jnp.logical_not(upper)
            pair = jnp.logical_and(jnp.logical_not(r_up), c_up)
        else:
            is_q = upper
            pair = jnp.logical_and(r_up, jnp.logical_not(c_up))
        if size < L:
            pair = jnp.logical_and(pair, (ri // size) == (ci // size))
        levels.append(dict(half=half, upper=upper, is_q=is_q, pair=pair,
                           scale=jnp.where(is_q, LOG2_E, -LOG2_E)))
        half = size
    return dict(eye=ri == ci, levels=levels)


def _gla_head_chunk(q_raw, fl, v, lb, St, reverse, m):
    L = q_raw.shape[0]
    q = q_raw * _sigmoid(q_raw)
    f = lb + (1.0 - lb) * _sigmoid(fl)
    k = 1.0 - f
    g = jnp.log(f)
    b = _cumsum_rows(g, reverse)
    b_last = b[0:1, :] if reverse else b[L - 1:L, :]

    att = jnp.where(m['eye'], _dot_nt(q, k), 0.0)
    fk = b
    for lv in m['levels']:
        half, upper, is_q = lv['half'], lv['upper'], lv['is_q']
        if reverse:
            bref = jnp.where(upper, fk, pltpu.roll(fk, L - half, 0))
        else:
            bref = jnp.where(upper, pltpu.roll(fk, half, 0), fk)
        e = jnp.exp2((b - bref) * lv['scale'])
        x = (jnp.where(is_q, q, k) * e).astype(BF16)
        prod = lax.dot_general(x, x, (((1,), (1,)), ((), ())), preferred_element_type=F32)
        yield
        att = jnp.where(lv['pair'], prod, att)
        if reverse:
            fk = jnp.where(upper, pltpu.roll(fk, half, 0), fk)
        else:
            fk = jnp.where(upper, fk, pltpu.roll(fk, L - half, 0))
    o = _dot(att, v) + _dot_nt(q * jnp.exp(b), St)
    kst = k * jnp.exp(b_last - b)
    st_new = St * jnp.exp(b_last) + _dot(v.T, kst)
    yield
    return o, st_new


def _branch_merge_body(yf_ref, yb_ref, bonus_ref, gr_ref, of_ref, ob_ref, gh_ref, pgr_ref, pgh_ref,
                       vec_ref, wr_ref, wh_ref, o_ref):
    vec = vec_ref[...]
    y = yf_ref[...] + yb_ref[...]
    ones_head = _ones_blocks(RW_HEAD)
    mean = _group_sums(y, ones_head) * (1.0 / RW_HEAD)
    dlt = y - mean
    var = _group_sums(dlt * dlt, ones_head) * (1.0 / RW_HEAD)
    yn = dlt * lax.rsqrt(var + RW_LN_EPS) * vec[0:1, :] + vec[1:2, :]
    out_r = (yn + bonus_ref[...]) * gr_ref[...]
    o = of_ref[...] + ob_ref[...]
    ms = _group_sums(o * o, _ones_blocks(LANES)) * (1.0 / HG_DV)
    gh = gh_ref[...]
    out_h = o * lax.rsqrt(ms + NORM_EPS) * vec[2:3, :] * (gh * _sigmoid(gh))
    br = _dot(out_r, wr_ref[...])
    bh = _dot(out_h, wh_ref[...])
    o_ref[...] = (_sigmoid(pgr_ref[...]) * br + _sigmoid(pgh_ref[...]) * bh).astype(o_ref.dtype)


def branch_merge(yf, yb, bonus, gr, of, ob, proj, vec, w_br_r, w_br_h, *, tm=BRANCH_ROWS):
    M = yf.shape[0]
    assert M % tm == 0
    row = lambda w, c: pl.BlockSpec((tm, w), lambda i: (i, c))
    const = lambda shape: pl.BlockSpec(shape, lambda i: (0,) * len(shape), pipeline_mode=pl.Buffered(1))
    return pl.pallas_call(
        _branch_merge_body,
        grid=(M // tm,),
        in_specs=[row(RW_DIM, 0), row(RW_DIM, 0), row(RW_DIM, 0), row(RW_DIM, 0),
                  row(HG_VDIM, 0), row(HG_VDIM, 0), row(HG_VDIM, COL_HG_G),
                  row(D_MODEL, COL_GATE_R), row(D_MODEL, COL_GATE_H),
                  const((8, RW_DIM)), const((RW_DIM, D_MODEL)), const((HG_VDIM, D_MODEL))],
        out_specs=pl.BlockSpec((tm, D_MODEL), lambda i: (i, 0)),
        out_shape=jax.ShapeDtypeStruct((M, D_MODEL), BF16),
        compiler_params=_params(("parallel",)),
        name="branch_merge",
    )(yf, yb, bonus, gr, of, ob, proj, proj, proj, vec, w_br_r, w_br_h)


def _out_proj_body(m_ref, x_ref, mod_ref, g_ref, w_ref, wr_ref, x1_ref, h2_ref, lg_ref):
    mod = mod_ref[0]
    D = D_MODEL
    x1 = x_ref[...] + mod[:, 2 * D:3 * D] * jnp.dot(m_ref[...], w_ref[...], preferred_element_type=F32)
    x1_ref[...] = x1
    y = x1 * lax.rsqrt(jnp.mean(x1 * x1, axis=-1, keepdims=True) + NORM_EPS) * g_ref[...]
    h2 = (y * (1.0 + mod[:, 4 * D:5 * D]) + mod[:, 3 * D:4 * D]).astype(BF16)
    h2_ref[...] = h2
    lg_ref[...] = jnp.dot(h2, wr_ref[...], preferred_element_type=F32)


def out_proj(merged, x, mod, norm_g, w_out, w_router, *, rows_per_mod, tm=OUT_ROWS):
    M, D = x.shape
    assert M % tm == 0 and rows_per_mod % tm == 0
    per = rows_per_mod // tm
    return pl.pallas_call(
        _out_proj_body,
        grid=(M // tm,),
        in_specs=[pl.BlockSpec((tm, D), lambda i: (i, 0)),
                  pl.BlockSpec((tm, D), lambda i: (i, 0)),
                  pl.BlockSpec((1, 1, N_MOD * D), lambda i: (i // per, 0, 0)),
                  pl.BlockSpec((1, D), lambda i: (0, 0)),
                  pl.BlockSpec((D, D), lambda i: (0, 0), pipeline_mode=pl.Buffered(1)),
                  pl.BlockSpec((D, LANES), lambda i: (0, 0), pipeline_mode=pl.Buffered(1))],
        out_specs=[pl.BlockSpec((tm, D), lambda i: (i, 0)),
                   pl.BlockSpec((tm, D), lambda i: (i, 0)),
                   pl.BlockSpec((tm, LANES), lambda i: (i, 0))],
        out_shape=[jax.ShapeDtypeStruct((M, D), F32), jax.ShapeDtypeStruct((M, D), BF16),
                   jax.ShapeDtypeStruct((M, LANES), F32)],
        compiler_params=_params(("parallel",)),
        name="out_proj",
    )(merged, x, mod, norm_g, w_out, w_router)


def _experts_body(x_ref, wg_ref, wu_ref, wd_ref, val_ref, o_ref):
    x = x_ref[0]
    gate = jnp.dot(x, wg_ref[0], preferred_element_type=F32)
    up = jnp.dot(x, wu_ref[0], preferred_element_type=F32)
    hid = (gate * _sigmoid(gate) * up).astype(BF16)
    yo = jnp.dot(hid, wd_ref[0], preferred_element_type=F32)
    scale = jnp.concatenate([val_ref[0]] * (D_MODEL // LANES), axis=1)
    o_ref[0] = (yo * scale).astype(o_ref.dtype)


def moe_experts(xe, w_gate, w_up, w_down, vals, *, tm=EXPERT_ROWS):
    E, M, D = xe.shape
    tm = min(tm, M)
    assert M % tm == 0
    return pl.pallas_call(
        _experts_body,
        grid=(E, M // tm),
        in_specs=[pl.BlockSpec((1, tm, D), lambda e, i: (e, i, 0)),
                  pl.BlockSpec((1, D, EXPERT_FF), lambda e, i: (e, 0, 0)),
                  pl.BlockSpec((1, D, EXPERT_FF), lambda e, i: (e, 0, 0)),
                  pl.BlockSpec((1, EXPERT_FF, D), lambda e, i: (e, 0, 0)),
                  pl.BlockSpec((1, tm, LANES), lambda e, i: (e, i, 0))],
        out_specs=pl.BlockSpec((1, tm, D), lambda e, i: (e, i, 0)),
        out_shape=jax.ShapeDtypeStruct((E, M, D), BF16),
        compiler_params=_params(("parallel", "parallel")),
        name="moe_experts",
    )(xe, w_gate, w_up, w_down, vals)


def _combine_plan(tok_sorted, T, rows, chunk):
    B, R = tok_sorted.shape
    n_tiles, n_chunks = T // rows, R // chunk
    n_pairs = n_tiles + n_chunks
    bounds = jnp.arange(n_tiles + 1, dtype=jnp.int32) * rows
    start = jnp.sum((tok_sorted[:, None, :] < bounds[None, :, None]).astype(jnp.int32), axis=-1)
    first_chunk = jnp.minimum(start[:, :-1] // chunk, n_chunks - 1)
    last_chunk = jnp.clip((start[:, 1:] - 1) // chunk, first_chunk, n_chunks - 1)
    count = last_chunk - first_chunk + 1
    offset = jnp.cumsum(count, axis=-1) - count
    total = offset[:, -1] + count[:, -1]
    pair = jnp.arange(n_pairs, dtype=jnp.int32)
    tile_of = jnp.sum((pair[None, :, None] >= offset[:, None, :]).astype(jnp.int32), axis=-1) - 1
    k_in = pair[None, :] - jnp.take_along_axis(offset, tile_of, axis=1)
    count_t = jnp.take_along_axis(count, tile_of, axis=1)
    chunk_of = jnp.take_along_axis(first_chunk, tile_of, axis=1) + jnp.minimum(k_in, count_t - 1)
    real = pair[None, :] < total[:, None]
    flags = (jnp.logical_and(real, k_in == 0).astype(jnp.int32)
             + 2 * real.astype(jnp.int32)
             + 4 * jnp.logical_and(real, k_in == count_t - 1).astype(jnp.int32))
    return tile_of, chunk_of, flags


def _moe_combine_body(tile_ref, chunk_ref, flag_ref, yo_ref, tok_ref, x1_ref, gt_ref, g_ref, o_ref, acc, *, final_norm):
    b = pl.program_id(0)
    p = pl.program_id(1)
    flags = flag_ref[b, p]
    rows = x1_ref.shape[1]

    @pl.when((flags & 1) != 0)
    def _():
        acc[...] = jnp.zeros(acc.shape, F32)

    @pl.when((flags & 2) != 0)
    def _():
        tok = lax.broadcasted_iota(jnp.int32, (rows, tok_ref.shape[3]), 0) + tile_ref[b, p] * rows
        onehot = jnp.where(tok == tok_ref[0, 0], 1.0, 0.0).astype(BF16)
        acc[...] += jnp.dot(onehot, yo_ref[0], preferred_element_type=F32)

    @pl.when((flags & 4) != 0)
    def _():
        x2 = x1_ref[0] + gt_ref[0] * acc[...]
        if final_norm:
            x2 = x2 * lax.rsqrt(jnp.mean(x2 * x2, axis=-1, keepdims=True) + NORM_EPS) * g_ref[...]
        o_ref[0] = x2


def moe_combine(yo_sorted, tok_sorted, x1, gt2, final_g):
    B, R, D = yo_sorted.shape
    T = x1.shape[1]
    per = B // gt2.shape[0]
    rows, chunk = min(T, COMBINE_ROWS), min(R, COMBINE_CHUNK)
    assert T % rows == 0 and R % chunk == 0
    tile_of, chunk_of, flags = _combine_plan(tok_sorted, T, rows, chunk)
    g = jnp.ones((1, D), F32) if final_g is None else final_g.reshape(1, D)
    grid_spec = pltpu.PrefetchScalarGridSpec(
        num_scalar_prefetch=3,
        grid=(B, tile_of.shape[1]),
        in_specs=[pl.BlockSpec((1, chunk, D), lambda b, p, t, c, f: (b, c[b, p], 0)),
                  pl.BlockSpec((1, 1, 1, chunk), lambda b, p, t, c, f: (b, c[b, p], 0, 0)),
                  pl.BlockSpec((1, rows, D), lambda b, p, t, c, f: (b, t[b, p], 0)),
                  pl.BlockSpec((1, 1, D), lambda b, p, t, c, f: (b // per, 0, 0)),
                  pl.BlockSpec((1, D), lambda b, p, t, c, f: (0, 0))],
        out_specs=pl.BlockSpec((1, rows, D), lambda b, p, t, c, f: (b, t[b, p], 0)),
        scratch_shapes=[pltpu.VMEM((rows, D), F32)])
    return pl.pallas_call(
        functools.partial(_moe_combine_body, final_norm=final_g is not None),
        grid_spec=grid_spec,
        out_shape=jax.ShapeDtypeStruct((B, T, D), F32),
        compiler_params=_params(("parallel", "arbitrary")),
        name="moe_combine",
    )(tile_of, chunk_of, flags, yo_sorted, tok_sorted.reshape(B, R // chunk, 1, chunk), x1, gt2, g)


def _expert_choice_moe(h2, logits, x1, gt2, final_g, p):
    B, T, D = h2.shape
    cap = CAPACITY_FACTOR * T // N_EXPERTS
    aff = jax.nn.softmax(logits[..., :N_EXPERTS], axis=-1)
    vals, idx = lax.top_k(jnp.swapaxes(aff, 1, 2), cap)
    bidx = jnp.arange(B)[:, None, None]
    xg = h2[bidx, idx]
    xe = jnp.swapaxes(xg, 0, 1).reshape(N_EXPERTS, B * cap, D)
    ve = jnp.broadcast_to(jnp.swapaxes(vals, 0, 1).reshape(N_EXPERTS, B * cap, 1), (N_EXPERTS, B * cap, LANES))
    yo = moe_experts(xe, p['moe_w_gate'], p['moe_w_up'], p['moe_w_down'], ve)
    tok = idx.reshape(B, N_EXPERTS * cap)
    order = jnp.argsort(tok, axis=-1)
    tok_sorted = jnp.take_along_axis(tok, order, axis=-1)
    src_row = (order // cap) * (B * cap) + jnp.arange(B, dtype=order.dtype)[:, None] * cap + order % cap
    yo_sorted = yo.reshape(N_EXPERTS * B * cap, D)[src_row]
    return moe_combine(yo_sorted, tok_sorted, x1, gt2, final_g)


def _layer(x, mod, s_rw, s_hg, width, final_g, p):
    B, T, D = x.shape
    rows_per_mod = (B * T) // mod.shape[0]
    xf = x.reshape(B * T, D)
    h = norm_modulate(xf, mod, p['norm1_g'], rows_per_mod=rows_per_mod)
    proj = in_proj(h, p['w_in'], p['mu'], width=width)
    proj3 = proj.reshape(B, T, PROJ_COLS)
    yf, yb, bonus, gr, of, ob, hfin, sfin = recurrent_scans(
        proj3, p['rw_vec'], p['rw_w2pad'], p['rw_a2pad'], p['rw_g2'], p['hg_lb'],
        rwkv_state_to_blockdiag(s_rw), jnp.swapaxes(s_hg, -1, -2))
    flat = lambda u: u.reshape(B * T, u.shape[-1])
    merged = branch_merge(flat(yf), flat(yb), flat(bonus), flat(gr), flat(of), flat(ob), proj,
                          p['br_vec'], p['w_br_rwkv'], p['w_br_hgrn'])
    x1, h2, logits = out_proj(merged, xf, mod, p['norm2_g'], p['w_out'], p['moe_router'],
                              rows_per_mod=rows_per_mod)
    x2 = _expert_choice_moe(h2.reshape(B, T, D), logits.reshape(B, T, LANES), x1.reshape(B, T, D),
                            mod[:, :, 5 * D:6 * D], final_g, p)
    return x2, rwkv_blockdiag_to_state(hfin), jnp.swapaxes(sfin, -1, -2)


def kernel(x_prompt, x_sample, c, state_rwkv, state_hgrn, c_ctx, norm1_g, norm2_g, final_norm_g, ada_w, ada_b, w_in, rw_mu, rw_w0, rw_w2, rw_a0, rw_a2, rw_g2, rw_k_k, rw_k_a, rw_r_k, rw_ln_w, rw_ln_b, hg_lb, hg_norm_g, w_br_rwkv, w_br_hgrn, w_out, moe_router, moe_w_gate, moe_w_up, moe_w_down):
    depth = norm1_g.shape[0]
    D = D_MODEL
    lb_all = jnp.cumsum(jax.nn.softmax(hg_lb.astype(F32), axis=0), axis=0)
    bp = x_prompt.shape[0]
    bs = x_sample.shape[0]
    xp, xs = x_prompt, x_sample
    new_rw, new_hg = [], []
    for l in range(depth):
        wl = w_in[l]
        w_perm = jnp.concatenate([wl[:, RW_COLS + HG_COLS:], wl[:, RW_COLS:RW_COLS + HG_COLS], wl[:, :RW_COLS],
                                  jnp.zeros((D, PROJ_COLS - IN_COLS), F32)], axis=1).astype(BF16)
        mu_perm = jnp.concatenate([jnp.zeros((2, GATE_COLS + HG_COLS), F32), rw_mu[l],
                                   jnp.zeros((2, PROJ_COLS - IN_COLS), F32)], axis=1)
        pr = dict(rw_w0=rw_w0[l], rw_w2=rw_w2[l], rw_a0=rw_a0[l], rw_a2=rw_a2[l], rw_g2=rw_g2[l],
                  rw_k_k=rw_k_k[l], rw_k_a=rw_k_a[l], rw_r_k=rw_r_k[l])
        rw_vec, rw_w2pad, rw_a2pad, rw_g2b = rwkv_pack_params(pr)
        zrow = jnp.zeros((RW_DIM,), F32)
        br_vec = jnp.stack([rw_ln_w[l], rw_ln_b[l], hg_norm_g[l], zrow, zrow, zrow, zrow, zrow])
        p = dict(norm1_g=norm1_g[l][None, :], norm2_g=norm2_g[l][None, :], w_in=w_perm, mu=mu_perm,
                 rw_vec=rw_vec, rw_w2pad=rw_w2pad, rw_a2pad=rw_a2pad, rw_g2=rw_g2b,
                 hg_lb=lb_all[l], br_vec=br_vec,
                 w_br_rwkv=w_br_rwkv[l].astype(BF16), w_br_hgrn=w_br_hgrn[l].astype(BF16),
                 w_out=w_out[l].astype(BF16),
                 moe_router=jnp.pad(moe_router[l], ((0, 0), (0, LANES - N_EXPERTS))).astype(BF16),
                 moe_w_gate=cast_experts_bf16(moe_w_gate[l]), moe_w_up=cast_experts_bf16(moe_w_up[l]),
                 moe_w_down=cast_experts_bf16(moe_w_down[l]))
        cond = jnp.concatenate([c_ctx[None, :], c], axis=0)
        rows = -(-(1 + bs) // BF16_SUBLANES) * BF16_SUBLANES
        cond = jnp.pad(jax.nn.silu(cond), ((0, rows - 1 - bs), (0, 0)))
        mod = matmul(cond, ada_w[l], tm=rows, tn=ADA_COLS)[:1 + bs] + ada_b[l]
        mod = mod.reshape(1 + bs, 1, N_MOD * D)
        zero_rw = jnp.zeros((bp, 2, RW_HEADS, RW_HEAD, RW_HEAD), F32)
        zero_hg = jnp.zeros((bp, 2, HG_HEADS, HG_DK, HG_DV), F32)
        final_g = final_norm_g if l == depth - 1 else None
        xp, s_rw, s_hg = _layer(xp, mod[:1], zero_rw, zero_hg, xp.shape[1], final_g, p)
        new_rw.append(s_rw)
        new_hg.append(s_hg)
        xs, _, _ = _layer(xs, mod[1:], state_rwkv[:, l].astype(F32), state_hgrn[:, l].astype(F32), GRID_W,
                          final_g, p)
    return (xp, xs, jnp.stack(new_rw, axis=1), jnp.stack(new_hg, axis=1))
```

```python
import functools

import jax
import jax.numpy as jnp
from jax import lax
from jax.experimental import pallas as pl
from jax.experimental.pallas import tpu as pltpu

F32 = jnp.float32
BF16 = jnp.bfloat16

D_MODEL = 2048
GRID_W = 64
RW_HEADS = 16
RW_HEAD = 64
RW_DIM = RW_HEADS * RW_HEAD
DECAY_LORA = 64
AAA_LORA = 64
GATE_LORA = 128
RW_LORA_COLS = 2 * DECAY_LORA + 2 * AAA_LORA + GATE_LORA
RW_LN_EPS = 64e-5
HG_HEADS = 8
HG_DK = 128
HG_DV = 128
HG_KDIM = HG_HEADS * HG_DK
HG_VDIM = HG_HEADS * HG_DV
N_EXPERTS = 16
EXPERT_FF = 1024
CAPACITY_FACTOR = 2
NORM_EPS = 1e-6
N_MOD = 6
RW_COLS = 3 * RW_DIM + RW_LORA_COLS
HG_COLS = 3 * HG_KDIM + 2 * HG_VDIM
GATE_COLS = 2 * D_MODEL

LANES = 128
BF16_SUBLANES = 16
V7X_VMEM_LIMIT_BYTES = 52 * 1024 * 1024

NORM_ROWS = 512
PROJ_ROWS = 1024
BRANCH_ROWS = 256
OUT_ROWS = 512
EXPERT_ROWS = 512
ADA_COLS = 1024

V7X_MXU_WIDTH = 256
PLAIN_COLS = GATE_COLS + HG_COLS
PLAIN_TN = 4 * V7X_MXU_WIDTH
RW_PROJ_TN = 5 * V7X_MXU_WIDTH
RW_PROJ_COLS = -(-RW_COLS // RW_PROJ_TN) * RW_PROJ_TN
COL_GATE_R, COL_GATE_H = 0, 1
COL_HG_Q, COL_HG_FF, COL_HG_FB, COL_HG_I, COL_HG_G = (GATE_COLS // HG_KDIM + j for j in range(5))
COL_RW_R, COL_RW_K, COL_RW_V = 0, 1, 2
COL_RW_LORA = 3 * RW_DIM // RW_LORA_COLS

RW_CHUNK = 64
RW_PAIRS = RW_DIM // LANES
HG_L = 128
HG_FIRST_ROUND = 4
COMBINE_ROWS = 512
COMBINE_CHUNK = 512
DECAY_SCALE = 0.6065306597126334
LOG2_E = 1.4426950408889634


def _params(sem):
    return pltpu.CompilerParams(dimension_semantics=sem, vmem_limit_bytes=V7X_VMEM_LIMIT_BYTES)


def _dot(a, b):
    return jnp.dot(a.astype(BF16), b.astype(BF16), preferred_element_type=F32)


def _dot_nt(a, b):
    return lax.dot_general(a.astype(BF16), b.astype(BF16), (((1,), (1,)), ((), ())),
                           preferred_element_type=F32)


def _split_dot(x, w_bf16):
    hi = x.astype(BF16)
    lo = (x - hi.astype(F32)).astype(BF16)
    return (jnp.dot(hi, w_bf16, preferred_element_type=F32)
            + jnp.dot(lo, w_bf16, preferred_element_type=F32))


def _group_sums(x, ones_bd):
    parts = [_split_dot(x[:, p * LANES:(p + 1) * LANES], ones_bd) for p in range(x.shape[1] // LANES)]
    return jnp.concatenate(parts, axis=1)


def _ones_blocks(group):
    ri = lax.broadcasted_iota(jnp.int32, (LANES, LANES), 0)
    ci = lax.broadcasted_iota(jnp.int32, (LANES, LANES), 1)
    return jnp.where((ri // group) == (ci // group), 1.0, 0.0).astype(BF16)


def _sigmoid(x):
    return 1.0 / (1.0 + jnp.exp(-x))


def _cumsum_rows(x, reverse):
    L = x.shape[0]
    row = lax.broadcasted_iota(jnp.int32, x.shape, 0)
    s = 1
    while s < L:
        if reverse:
            x = x + jnp.where(row < L - s, pltpu.roll(x, L - s, 0), 0.0)
        else:
            x = x + jnp.where(row >= s, pltpu.roll(x, s, 0), 0.0)
        s *= 2
    return x


def _interleave(gens, first_round=None):
    results = [None] * len(gens)
    first_round = first_round or [0] * len(gens)
    active = list(enumerate(gens))
    rnd = 0
    while active:
        still = []
        for idx, g in active:
            if rnd < first_round[idx]:
                still.append((idx, g))
                continue
            try:
                next(g)
                still.append((idx, g))
            except StopIteration as e:
                results[idx] = e.value
        active = still
        rnd += 1
    return results


def _mm_body(x_ref, w_ref, o_ref):
    o_ref[...] = _dot(x_ref[...], w_ref[...]).astype(o_ref.dtype)


def matmul(x, w, *, tm, tn, out_dtype=F32):
    M, K = x.shape
    _, N = w.shape
    return pl.pallas_call(
        _mm_body,
        grid=(M // tm, N // tn),
        in_specs=[pl.BlockSpec((tm, K), lambda i, j: (i, 0)),
                  pl.BlockSpec((K, tn), lambda i, j: (0, j))],
        out_specs=pl.BlockSpec((tm, tn), lambda i, j: (i, j)),
        out_shape=jax.ShapeDtypeStruct((M, N), out_dtype),
        compiler_params=_params(("parallel", "parallel")),
        name="matmul",
    )(x, w)


def _cast_body(w_ref, o_ref):
    o_ref[...] = w_ref[...].astype(o_ref.dtype)


def cast_experts_bf16(w):
    E, K, N = w.shape
    return pl.pallas_call(
        _cast_body,
        grid=(E,),
        in_specs=[pl.BlockSpec((1, K, N), lambda e: (e, 0, 0))],
        out_specs=pl.BlockSpec((1, K, N), lambda e: (e, 0, 0)),
        out_shape=jax.ShapeDtypeStruct((E, K, N), BF16),
        compiler_params=_params(("parallel",)),
        name="cast_experts_bf16",
    )(w)


def _norm_mod_body(x_ref, mod_ref, g_ref, o_ref):
    x = x_ref[...]
    y = x * lax.rsqrt(jnp.mean(x * x, axis=-1, keepdims=True) + NORM_EPS) * g_ref[...]
    m = mod_ref[0]
    o_ref[...] = (y * (1.0 + m[:, D_MODEL:2 * D_MODEL]) + m[:, 0:D_MODEL]).astype(o_ref.dtype)


def norm_modulate(x, mod, norm_g, *, rows_per_mod, tm=NORM_ROWS):
    M, D = x.shape
    assert M % tm == 0 and rows_per_mod % tm == 0
    per = rows_per_mod // tm
    return pl.pallas_call(
        _norm_mod_body,
        grid=(M // tm,),
        in_specs=[pl.BlockSpec((tm, D), lambda i: (i, 0)),
                  pl.BlockSpec((1, 1, N_MOD * D), lambda i: (i // per, 0, 0)),
                  pl.BlockSpec((1, D), lambda i: (0, 0))],
        out_specs=pl.BlockSpec((tm, D), lambda i: (i, 0)),
        out_shape=jax.ShapeDtypeStruct((M, D), BF16),
        compiler_params=_params(("parallel",)),
        name="norm_modulate",
    )(x, mod, norm_g)


def _in_proj_rwkv_body(h_ref, w_ref, mu_ref, o_ref, *, width):
    acc = jnp.dot(h_ref[...], w_ref[...], preferred_element_type=F32)
    tm = acc.shape[0]
    pos = lax.broadcasted_iota(jnp.int32, acc.shape, 0) & (width - 1)
    prev = jnp.where(pos == 0, 0.0, pltpu.roll(acc, 1, 0))
    nxt = jnp.where(pos == width - 1, 0.0, pltpu.roll(acc, tm - 1, 0))
    mu = mu_ref[...]
    o_ref[...] = acc + mu[0:1, :] * (prev - acc) + mu[1:2, :] * (nxt - acc)


def in_proj_rwkv(h, w_rw, mu_rw, *, width, tm=PROJ_ROWS):
    M, D = h.shape
    assert M % tm == 0 and tm % width == 0
    return pl.pallas_call(
        functools.partial(_in_proj_rwkv_body, width=width),
        grid=(M // tm, RW_PROJ_COLS // RW_PROJ_TN),
        in_specs=[pl.BlockSpec((tm, D), lambda i, j: (i, 0)),
                  pl.BlockSpec((D, RW_PROJ_TN), lambda i, j: (0, j)),
                  pl.BlockSpec((2, RW_PROJ_TN), lambda i, j: (0, j))],
        out_specs=pl.BlockSpec((tm, RW_PROJ_TN), lambda i, j: (i, j)),
        out_shape=jax.ShapeDtypeStruct((M, RW_PROJ_COLS), F32),
        compiler_params=_params(("parallel", "parallel")),
        name="in_proj_rwkv",
    )(h, w_rw, mu_rw)


def _block_diag(x, lane_lo):
    return jnp.concatenate([jnp.where(lane_lo, x, 0.0), jnp.where(lane_lo, 0.0, x)], axis=0)


def _rwkv_masks(reverse):
    L = RW_CHUNK
    n2 = 2 * L
    ri = lax.broadcasted_iota(jnp.int32, (n2, n2), 0)
    ci = lax.broadcasted_iota(jnp.int32, (n2, n2), 1)
    rt = ri & (L - 1)
    ct = ci & (L - 1)
    blocks = []
    rb, cb = rt, ct
    s = 1
    while s < L:
        rb = rb >> 1
        cb = cb >> 1
        blocks.append(rb == cb)
        s *= 2
    couplings = [jnp.logical_and(blocks[j + 1], jnp.logical_not(blocks[j])) for j in range(len(blocks) - 1)]
    return dict(
        lane_lo=lax.broadcasted_iota(jnp.int32, (L, LANES), 1) < RW_HEAD,
        eye=ri == ci,
        strict=(rt < ct) if reverse else (rt > ct),
        incl=(rt <= ct) if reverse else (rt >= ct),
        pair_block=blocks[0],
        couplings=couplings)


def _rwkv_pair_chunk(q, lw, kd, v, kk, a, reverse, m):
    L = q.shape[0]
    n2 = 2 * L
    lane_lo = m['lane_lo']
    c = _cumsum_rows(lw, reverse)
    c_last = c[0:1, :] if reverse else c[L - 1:L, :]
    g_inc = jnp.exp(c)
    g_exc = jnp.exp(c - lw)
    g_inv = jnp.exp(-c)
    g_rem = jnp.exp(c_last - c)
    beta = kk * a
    ab = _block_diag(-kk * g_exc, lane_lo)
    qb = _block_diag(q * g_inc, lane_lo)
    bt = _block_diag(beta * g_inv, lane_lo)
    kt = _block_diag(kd * g_inv, lane_lo)
    vb = _block_diag(v, lane_lo)
    bh = _block_diag(beta * g_rem, lane_lo)
    kh = _block_diag(kd * g_rem, lane_lo)

    aa = _dot_nt(jnp.concatenate([ab, qb], axis=0), jnp.concatenate([bt, kt], axis=0))
    yield
    a_ab = jnp.where(m['strict'], aa[:n2, :n2], 0.0)
    a_ak = jnp.where(m['strict'], aa[:n2, n2:], 0.0)
    a_qb = jnp.where(m['incl'], aa[n2:, :n2], 0.0)
    a_qk = jnp.where(m['incl'], aa[n2:, n2:], 0.0)

    P = jnp.where(m['eye'], 1.0, 0.0) + jnp.where(m['pair_block'], a_ab, 0.0)
    av = _dot(a_ak, vb)
    for coupling_mask in m['couplings']:
        pc = _dot(P, jnp.where(coupling_mask, a_ab, 0.0))
        yield
        P = P + _dot(pc, P)
        yield

    wu_u0 = _dot(P, jnp.concatenate([ab, av], axis=1))
    yield
    rhs = jnp.concatenate([wu_u0, jnp.concatenate([jnp.zeros_like(vb), vb], axis=1)], axis=0)
    qy = _dot(jnp.concatenate([a_qb, a_qk], axis=1), rhs)
    st = _dot(jnp.concatenate([bh, kh], axis=0).T, rhs)
    yield
    q_eff = qb + qy[:, :LANES]
    g_last = jnp.exp(jnp.broadcast_to(c_last, (n2, LANES)))
    m_mat = jnp.where(m['eye'], g_last, 0.0) + st[:, :LANES]
    return jnp.concatenate([q_eff, m_mat], axis=0), jnp.concatenate([qy[:, LANES:], st[:, LANES:]], axis=0)


def _rwkv_apply_state(lhs, add, H):
    L = RW_CHUNK
    out = _dot(lhs, H) + add
    return out[:L] + out[L:2 * L], out[2 * L:]


def _rwkv_prep(r, k, v, lora, vec, w2pad, a2pad, ones_bd, d, want_other_kd):
    lora_in_w = lora[:, 0:LANES]
    lora_in_a = lora[:, LANES:2 * LANES]
    kk = k * vec[4:5, :]
    ss = _group_sums(kk * kk, ones_bd)
    kk = kk * lax.rsqrt(jnp.maximum(ss, 1e-24))
    u = vec[d:d + 1, :] + jnp.dot(jnp.tanh(lora_in_w).astype(BF16), w2pad[d], preferred_element_type=F32)
    lw = -DECAY_SCALE * _sigmoid(u)
    a = _sigmoid(vec[2 + d:3 + d, :] + jnp.dot(lora_in_a.astype(BF16), a2pad[d], preferred_element_type=F32))
    kd = k * (1.0 + (a - 1.0) * vec[5:6, :])
    out = dict(r=r, k=k, v=v, kk=kk, lw=lw, a=a, kd=kd)
    if want_other_kd:
        o = 1 - d
        a_o = _sigmoid(vec[2 + o:3 + o, :] + jnp.dot(lora_in_a.astype(BF16), a2pad[o], preferred_element_type=F32))
        out['kd_other'] = k * (1.0 + (a_o - 1.0) * vec[5:6, :])
    return out


def _scan_body(rf_ref, kf_ref, vf_ref, lf_ref, rb_ref, kb_ref, vb_ref, lb_ref,
               qf_ref, ff_ref, if_ref, qb_ref, fb_ref, ib_ref,
               vec_ref, w2_ref, a2_ref, g2_ref, hglb_ref, h0_ref, s0_ref,
               yf_ref, yb_ref, bonus_ref, g_ref, of_ref, ob_ref, hfin_ref, sfin_ref, h_scr, s_scr):
    i = pl.program_id(1)
    n = pl.num_programs(1)

    @pl.when(i == 0)
    def _():
        h_scr[...] = h0_ref[0]
        s_scr[...] = s0_ref[0]

    vec = vec_ref[...]
    ones_bd = _ones_blocks(RW_HEAD)
    fw = _rwkv_prep(rf_ref[0], kf_ref[0], vf_ref[0], lf_ref[0], vec, w2_ref, a2_ref, ones_bd, 0, True)
    bw = _rwkv_prep(rb_ref[0], kb_ref[0], vb_ref[0], lb_ref[0], vec, w2_ref, a2_ref, ones_bd, 1, False)

    rk = fw['r'] * vec[6:7, :]
    bonus_ref[0] = _group_sums(rk * (fw['kd'] + fw['kd_other']), ones_bd) * fw['v']
    gd = lf_ref[0][:, 2 * LANES:3 * LANES]
    g_ref[0] = jnp.dot(_sigmoid(gd).astype(BF16), g2_ref[...], preferred_element_type=F32)

    chunk_order = ((0, 1), (1, 0))
    rw_masks = (_rwkv_masks(False), _rwkv_masks(True))
    hg_masks = (_gla_masks(False), _gla_masks(True))
    chains = []
    for d, t in enumerate((fw, bw)):
        for half in chunk_order[d]:
            rows = slice(half * RW_CHUNK, (half + 1) * RW_CHUNK)
            for p in range(RW_PAIRS):
                sl = slice(p * LANES, (p + 1) * LANES)
                chains.append(_rwkv_pair_chunk(t['r'][rows, sl], t['lw'][rows, sl], t['kd'][rows, sl],
                                               t['v'][rows, sl], t['kk'][rows, sl], t['a'][rows, sl], d == 1,
                                               rw_masks[d]))
    n_rw = len(chains)
    for d, (q_ref, f_ref, v_ref) in enumerate(((qf_ref, ff_ref, if_ref), (qb_ref, fb_ref, ib_ref))):
        for h in range(HG_HEADS):
            sl = slice(h * LANES, (h + 1) * LANES)
            chains.append(_gla_head_chunk(q_ref[0, :, sl], f_ref[0, :, sl], v_ref[0, :, sl],
                                          hglb_ref[d:d + 1, sl], s_scr[d, h], d == 1, hg_masks[d]))
    outs = _interleave(chains, [0] * n_rw + [HG_FIRST_ROUND] * (len(chains) - n_rw))

    states = [[h_scr[d, p] for p in range(RW_PAIRS)] for d in range(2)]
    for step in range(2):
        for d, y_ref in enumerate((yf_ref, yb_ref)):
            half = chunk_order[d][step]
            for p in range(RW_PAIRS):
                lhs, add = outs[(d * 2 + step) * RW_PAIRS + p]
                y, states[d][p] = _rwkv_apply_state(lhs, add, states[d][p])
                y_ref[0, half * RW_CHUNK:(half + 1) * RW_CHUNK, p * LANES:(p + 1) * LANES] = y
    for d in range(2):
        for p in range(RW_PAIRS):
            h_scr[d, p] = states[d][p]

    for d, o_ref in enumerate((of_ref, ob_ref)):
        for h in range(HG_HEADS):
            o, st_new = outs[n_rw + d * HG_HEADS + h]
            o_ref[0, :, h * LANES:(h + 1) * LANES] = o
            s_scr[d, h] = st_new

    @pl.when(i == n - 1)
    def _():
        hfin_ref[0] = h_scr[...]
        sfin_ref[0] = s_scr[...]


def recurrent_scans(proj_rw, proj_plain, vec, w2pad, a2pad, g2, hg_lb, h0, s0t):
    B, T, _ = proj_rw.shape
    assert T % HG_L == 0 and HG_L == 2 * RW_CHUNK
    n = T // HG_L
    fwd = lambda c: (lambda b, i: (b, i, c))
    bwd = lambda c: (lambda b, i: (b, n - 1 - i, c))
    wide = lambda fn: pl.BlockSpec((1, HG_L, RW_DIM), fn)
    lora = lambda fn: pl.BlockSpec((1, HG_L, RW_LORA_COLS), fn)
    const = lambda shape: pl.BlockSpec(shape, lambda b, i: (0,) * len(shape))
    rw_st = pl.BlockSpec((1, 2, RW_PAIRS, LANES, LANES), lambda b, i: (b, 0, 0, 0, 0))
    hg_st = pl.BlockSpec((1, 2, HG_HEADS, HG_DV, HG_DK), lambda b, i: (b, 0, 0, 0, 0))
    out_sd = jax.ShapeDtypeStruct((B, T, RW_DIM), F32)
    return pl.pallas_call(
        _scan_body,
        grid=(B, n),
        in_specs=[wide(fwd(COL_RW_R)), wide(fwd(COL_RW_K)), wide(fwd(COL_RW_V)), lora(fwd(COL_RW_LORA)),
                  wide(bwd(COL_RW_R)), wide(bwd(COL_RW_K)), wide(bwd(COL_RW_V)), lora(bwd(COL_RW_LORA)),
                  wide(fwd(COL_HG_Q)), wide(fwd(COL_HG_FF)), wide(fwd(COL_HG_I)),
                  wide(bwd(COL_HG_Q)), wide(bwd(COL_HG_FB)), wide(bwd(COL_HG_I)),
                  const((8, RW_DIM)), const((2, LANES, RW_DIM)), const((2, LANES, RW_DIM)),
                  const((LANES, RW_DIM)), const((2, HG_KDIM)), rw_st, hg_st],
        out_specs=[wide(fwd(0)), wide(bwd(0)), wide(fwd(0)), wide(fwd(0)), wide(fwd(0)), wide(bwd(0)),
                   rw_st, hg_st],
        out_shape=[out_sd] * 6 + [jax.ShapeDtypeStruct((B, 2, RW_PAIRS, LANES, LANES), F32),
                                  jax.ShapeDtypeStruct((B, 2, HG_HEADS, HG_DV, HG_DK), F32)],
        scratch_shapes=[pltpu.VMEM((2, RW_PAIRS, LANES, LANES), F32),
                        pltpu.VMEM((2, HG_HEADS, HG_DV, HG_DK), F32)],
        compiler_params=_params(("parallel", "arbitrary")),
        name="recurrent_scans",
    )(*([proj_rw] * 8), *([proj_plain] * 6), vec, w2pad, a2pad, g2, hg_lb, h0, s0t)


def rwkv_pack_params(p):
    z = jnp.zeros((RW_HEAD, RW_DIM), F32)
    w2pad = jnp.stack([jnp.concatenate([p['rw_w2'][0], z], 0), jnp.concatenate([z, p['rw_w2'][1]], 0)]).astype(BF16)
    a2pad = jnp.stack([jnp.concatenate([p['rw_a2'][0], z], 0), jnp.concatenate([z, p['rw_a2'][1]], 0)]).astype(BF16)
    vec = jnp.stack([p['rw_w0'][0], p['rw_w0'][1], p['rw_a0'][0], p['rw_a0'][1], p['rw_k_k'], p['rw_k_a'],
                     p['rw_r_k'].reshape(RW_DIM), jnp.zeros((RW_DIM,), F32)])
    return vec, w2pad, a2pad, p['rw_g2'].astype(BF16)


def rwkv_state_to_blockdiag(s):
    B = s.shape[0]
    h = jnp.swapaxes(s, -1, -2).reshape(B, 2, RW_PAIRS, 2, RW_HEAD, RW_HEAD)
    z = jnp.zeros_like(h[:, :, :, 0])
    top = jnp.concatenate([h[:, :, :, 0], z], axis=-1)
    bot = jnp.concatenate([z, h[:, :, :, 1]], axis=-1)
    return jnp.concatenate([top, bot], axis=-2)


def rwkv_blockdiag_to_state(hb):
    B = hb.shape[0]
    h0 = hb[:, :, :, :RW_HEAD, :RW_HEAD]
    h1 = hb[:, :, :, RW_HEAD:, RW_HEAD:]
    h = jnp.stack([h0, h1], axis=3).reshape(B, 2, RW_HEADS, RW_HEAD, RW_HEAD)
    return jnp.swapaxes(h, -1, -2)


def _gla_masks(reverse):
    L = HG_L
    row = lax.broadcasted_iota(jnp.int32, (L, LANES), 0)
    ri = lax.broadcasted_iota(jnp.int32, (L, L), 0)
    ci = lax.broadcasted_iota(jnp.int32, (L, L), 1)
    levels = []
    half = 1
    while half < L:
        size = 2 * half
        upper = (row & half) != 0
        r_up = (ri & half) != 0
        c_up = (ci & half) != 0
        if reverse:
            is_q = jnp.logical_not(upper)
            pair = jnp.logical_and(jnp.logical_not(r_up), c_up)
        else:
            is_q = upper
            pair = jnp.logical_and(r_up, jnp.logical_not(c_up))
        if size < L:
            pair = jnp.logical_and(pair, (ri // size) == (ci // size))
        levels.append(dict(half=half, upper=upper, is_q=is_q, pair=pair,
                           scale=jnp.where(is_q, LOG2_E, -LOG2_E)))
        half = size
    return dict(eye=ri == ci, levels=levels)


def _gla_head_chunk(q_raw, fl, v, lb, St, reverse, m):
    L = q_raw.shape[0]
    q = q_raw * _sigmoid(q_raw)
    f = lb + (1.0 - lb) * _sigmoid(fl)
    k = 1.0 - f
    g = jnp.log(f)
    b = _cumsum_rows(g, reverse)
    b_last = b[0:1, :] if reverse else b[L - 1:L, :]

    att = jnp.where(m['eye'], _dot_nt(q, k), 0.0)
    fk = b
    for lv in m['levels']:
        half, upper, is_q = lv['half'], lv['upper'], lv['is_q']
        if reverse:
            bref = jnp.where(upper, fk, pltpu.roll(fk, L - half, 0))
        else:
            bref = jnp.where(upper, pltpu.roll(fk, half, 0), fk)
        e = jnp.exp2((b - bref) * lv['scale'])
        x = (jnp.where(is_q, q, k) * e).astype(BF16)
        prod = lax.dot_general(x, x, (((1,), (1,)), ((), ())), preferred_element_type=F32)
        yield
        att = jnp.where(lv['pair'], prod, att)
        if reverse:
            fk = jnp.where(upper, pltpu.roll(fk, half, 0), fk)
        else:
            fk = jnp.where(upper, fk, pltpu.roll(fk, L - half, 0))
    o = _dot(att, v) + _dot_nt(q * jnp.exp(b), St)
    kst = k * jnp.exp(b_last - b)
    st_new = St * jnp.exp(b_last) + _dot(v.T, kst)
    yield
    return o, st_new


def _branch_merge_body(yf_ref, yb_ref, bonus_ref, gr_ref, of_ref, ob_ref, gh_ref, pgr_ref, pgh_ref,
                       vec_ref, wr_ref, wh_ref, o_ref):
    vec = vec_ref[...]
    y = yf_ref[...] + yb_ref[...]
    ones_head = _ones_blocks(RW_HEAD)
    mean = _group_sums(y, ones_head) * (1.0 / RW_HEAD)
    dlt = y - mean
    var = _group_sums(dlt * dlt, ones_head) * (1.0 / RW_HEAD)
    yn = dlt * lax.rsqrt(var + RW_LN_EPS) * vec[0:1, :] + vec[1:2, :]
    out_r = (yn + bonus_ref[...]) * gr_ref[...]
    o = of_ref[...] + ob_ref[...]
    ms = _group_sums(o * o, _ones_blocks(LANES)) * (1.0 / HG_DV)
    gh = gh_ref[...]
    out_h = o * lax.rsqrt(ms + NORM_EPS) * vec[2:3, :] * (gh * _sigmoid(gh))
    br = _dot(out_r, wr_ref[...])
    bh = _dot(out_h, wh_ref[...])
    o_ref[...] = (_sigmoid(pgr_ref[...]) * br + _sigmoid(pgh_ref[...]) * bh).astype(o_ref.dtype)


def branch_merge(yf, yb, bonus, gr, of, ob, proj, vec, w_br_r, w_br_h, *, tm=BRANCH_ROWS):
    M = yf.shape[0]
    assert M % tm == 0
    row = lambda w, c: pl.BlockSpec((tm, w), lambda i: (i, c))
    const = lambda shape: pl.BlockSpec(shape, lambda i: (0,) * len(shape), pipeline_mode=pl.Buffered(1))
    return pl.pallas_call(
        _branch_merge_body,
        grid=(M // tm,),
        in_specs=[row(RW_DIM, 0), row(RW_DIM, 0), row(RW_DIM, 0), row(RW_DIM, 0),
                  row(HG_VDIM, 0), row(HG_VDIM, 0), row(HG_VDIM, COL_HG_G),
                  row(D_MODEL, COL_GATE_R), row(D_MODEL, COL_GATE_H),
                  const((8, RW_DIM)), const((RW_DIM, D_MODEL)), const((HG_VDIM, D_MODEL))],
        out_specs=pl.BlockSpec((tm, D_MODEL), lambda i: (i, 0)),
        out_shape=jax.ShapeDtypeStruct((M, D_MODEL), BF16),
        compiler_params=_params(("parallel",)),
        name="branch_merge",
    )(yf, yb, bonus, gr, of, ob, proj, proj, proj, vec, w_br_r, w_br_h)


def _out_proj_body(m_ref, x_ref, mod_ref, g_ref, w_ref, wr_ref, x1_ref, h2_ref, lg_ref):
    mod = mod_ref[0]
    D = D_MODEL
    x1 = x_ref[...] + mod[:, 2 * D:3 * D] * jnp.dot(m_ref[...], w_ref[...], preferred_element_type=F32)
    x1_ref[...] = x1
    y = x1 * lax.rsqrt(jnp.mean(x1 * x1, axis=-1, keepdims=True) + NORM_EPS) * g_ref[...]
    h2 = (y * (1.0 + mod[:, 4 * D:5 * D]) + mod[:, 3 * D:4 * D]).astype(BF16)
    h2_ref[...] = h2
    lg_ref[...] = jnp.dot(h2, wr_ref[...], preferred_element_type=F32)


def out_proj(merged, x, mod, norm_g, w_out, w_router, *, rows_per_mod, tm=OUT_ROWS):
    M, D = x.shape
    assert M % tm == 0 and rows_per_mod % tm == 0
    per = rows_per_mod // tm
    return pl.pallas_call(
        _out_proj_body,
        grid=(M // tm,),
        in_specs=[pl.BlockSpec((tm, D), lambda i: (i, 0)),
                  pl.BlockSpec((tm, D), lambda i: (i, 0)),
                  pl.BlockSpec((1, 1, N_MOD * D), lambda i: (i // per, 0, 0)),
                  pl.BlockSpec((1, D), lambda i: (0, 0)),
                  pl.BlockSpec((D, D), lambda i: (0, 0), pipeline_mode=pl.Buffered(1)),
                  pl.BlockSpec((D, LANES), lambda i: (0, 0), pipeline_mode=pl.Buffered(1))],
        out_specs=[pl.BlockSpec((tm, D), lambda i: (i, 0)),
                   pl.BlockSpec((tm, D), lambda i: (i, 0)),
                   pl.BlockSpec((tm, LANES), lambda i: (i, 0))],
        out_shape=[jax.ShapeDtypeStruct((M, D), F32), jax.ShapeDtypeStruct((M, D), BF16),
                   jax.ShapeDtypeStruct((M, LANES), F32)],
        compiler_params=_params(("parallel",)),
        name="out_proj",
    )(merged, x, mod, norm_g, w_out, w_router)


def _experts_body(x_ref, wg_ref, wu_ref, wd_ref, val_ref, o_ref):
    x = x_ref[0]
    gate = jnp.dot(x, wg_ref[0], preferred_element_type=F32)
    up = jnp.dot(x, wu_ref[0], preferred_element_type=F32)
    hid = (gate * _sigmoid(gate) * up).astype(BF16)
    yo = jnp.dot(hid, wd_ref[0], preferred_element_type=F32)
    scale = jnp.concatenate([val_ref[0]] * (D_MODEL // LANES), axis=1)
    o_ref[0] = (yo * scale).astype(o_ref.dtype)


def moe_experts(xe, w_gate, w_up, w_down, vals, *, tm=EXPERT_ROWS):
    E, M, D = xe.shape
    tm = min(tm, M)
    assert M % tm == 0
    return pl.pallas_call(
        _experts_body,
        grid=(E, M // tm),
        in_specs=[pl.BlockSpec((1, tm, D), lambda e, i: (e, i, 0)),
                  pl.BlockSpec((1, D, EXPERT_FF), lambda e, i: (e, 0, 0)),
                  pl.BlockSpec((1, D, EXPERT_FF), lambda e, i: (e, 0, 0)),
                  pl.BlockSpec((1, EXPERT_FF, D), lambda e, i: (e, 0, 0)),
                  pl.BlockSpec((1, tm, LANES), lambda e, i: (e, i, 0))],
        out_specs=pl.BlockSpec((1, tm, D), lambda e, i: (e, i, 0)),
        out_shape=jax.ShapeDtypeStruct((E, M, D), BF16),
        compiler_params=_params(("parallel", "parallel")),
        name="moe_experts",
    )(xe, w_gate, w_up, w_down, vals)


def _combine_plan(tok_sorted, T, rows, chunk):
    B, R = tok_sorted.shape
    n_tiles, n_chunks = T // rows, R // chunk
    n_pairs = n_tiles + n_chunks
    bounds = jnp.arange(n_tiles + 1, dtype=jnp.int32) * rows
    start = jnp.sum((tok_sorted[:, None, :] < bounds[None, :, None]).astype(jnp.int32), axis=-1)
    first_chunk = jnp.minimum(start[:, :-1] // chunk, n_chunks - 1)
    last_chunk = jnp.clip((start[:, 1:] - 1) // chunk, first_chunk, n_chunks - 1)
    count = last_chunk - first_chunk + 1
    offset = jnp.cumsum(count, axis=-1) - count
    total = offset[:, -1] + count[:, -1]
    pair = jnp.arange(n_pairs, dtype=jnp.int32)
    tile_of = jnp.sum((pair[None, :, None] >= offset[:, None, :]).astype(jnp.int32), axis=-1) - 1
    k_in = pair[None, :] - jnp.take_along_axis(offset, tile_of, axis=1)
    count_t = jnp.take_along_axis(count, tile_of, axis=1)
    chunk_of = jnp.take_along_axis(first_chunk, tile_of, axis=1) + jnp.minimum(k_in, count_t - 1)
    real = pair[None, :] < total[:, None]
    flags = (jnp.logical_and(real, k_in == 0).astype(jnp.int32)
             + 2 * real.astype(jnp.int32)
             + 4 * jnp.logical_and(real, k_in == count_t - 1).astype(jnp.int32))
    return tile_of, chunk_of, flags


def _moe_combine_body(tile_ref, chunk_ref, flag_ref, yo_ref, tok_ref, x1_ref, gt_ref, g_ref, o_ref, acc, *, final_norm):
    b = pl.program_id(0)
    p = pl.program_id(1)
    flags = flag_ref[b, p]
    rows = x1_ref.shape[1]

    @pl.when((flags & 1) != 0)
    def _():
        acc[...] = jnp.zeros(acc.shape, F32)

    @pl.when((flags & 2) != 0)
    def _():
        tok = lax.broadcasted_iota(jnp.int32, (rows, tok_ref.shape[3]), 0) + tile_ref[b, p] * rows
        onehot = jnp.where(tok == tok_ref[0, 0], 1.0, 0.0).astype(BF16)
        acc[...] += jnp.dot(onehot, yo_ref[0], preferred_element_type=F32)

    @pl.when((flags & 4) != 0)
    def _():
        x2 = x1_ref[0] + gt_ref[0] * acc[...]
        if final_norm:
            x2 = x2 * lax.rsqrt(jnp.mean(x2 * x2, axis=-1, keepdims=True) + NORM_EPS) * g_ref[...]
        o_ref[0] = x2


def moe_combine(yo_sorted, tok_sorted, x1, gt2, final_g):
    B, R, D = yo_sorted.shape
    T = x1.shape[1]
    per = B // gt2.shape[0]
    rows, chunk = min(T, COMBINE_ROWS), min(R, COMBINE_CHUNK)
    assert T % rows == 0 and R % chunk == 0
    tile_of, chunk_of, flags = _combine_plan(tok_sorted, T, rows, chunk)
    g = jnp.ones((1, D), F32) if final_g is None else final_g.reshape(1, D)
    grid_spec = pltpu.PrefetchScalarGridSpec(
        num_scalar_prefetch=3,
        grid=(B, tile_of.shape[1]),
        in_specs=[pl.BlockSpec((1, chunk, D), lambda b, p, t, c, f: (b, c[b, p], 0)),
                  pl.BlockSpec((1, 1, 1, chunk), lambda b, p, t, c, f: (b, c[b, p], 0, 0)),
                  pl.BlockSpec((1, rows, D), lambda b, p, t, c, f: (b, t[b, p], 0)),
                  pl.BlockSpec((1, 1, D), lambda b, p, t, c, f: (b // per, 0, 0)),
                  pl.BlockSpec((1, D), lambda b, p, t, c, f: (0, 0))],
        out_specs=pl.BlockSpec((1, rows, D), lambda b, p, t, c, f: (b, t[b, p], 0)),
        scratch_shapes=[pltpu.VMEM((rows, D), F32)])
    return pl.pallas_call(
        functools.partial(_moe_combine_body, final_norm=final_g is not None),
        grid_spec=grid_spec,
        out_shape=jax.ShapeDtypeStruct((B, T, D), F32),
        compiler_params=_params(("parallel", "arbitrary")),
        name="moe_combine",
    )(tile_of, chunk_of, flags, yo_sorted, tok_sorted.reshape(B, R // chunk, 1, chunk), x1, gt2, g)


def _expert_choice_moe(h2, logits, x1, gt2, final_g, p):
    B, T, D = h2.shape
    cap = CAPACITY_FACTOR * T // N_EXPERTS
    aff = jax.nn.softmax(logits[..., :N_EXPERTS], axis=-1)
    vals, idx = lax.top_k(jnp.swapaxes(aff, 1, 2), cap)
    bidx = jnp.arange(B)[:, None, None]
    xg = h2[bidx, idx]
    xe = jnp.swapaxes(xg, 0, 1).reshape(N_EXPERTS, B * cap, D)
    ve = jnp.broadcast_to(jnp.swapaxes(vals, 0, 1).reshape(N_EXPERTS, B * cap, 1), (N_EXPERTS, B * cap, LANES))
    yo = moe_experts(xe, p['moe_w_gate'], p['moe_w_up'], p['moe_w_down'], ve)
    tok = idx.reshape(B, N_EXPERTS * cap)
    order = jnp.argsort(tok, axis=-1)
    tok_sorted = jnp.take_along_axis(tok, order, axis=-1)
    src_row = (order // cap) * (B * cap) + jnp.arange(B, dtype=order.dtype)[:, None] * cap + order % cap
    yo_sorted = yo.reshape(N_EXPERTS * B * cap, D)[src_row]
    return moe_combine(yo_sorted, tok_sorted, x1, gt2, final_g)


def _layer(x, mod, s_rw, s_hg, width, final_g, p):
    B, T, D = x.shape
    rows_per_mod = (B * T) // mod.shape[0]
    xf = x.reshape(B * T, D)
    h = norm_modulate(xf, mod, p['norm1_g'], rows_per_mod=rows_per_mod)
    proj = matmul(h, p['w_in_plain'], tm=PROJ_ROWS, tn=PLAIN_TN)
    proj_rw = in_proj_rwkv(h, p['w_in_rw'], p['mu'], width=width)
    yf, yb, bonus, gr, of, ob, hfin, sfin = recurrent_scans(
        proj_rw.reshape(B, T, RW_PROJ_COLS), proj.reshape(B, T, PLAIN_COLS),
        p['rw_vec'], p['rw_w2pad'], p['rw_a2pad'], p['rw_g2'], p['hg_lb'],
        rwkv_state_to_blockdiag(s_rw), jnp.swapaxes(s_hg, -1, -2))
    flat = lambda u: u.reshape(B * T, u.shape[-1])
    merged = branch_merge(flat(yf), flat(yb), flat(bonus), flat(gr), flat(of), flat(ob), proj,
                          p['br_vec'], p['w_br_rwkv'], p['w_br_hgrn'])
    x1, h2, logits = out_proj(merged, xf, mod, p['norm2_g'], p['w_out'], p['moe_router'],
                              rows_per_mod=rows_per_mod)
    x2 = _expert_choice_moe(h2.reshape(B, T, D), logits.reshape(B, T, LANES), x1.reshape(B, T, D),
                            mod[:, :, 5 * D:6 * D], final_g, p)
    return x2, rwkv_blockdiag_to_state(hfin), jnp.swapaxes(sfin, -1, -2)


def kernel(x_prompt, x_sample, c, state_rwkv, state_hgrn, c_ctx, norm1_g, norm2_g, final_norm_g, ada_w, ada_b, w_in, rw_mu, rw_w0, rw_w2, rw_a0, rw_a2, rw_g2, rw_k_k, rw_k_a, rw_r_k, rw_ln_w, rw_ln_b, hg_lb, hg_norm_g, w_br_rwkv, w_br_hgrn, w_out, moe_router, moe_w_gate, moe_w_up, moe_w_down):
    depth = norm1_g.shape[0]
    D = D_MODEL
    lb_all = jnp.cumsum(jax.nn.softmax(hg_lb.astype(F32), axis=0), axis=0)
    bp = x_prompt.shape[0]
    bs = x_sample.shape[0]
    xp, xs = x_prompt, x_sample
    new_rw, new_hg = [], []
    for l in range(depth):
        wl = w_in[l]
        w_plain = jnp.concatenate([wl[:, RW_COLS + HG_COLS:], wl[:, RW_COLS:RW_COLS + HG_COLS]], axis=1).astype(BF16)
        w_rw = jnp.pad(wl[:, :RW_COLS], ((0, 0), (0, RW_PROJ_COLS - RW_COLS))).astype(BF16)
        mu_rw = jnp.pad(rw_mu[l], ((0, 0), (0, RW_PROJ_COLS - RW_COLS)))
        pr = dict(rw_w0=rw_w0[l], rw_w2=rw_w2[l], rw_a0=rw_a0[l], rw_a2=rw_a2[l], rw_g2=rw_g2[l],
                  rw_k_k=rw_k_k[l], rw_k_a=rw_k_a[l], rw_r_k=rw_r_k[l])
        rw_vec, rw_w2pad, rw_a2pad, rw_g2b = rwkv_pack_params(pr)
        zrow = jnp.zeros((RW_DIM,), F32)
        br_vec = jnp.stack([rw_ln_w[l], rw_ln_b[l], hg_norm_g[l], zrow, zrow, zrow, zrow, zrow])
        p = dict(norm1_g=norm1_g[l][None, :], norm2_g=norm2_g[l][None, :], w_in_plain=w_plain, w_in_rw=w_rw, mu=mu_rw,
                 rw_vec=rw_vec, rw_w2pad=rw_w2pad, rw_a2pad=rw_a2pad, rw_g2=rw_g2b,
                 hg_lb=lb_all[l], br_vec=br_vec,
                 w_br_rwkv=w_br_rwkv[l].astype(BF16), w_br_hgrn=w_br_hgrn[l].astype(BF16),
                 w_out=w_out[l].astype(BF16),
                 moe_router=jnp.pad(moe_router[l], ((0, 0), (0, LANES - N_EXPERTS))).astype(BF16),
                 moe_w_gate=cast_experts_bf16(moe_w_gate[l]), moe_w_up=cast_experts_bf16(moe_w_up[l]),
                 moe_w_down=cast_experts_bf16(moe_w_down[l]))
        cond = jnp.concatenate([c_ctx[None, :], c], axis=0)
        rows = -(-(1 + bs) // BF16_SUBLANES) * BF16_SUBLANES
        cond = jnp.pad(jax.nn.silu(cond), ((0, rows - 1 - bs), (0, 0)))
        mod = matmul(cond, ada_w[l], tm=rows, tn=ADA_COLS)[:1 + bs] + ada_b[l]
        mod = mod.reshape(1 + bs, 1, N_MOD * D)
        zero_rw = jnp.zeros((bp, 2, RW_HEADS, RW_HEAD, RW_HEAD), F32)
        zero_hg = jnp.zeros((bp, 2, HG_HEADS, HG_DK, HG_DV), F32)
        final_g = final_norm_g if l == depth - 1 else None
        xp, s_rw, s_hg = _layer(xp, mod[:1], zero_rw, zero_hg, xp.shape[1], final_g, p)
        new_rw.append(s_rw)
        new_hg.append(s_hg)
        xs, _, _ = _layer(xs, mod[1:], state_rwkv[:, l].astype(F32), state_hgrn[:, l].astype(F32), GRID_W,
                          final_g, p)
    return (xp, xs, jnp.stack(new_rw, axis=1), jnp.stack(new_hg, axis=1))
```

```python
import functools

import jax
import jax.numpy as jnp
from jax import lax
from jax.experimental import pallas as pl
from jax.experimental.pallas import tpu as pltpu

F32 = jnp.float32
BF16 = jnp.bfloat16

D_MODEL = 2048
GRID_W = 64
RW_HEADS = 16
RW_HEAD = 64
RW_DIM = RW_HEADS * RW_HEAD
DECAY_LORA = 64
AAA_LORA = 64
GATE_LORA = 128
RW_LORA_COLS = 2 * DECAY_LORA + 2 * AAA_LORA + GATE_LORA
RW_LN_EPS = 64e-5
HG_HEADS = 8
HG_DK = 128
HG_DV = 128
HG_KDIM = HG_HEADS * HG_DK
HG_VDIM = HG_HEADS * HG_DV
N_EXPERTS = 16
EXPERT_FF = 1024
CAPACITY_FACTOR = 2
NORM_EPS = 1e-6
N_MOD = 6
RW_COLS = 3 * RW_DIM + RW_LORA_COLS
HG_COLS = 3 * HG_KDIM + 2 * HG_VDIM
GATE_COLS = 2 * D_MODEL
IN_COLS = RW_COLS + HG_COLS + GATE_COLS

LANES = 128
BF16_SUBLANES = 16
V7X_VMEM_LIMIT_BYTES = 52 * 1024 * 1024

NORM_ROWS = 512
PROJ_ROWS = 1024
BRANCH_ROWS = 256
OUT_ROWS = 512
EXPERT_ROWS = 512
ADA_COLS = 1024

V7X_MXU_WIDTH = 256
PROJ_TN = 5 * V7X_MXU_WIDTH
PROJ_COLS = -(-IN_COLS // PROJ_TN) * PROJ_TN
PROJ_PLAIN_TILES = (GATE_COLS + HG_COLS) // PROJ_TN
COL_GATE_R, COL_GATE_H = 0, 1
COL_HG_Q, COL_HG_FF, COL_HG_FB, COL_HG_I, COL_HG_G = (GATE_COLS // HG_KDIM + j for j in range(5))
COL_RW_R, COL_RW_K, COL_RW_V = ((GATE_COLS + HG_COLS) // RW_DIM + j for j in range(3))
COL_RW_LORA = (GATE_COLS + HG_COLS + 3 * RW_DIM) // RW_LORA_COLS

RW_CHUNK = 64
RW_PAIRS = RW_DIM // LANES
HG_L = 128
HG_FIRST_ROUND = 4
COMBINE_ROWS = 512
COMBINE_CHUNK = 512
DECAY_SCALE = 0.6065306597126334
LOG2_E = 1.4426950408889634


def _params(sem):
    return pltpu.CompilerParams(dimension_semantics=sem, vmem_limit_bytes=V7X_VMEM_LIMIT_BYTES)


def _dot(a, b):
    return jnp.dot(a.astype(BF16), b.astype(BF16), preferred_element_type=F32)


def _dot_nt(a, b):
    return lax.dot_general(a.astype(BF16), b.astype(BF16), (((1,), (1,)), ((), ())),
                           preferred_element_type=F32)


def _split_dot(x, w_bf16):
    hi = x.astype(BF16)
    lo = (x - hi.astype(F32)).astype(BF16)
    return (jnp.dot(hi, w_bf16, preferred_element_type=F32)
            + jnp.dot(lo, w_bf16, preferred_element_type=F32))


def _group_sums(x, ones_bd):
    parts = [_split_dot(x[:, p * LANES:(p + 1) * LANES], ones_bd) for p in range(x.shape[1] // LANES)]
    return jnp.concatenate(parts, axis=1)


def _ones_blocks(group):
    ri = lax.broadcasted_iota(jnp.int32, (LANES, LANES), 0)
    ci = lax.broadcasted_iota(jnp.int32, (LANES, LANES), 1)
    return jnp.where((ri // group) == (ci // group), 1.0, 0.0).astype(BF16)


def _sigmoid(x):
    return 1.0 / (1.0 + jnp.exp(-x))


def _cumsum_rows(x, reverse):
    L = x.shape[0]
    row = lax.broadcasted_iota(jnp.int32, x.shape, 0)
    s = 1
    while s < L:
        if reverse:
            x = x + jnp.where(row < L - s, pltpu.roll(x, L - s, 0), 0.0)
        else:
            x = x + jnp.where(row >= s, pltpu.roll(x, s, 0), 0.0)
        s *= 2
    return x


def _interleave(gens, first_round=None):
    results = [None] * len(gens)
    first_round = first_round or [0] * len(gens)
    active = list(enumerate(gens))
    rnd = 0
    while active:
        still = []
        for idx, g in active:
            if rnd < first_round[idx]:
                still.append((idx, g))
                continue
            try:
                next(g)
                still.append((idx, g))
            except StopIteration as e:
                results[idx] = e.value
        active = still
        rnd += 1
    return results


def _mm_body(x_ref, w_ref, o_ref):
    o_ref[...] = _dot(x_ref[...], w_ref[...]).astype(o_ref.dtype)


def matmul(x, w, *, tm, tn, out_dtype=F32):
    M, K = x.shape
    _, N = w.shape
    return pl.pallas_call(
        _mm_body,
        grid=(M // tm, N // tn),
        in_specs=[pl.BlockSpec((tm, K), lambda i, j: (i, 0)),
                  pl.BlockSpec((K, tn), lambda i, j: (0, j))],
        out_specs=pl.BlockSpec((tm, tn), lambda i, j: (i, j)),
        out_shape=jax.ShapeDtypeStruct((M, N), out_dtype),
        compiler_params=_params(("parallel", "parallel")),
        name="matmul",
    )(x, w)


def _cast_body(w_ref, o_ref):
    o_ref[...] = w_ref[...].astype(o_ref.dtype)


def cast_experts_bf16(w):
    E, K, N = w.shape
    return pl.pallas_call(
        _cast_body,
        grid=(E,),
        in_specs=[pl.BlockSpec((1, K, N), lambda e: (e, 0, 0))],
        out_specs=pl.BlockSpec((1, K, N), lambda e: (e, 0, 0)),
        out_shape=jax.ShapeDtypeStruct((E, K, N), BF16),
        compiler_params=_params(("parallel",)),
        name="cast_experts_bf16",
    )(w)


def _norm_mod_body(x_ref, mod_ref, g_ref, o_ref):
    x = x_ref[...]
    y = x * lax.rsqrt(jnp.mean(x * x, axis=-1, keepdims=True) + NORM_EPS) * g_ref[...]
    m = mod_ref[0]
    o_ref[...] = (y * (1.0 + m[:, D_MODEL:2 * D_MODEL]) + m[:, 0:D_MODEL]).astype(o_ref.dtype)


def norm_modulate(x, mod, norm_g, *, rows_per_mod, tm=NORM_ROWS):
    M, D = x.shape
    assert M % tm == 0 and rows_per_mod % tm == 0
    per = rows_per_mod // tm
    return pl.pallas_call(
        _norm_mod_body,
        grid=(M // tm,),
        in_specs=[pl.BlockSpec((tm, D), lambda i: (i, 0)),
                  pl.BlockSpec((1, 1, N_MOD * D), lambda i: (i // per, 0, 0)),
                  pl.BlockSpec((1, D), lambda i: (0, 0))],
        out_specs=pl.BlockSpec((tm, D), lambda i: (i, 0)),
        out_shape=jax.ShapeDtypeStruct((M, D), BF16),
        compiler_params=_params(("parallel",)),
        name="norm_modulate",
    )(x, mod, norm_g)


def _in_proj_body(h_ref, w_ref, mu_ref, o_ref, *, width):
    j = pl.program_id(1)

    @pl.when(j < PROJ_PLAIN_TILES)
    def _():
        o_ref[...] = jnp.dot(h_ref[...], w_ref[...], preferred_element_type=F32)

    @pl.when(j >= PROJ_PLAIN_TILES)
    def _():
        acc = jnp.dot(h_ref[...], w_ref[...], preferred_element_type=F32)
        tm = acc.shape[0]
        pos = lax.broadcasted_iota(jnp.int32, acc.shape, 0) & (width - 1)
        prev = jnp.where(pos == 0, 0.0, pltpu.roll(acc, 1, 0))
        nxt = jnp.where(pos == width - 1, 0.0, pltpu.roll(acc, tm - 1, 0))
        mu = mu_ref[...]
        o_ref[...] = acc + mu[0:1, :] * (prev - acc) + mu[1:2, :] * (nxt - acc)


def in_proj(h, w_perm, mu_perm, *, width, tm=PROJ_ROWS):
    M, D = h.shape
    assert M % tm == 0 and tm % width == 0
    return pl.pallas_call(
        functools.partial(_in_proj_body, width=width),
        grid=(M // tm, PROJ_COLS // PROJ_TN),
        in_specs=[pl.BlockSpec((tm, D), lambda i, j: (i, 0)),
                  pl.BlockSpec((D, PROJ_TN), lambda i, j: (0, j)),
                  pl.BlockSpec((2, PROJ_TN), lambda i, j: (0, j))],
        out_specs=pl.BlockSpec((tm, PROJ_TN), lambda i, j: (i, j)),
        out_shape=jax.ShapeDtypeStruct((M, PROJ_COLS), F32),
        compiler_params=_params(("parallel", "parallel")),
        name="in_proj",
    )(h, w_perm, mu_perm)


def _block_diag(x, lane_lo):
    return jnp.concatenate([jnp.where(lane_lo, x, 0.0), jnp.where(lane_lo, 0.0, x)], axis=0)


def _rwkv_masks(reverse):
    L = RW_CHUNK
    n2 = 2 * L
    ri = lax.broadcasted_iota(jnp.int32, (n2, n2), 0)
    ci = lax.broadcasted_iota(jnp.int32, (n2, n2), 1)
    rt = ri & (L - 1)
    ct = ci & (L - 1)
    blocks = []
    rb, cb = rt, ct
    s = 1
    while s < L:
        rb = rb >> 1
        cb = cb >> 1
        blocks.append(rb == cb)
        s *= 2
    couplings = [jnp.logical_and(blocks[j + 1], jnp.logical_not(blocks[j])) for j in range(len(blocks) - 1)]
    return dict(
        lane_lo=lax.broadcasted_iota(jnp.int32, (L, LANES), 1) < RW_HEAD,
        eye=ri == ci,
        strict=(rt < ct) if reverse else (rt > ct),
        incl=(rt <= ct) if reverse else (rt >= ct),
        pair_block=blocks[0],
        couplings=couplings)


def _rwkv_pair_chunk(q, lw, kd, v, kk, a, reverse, m):
    L = q.shape[0]
    n2 = 2 * L
    lane_lo = m['lane_lo']
    c = _cumsum_rows(lw, reverse)
    c_last = c[0:1, :] if reverse else c[L - 1:L, :]
    g_inc = jnp.exp(c)
    g_exc = jnp.exp(c - lw)
    g_inv = jnp.exp(-c)
    g_rem = jnp.exp(c_last - c)
    beta = kk * a
    ab = _block_diag(-kk * g_exc, lane_lo)
    qb = _block_diag(q * g_inc, lane_lo)
    bt = _block_diag(beta * g_inv, lane_lo)
    kt = _block_diag(kd * g_inv, lane_lo)
    vb = _block_diag(v, lane_lo)
    bh = _block_diag(beta * g_rem, lane_lo)
    kh = _block_diag(kd * g_rem, lane_lo)

    aa = _dot_nt(jnp.concatenate([ab, qb], axis=0), jnp.concatenate([bt, kt], axis=0))
    yield
    a_ab = jnp.where(m['strict'], aa[:n2, :n2], 0.0)
    a_ak = jnp.where(m['strict'], aa[:n2, n2:], 0.0)
    a_qb = jnp.where(m['incl'], aa[n2:, :n2], 0.0)
    a_qk = jnp.where(m['incl'], aa[n2:, n2:], 0.0)

    P = jnp.where(m['eye'], 1.0, 0.0) + jnp.where(m['pair_block'], a_ab, 0.0)
    av = _dot(a_ak, vb)
    for coupling_mask in m['couplings']:
        pc = _dot(P, jnp.where(coupling_mask, a_ab, 0.0))
        yield
        P = P + _dot(pc, P)
        yield

    wu_u0 = _dot(P, jnp.concatenate([ab, av], axis=1))
    yield
    rhs = jnp.concatenate([wu_u0, jnp.concatenate([jnp.zeros_like(vb), vb], axis=1)], axis=0)
    qy = _dot(jnp.concatenate([a_qb, a_qk], axis=1), rhs)
    st = _dot(jnp.concatenate([bh, kh], axis=0).T, rhs)
    yield
    q_eff = qb + qy[:, :LANES]
    g_last = jnp.exp(jnp.broadcast_to(c_last, (n2, LANES)))
    m_mat = jnp.where(m['eye'], g_last, 0.0) + st[:, :LANES]
    return jnp.concatenate([q_eff, m_mat], axis=0), jnp.concatenate([qy[:, LANES:], st[:, LANES:]], axis=0)


def _rwkv_apply_state(lhs, add, H):
    L = RW_CHUNK
    out = _dot(lhs, H) + add
    return out[:L] + out[L:2 * L], out[2 * L:]


def _rwkv_prep(r, k, v, lora, vec, w2pad, a2pad, ones_bd, d, want_other_kd):
    lora_in_w = lora[:, 0:LANES]
    lora_in_a = lora[:, LANES:2 * LANES]
    kk = k * vec[4:5, :]
    ss = _group_sums(kk * kk, ones_bd)
    kk = kk * lax.rsqrt(jnp.maximum(ss, 1e-24))
    u = vec[d:d + 1, :] + jnp.dot(jnp.tanh(lora_in_w).astype(BF16), w2pad[d], preferred_element_type=F32)
    lw = -DECAY_SCALE * _sigmoid(u)
    a = _sigmoid(vec[2 + d:3 + d, :] + jnp.dot(lora_in_a.astype(BF16), a2pad[d], preferred_element_type=F32))
    kd = k * (1.0 + (a - 1.0) * vec[5:6, :])
    out = dict(r=r, k=k, v=v, kk=kk, lw=lw, a=a, kd=kd)
    if want_other_kd:
        o = 1 - d
        a_o = _sigmoid(vec[2 + o:3 + o, :] + jnp.dot(lora_in_a.astype(BF16), a2pad[o], preferred_element_type=F32))
        out['kd_other'] = k * (1.0 + (a_o - 1.0) * vec[5:6, :])
    return out


def _scan_body(rf_ref, kf_ref, vf_ref, lf_ref, rb_ref, kb_ref, vb_ref, lb_ref,
               qf_ref, ff_ref, if_ref, qb_ref, fb_ref, ib_ref,
               vec_ref, w2_ref, a2_ref, g2_ref, hglb_ref, h0_ref, s0_ref,
               yf_ref, yb_ref, bonus_ref, g_ref, of_ref, ob_ref, hfin_ref, sfin_ref, h_scr, s_scr):
    i = pl.program_id(1)
    n = pl.num_programs(1)

    @pl.when(i == 0)
    def _():
        h_scr[...] = h0_ref[0]
        s_scr[...] = s0_ref[0]

    vec = vec_ref[...]
    ones_bd = _ones_blocks(RW_HEAD)
    fw = _rwkv_prep(rf_ref[0], kf_ref[0], vf_ref[0], lf_ref[0], vec, w2_ref, a2_ref, ones_bd, 0, True)
    bw = _rwkv_prep(rb_ref[0], kb_ref[0], vb_ref[0], lb_ref[0], vec, w2_ref, a2_ref, ones_bd, 1, False)

    rk = fw['r'] * vec[6:7, :]
    bonus_ref[0] = _group_sums(rk * (fw['kd'] + fw['kd_other']), ones_bd) * fw['v']
    gd = lf_ref[0][:, 2 * LANES:3 * LANES]
    g_ref[0] = jnp.dot(_sigmoid(gd).astype(BF16), g2_ref[...], preferred_element_type=F32)

    chunk_order = ((0, 1), (1, 0))
    rw_masks = (_rwkv_masks(False), _rwkv_masks(True))
    hg_masks = (_gla_masks(False), _gla_masks(True))
    chains = []
    for d, t in enumerate((fw, bw)):
        for half in chunk_order[d]:
            rows = slice(half * RW_CHUNK, (half + 1) * RW_CHUNK)
            for p in range(RW_PAIRS):
                sl = slice(p * LANES, (p + 1) * LANES)
                chains.append(_rwkv_pair_chunk(t['r'][rows, sl], t['lw'][rows, sl], t['kd'][rows, sl],
                                               t['v'][rows, sl], t['kk'][rows, sl], t['a'][rows, sl], d == 1,
                                               rw_masks[d]))
    n_rw = len(chains)
    for d, (q_ref, f_ref, v_ref) in enumerate(((qf_ref, ff_ref, if_ref), (qb_ref, fb_ref, ib_ref))):
        for h in range(HG_HEADS):
            sl = slice(h * LANES, (h + 1) * LANES)
            chains.append(_gla_head_chunk(q_ref[0, :, sl], f_ref[0, :, sl], v_ref[0, :, sl],
                                          hglb_ref[d:d + 1, sl], s_scr[d, h], d == 1, hg_masks[d]))
    outs = _interleave(chains, [0] * n_rw + [HG_FIRST_ROUND] * (len(chains) - n_rw))

    states = [[h_scr[d, p] for p in range(RW_PAIRS)] for d in range(2)]
    for step in range(2):
        for d, y_ref in enumerate((yf_ref, yb_ref)):
            half = chunk_order[d][step]
            for p in range(RW_PAIRS):
                lhs, add = outs[(d * 2 + step) * RW_PAIRS + p]
                y, states[d][p] = _rwkv_apply_state(lhs, add, states[d][p])
                y_ref[0, half * RW_CHUNK:(half + 1) * RW_CHUNK, p * LANES:(p + 1) * LANES] = y
    for d in range(2):
        for p in range(RW_PAIRS):
            h_scr[d, p] = states[d][p]

    for d, o_ref in enumerate((of_ref, ob_ref)):
        for h in range(HG_HEADS):
            o, st_new = outs[n_rw + d * HG_HEADS + h]
            o_ref[0, :, h * LANES:(h + 1) * LANES] = o
            s_scr[d, h] = st_new

    @pl.when(i == n - 1)
    def _():
        hfin_ref[0] = h_scr[...]
        sfin_ref[0] = s_scr[...]


def recurrent_scans(proj, vec, w2pad, a2pad, g2, hg_lb, h0, s0t):
    B, T, _ = proj.shape
    assert T % HG_L == 0 and HG_L == 2 * RW_CHUNK
    n = T // HG_L
    fwd = lambda c: (lambda b, i: (b, i, c))
    bwd = lambda c: (lambda b, i: (b, n - 1 - i, c))
    wide = lambda fn: pl.BlockSpec((1, HG_L, RW_DIM), fn)
    lora = lambda fn: pl.BlockSpec((1, HG_L, RW_LORA_COLS), fn)
    const = lambda shape: pl.BlockSpec(shape, lambda b, i: (0,) * len(shape))
    rw_st = pl.BlockSpec((1, 2, RW_PAIRS, LANES, LANES), lambda b, i: (b, 0, 0, 0, 0))
    hg_st = pl.BlockSpec((1, 2, HG_HEADS, HG_DV, HG_DK), lambda b, i: (b, 0, 0, 0, 0))
    out_sd = jax.ShapeDtypeStruct((B, T, RW_DIM), F32)
    return pl.pallas_call(
        _scan_body,
        grid=(B, n),
        in_specs=[wide(fwd(COL_RW_R)), wide(fwd(COL_RW_K)), wide(fwd(COL_RW_V)), lora(fwd(COL_RW_LORA)),
                  wide(bwd(COL_RW_R)), wide(bwd(COL_RW_K)), wide(bwd(COL_RW_V)), lora(bwd(COL_RW_LORA)),
                  wide(fwd(COL_HG_Q)), wide(fwd(COL_HG_FF)), wide(fwd(COL_HG_I)),
                  wide(bwd(COL_HG_Q)), wide(bwd(COL_HG_FB)), wide(bwd(COL_HG_I)),
                  const((8, RW_DIM)), const((2, LANES, RW_DIM)), const((2, LANES, RW_DIM)),
                  const((LANES, RW_DIM)), const((2, HG_KDIM)), rw_st, hg_st],
        out_specs=[wide(fwd(0)), wide(bwd(0)), wide(fwd(0)), wide(fwd(0)), wide(fwd(0)), wide(bwd(0)),
                   rw_st, hg_st],
        out_shape=[out_sd] * 6 + [jax.ShapeDtypeStruct((B, 2, RW_PAIRS, LANES, LANES), F32),
                                  jax.ShapeDtypeStruct((B, 2, HG_HEADS, HG_DV, HG_DK), F32)],
        scratch_shapes=[pltpu.VMEM((2, RW_PAIRS, LANES, LANES), F32),
                        pltpu.VMEM((2, HG_HEADS, HG_DV, HG_DK), F32)],
        compiler_params=_params(("parallel", "arbitrary")),
        name="recurrent_scans",
    )(*([proj] * 14), vec, w2pad, a2pad, g2, hg_lb, h0, s0t)


def rwkv_pack_params(p):
    z = jnp.zeros((RW_HEAD, RW_DIM), F32)
    w2pad = jnp.stack([jnp.concatenate([p['rw_w2'][0], z], 0), jnp.concatenate([z, p['rw_w2'][1]], 0)]).astype(BF16)
    a2pad = jnp.stack([jnp.concatenate([p['rw_a2'][0], z], 0), jnp.concatenate([z, p['rw_a2'][1]], 0)]).astype(BF16)
    vec = jnp.stack([p['rw_w0'][0], p['rw_w0'][1], p['rw_a0'][0], p['rw_a0'][1], p['rw_k_k'], p['rw_k_a'],
                     p['rw_r_k'].reshape(RW_DIM), jnp.zeros((RW_DIM,), F32)])
    return vec, w2pad, a2pad, p['rw_g2'].astype(BF16)


def rwkv_state_to_blockdiag(s):
    B = s.shape[0]
    h = jnp.swapaxes(s, -1, -2).reshape(B, 2, RW_PAIRS, 2, RW_HEAD, RW_HEAD)
    z = jnp.zeros_like(h[:, :, :, 0])
    top = jnp.concatenate([h[:, :, :, 0], z], axis=-1)
    bot = jnp.concatenate([z, h[:, :, :, 1]], axis=-1)
    return jnp.concatenate([top, bot], axis=-2)


def rwkv_blockdiag_to_state(hb):
    B = hb.shape[0]
    h0 = hb[:, :, :, :RW_HEAD, :RW_HEAD]
    h1 = hb[:, :, :, RW_HEAD:, RW_HEAD:]
    h = jnp.stack([h0, h1], axis=3).reshape(B, 2, RW_HEADS, RW_HEAD, RW_HEAD)
    return jnp.swapaxes(h, -1, -2)


def _gla_masks(reverse):
    L = HG_L
    row = lax.broadcasted_iota(jnp.int32, (L, LANES), 0)
    ri = lax.broadcasted_iota(jnp.int32, (L, L), 0)
    ci = lax.broadcasted_iota(jnp.int32, (L, L), 1)
    levels = []
    half = 1
    while half < L:
        size = 2 * half
        upper = (row & half) != 0
        r_up = (ri & half) != 0
        c_up = (ci & half) != 0
        if reverse:
            is_q = jnp.logical_not(upper)
            pair = jnp.logical_and(jnp.logical_not(r_up), c_up)
        else:
            is_q = upper
            pair = jnp.logical_and(r_up, jnp.logical_not(c_up))
        if size < L:
            pair = jnp.logical_and(pair, (ri // size) == (ci // size))
        levels.append(dict(half=half, upper=upper, is_q=is_q, pair=pair,
                           scale=jnp.where(is_q, LOG2_E, -LOG2_E)))
        half = size
    return dict(eye=ri == ci, levels=levels)


def _gla_head_chunk(q_raw, fl, v, lb, St, reverse, m):
    L = q_raw.shape[0]
    q = q_raw * _sigmoid(q_raw)
    f = lb + (1.0 - lb) * _sigmoid(fl)
    k = 1.0 - f
    g = jnp.log(f)
    b = _cumsum_rows(g, reverse)
    b_last = b[0:1, :] if reverse else b[L - 1:L, :]

    att = jnp.where(m['eye'], _dot_nt(q, k), 0.0)
    fk = b
    for lv in m['levels']:
        half, upper, is_q = lv['half'], lv['upper'], lv['is_q']
        if reverse:
            bref = jnp.where(upper, fk, pltpu.roll(fk, L - half, 0))
        else:
            bref = jnp.where(upper, pltpu.roll(fk, half, 0), fk)
        e = jnp.exp2((b - bref) * lv['scale'])
        x = (jnp.where(is_q, q, k) * e).astype(BF16)
        prod = lax.dot_general(x, x, (((1,), (1,)), ((), ())), preferred_element_type=F32)
        yield
        att = jnp.where(lv['pair'], prod, att)
        if reverse:
            fk = jnp.where(upper, pltpu.roll(fk, half, 0), fk)
        else:
            fk = jnp.where(upper, fk, pltpu.roll(fk, L - half, 0))
    o = _dot(att, v) + _dot_nt(q * jnp.exp(b), St)
    kst = k * jnp.exp(b_last - b)
    st_new = St * jnp.exp(b_last) + _dot(v.T, kst)
    yield
    return o, st_new


def _branch_merge_body(yf_ref, yb_ref, bonus_ref, gr_ref, of_ref, ob_ref, gh_ref, pgr_ref, pgh_ref,
                       vec_ref, wr_ref, wh_ref, o_ref):
    vec = vec_ref[...]
    y = yf_ref[...] + yb_ref[...]
    ones_head = _ones_blocks(RW_HEAD)
    mean = _group_sums(y, ones_head) * (1.0 / RW_HEAD)
    dlt = y - mean
    var = _group_sums(dlt * dlt, ones_head) * (1.0 / RW_HEAD)
    yn = dlt * lax.rsqrt(var + RW_LN_EPS) * vec[0:1, :] + vec[1:2, :]
    out_r = (yn + bonus_ref[...]) * gr_ref[...]
    o = of_ref[...] + ob_ref[...]
    ms = _group_sums(o * o, _ones_blocks(LANES)) * (1.0 / HG_DV)
    gh = gh_ref[...]
    out_h = o * lax.rsqrt(ms + NORM_EPS) * vec[2:3, :] * (gh * _sigmoid(gh))
    br = _dot(out_r, wr_ref[...])
    bh = _dot(out_h, wh_ref[...])
    o_ref[...] = (_sigmoid(pgr_ref[...]) * br + _sigmoid(pgh_ref[...]) * bh).astype(o_ref.dtype)


def branch_merge(yf, yb, bonus, gr, of, ob, proj, vec, w_br_r, w_br_h, *, tm=BRANCH_ROWS):
    M = yf.shape[0]
    assert M % tm == 0
    row = lambda w, c: pl.BlockSpec((tm, w), lambda i: (i, c))
    const = lambda shape: pl.BlockSpec(shape, lambda i: (0,) * len(shape), pipeline_mode=pl.Buffered(1))
    return pl.pallas_call(
        _branch_merge_body,
        grid=(M // tm,),
        in_specs=[row(RW_DIM, 0), row(RW_DIM, 0), row(RW_DIM, 0), row(RW_DIM, 0),
                  row(HG_VDIM, 0), row(HG_VDIM, 0), row(HG_VDIM, COL_HG_G),
                  row(D_MODEL, COL_GATE_R), row(D_MODEL, COL_GATE_H),
                  const((8, RW_DIM)), const((RW_DIM, D_MODEL)), const((HG_VDIM, D_MODEL))],
        out_specs=pl.BlockSpec((tm, D_MODEL), lambda i: (i, 0)),
        out_shape=jax.ShapeDtypeStruct((M, D_MODEL), BF16),
        compiler_params=_params(("parallel",)),
        name="branch_merge",
    )(yf, yb, bonus, gr, of, ob, proj, proj, proj, vec, w_br_r, w_br_h)


def _out_proj_body(m_ref, x_ref, mod_ref, g_ref, w_ref, wr_ref, x1_ref, h2_ref, lg_ref):
    mod = mod_ref[0]
    D = D_MODEL
    x1 = x_ref[...] + mod[:, 2 * D:3 * D] * jnp.dot(m_ref[...], w_ref[...], preferred_element_type=F32)
    x1_ref[...] = x1
    y = x1 * lax.rsqrt(jnp.mean(x1 * x1, axis=-1, keepdims=True) + NORM_EPS) * g_ref[...]
    h2 = (y * (1.0 + mod[:, 4 * D:5 * D]) + mod[:, 3 * D:4 * D]).astype(BF16)
    h2_ref[...] = h2
    lg_ref[...] = jnp.dot(h2, wr_ref[...], preferred_element_type=F32)


def out_proj(merged, x, mod, norm_g, w_out, w_router, *, rows_per_mod, tm=OUT_ROWS):
    M, D = x.shape
    assert M % tm == 0 and rows_per_mod % tm == 0
    per = rows_per_mod // tm
    return pl.pallas_call(
        _out_proj_body,
        grid=(M // tm,),
        in_specs=[pl.BlockSpec((tm, D), lambda i: (i, 0)),
                  pl.BlockSpec((tm, D), lambda i: (i, 0)),
                  pl.BlockSpec((1, 1, N_MOD * D), lambda i: (i // per, 0, 0)),
                  pl.BlockSpec((1, D), lambda i: (0, 0)),
                  pl.BlockSpec((D, D), lambda i: (0, 0), pipeline_mode=pl.Buffered(1)),
                  pl.BlockSpec((D, LANES), lambda i: (0, 0), pipeline_mode=pl.Buffered(1))],
        out_specs=[pl.BlockSpec((tm, D), lambda i: (i, 0)),
                   pl.BlockSpec((tm, D), lambda i: (i, 0)),
                   pl.BlockSpec((tm, LANES), lambda i: (i, 0))],
        out_shape=[jax.ShapeDtypeStruct((M, D), F32), jax.ShapeDtypeStruct((M, D), BF16),
                   jax.ShapeDtypeStruct((M, LANES), F32)],
        compiler_params=_params(("parallel",)),
        name="out_proj",
    )(merged, x, mod, norm_g, w_out, w_router)


def _experts_body(x_ref, wg_ref, wu_ref, wd_ref, val_ref, o_ref):
    x = x_ref[0]
    gate = jnp.dot(x, wg_ref[0], preferred_element_type=F32)
    up = jnp.dot(x, wu_ref[0], preferred_element_type=F32)
    hid = (gate * _sigmoid(gate) * up).astype(BF16)
    yo = jnp.dot(hid, wd_ref[0], preferred_element_type=F32)
    scale = jnp.concatenate([val_ref[0]] * (D_MODEL // LANES), axis=1)
    o_ref[0] = (yo * scale).astype(o_ref.dtype)


def moe_experts(xe, w_gate, w_up, w_down, vals, *, tm=EXPERT_ROWS):
    E, M, D = xe.shape
    tm = min(tm, M)
    assert M % tm == 0
    return pl.pallas_call(
        _experts_body,
        grid=(E, M // tm),
        in_specs=[pl.BlockSpec((1, tm, D), lambda e, i: (e, i, 0)),
                  pl.BlockSpec((1, D, EXPERT_FF), lambda e, i: (e, 0, 0)),
                  pl.BlockSpec((1, D, EXPERT_FF), lambda e, i: (e, 0, 0)),
                  pl.BlockSpec((1, EXPERT_FF, D), lambda e, i: (e, 0, 0)),
                  pl.BlockSpec((1, tm, LANES), lambda e, i: (e, i, 0))],
        out_specs=pl.BlockSpec((1, tm, D), lambda e, i: (e, i, 0)),
        out_shape=jax.ShapeDtypeStruct((E, M, D), BF16),
        compiler_params=_params(("parallel", "parallel")),
        name="moe_experts",
    )(xe, w_gate, w_up, w_down, vals)


def _combine_plan(tok_sorted, T, rows, chunk):
    B, R = tok_sorted.shape
    n_tiles, n_chunks = T // rows, R // chunk
    n_pairs = n_tiles + n_chunks
    bounds = jnp.arange(n_tiles + 1, dtype=jnp.int32) * rows
    start = jnp.sum((tok_sorted[:, None, :] < bounds[None, :, None]).astype(jnp.int32), axis=-1)
    first_chunk = jnp.minimum(start[:, :-1] // chunk, n_chunks - 1)
    last_chunk = jnp.clip((start[:, 1:] - 1) // chunk, first_chunk, n_chunks - 1)
    count = last_chunk - first_chunk + 1
    offset = jnp.cumsum(count, axis=-1) - count
    total = offset[:, -1] + count[:, -1]
    pair = jnp.arange(n_pairs, dtype=jnp.int32)
    tile_of = jnp.sum((pair[None, :, None] >= offset[:, None, :]).astype(jnp.int32), axis=-1) - 1
    k_in = pair[None, :] - jnp.take_along_axis(offset, tile_of, axis=1)
    count_t = jnp.take_along_axis(count, tile_of, axis=1)
    chunk_of = jnp.take_along_axis(first_chunk, tile_of, axis=1) + jnp.minimum(k_in, count_t - 1)
    real = pair[None, :] < total[:, None]
    flags = (jnp.logical_and(real, k_in == 0).astype(jnp.int32)
             + 2 * real.astype(jnp.int32)
             + 4 * jnp.logical_and(real, k_in == count_t - 1).astype(jnp.int32))
    return tile_of, chunk_of, flags


def _moe_combine_body(tile_ref, chunk_ref, flag_ref, yo_ref, tok_ref, x1_ref, gt_ref, g_ref, o_ref, acc, *, final_norm):
    b = pl.program_id(0)
    p = pl.program_id(1)
    flags = flag_ref[b, p]
    rows = x1_ref.shape[1]

    @pl.when((flags & 1) != 0)
    def _():
        acc[...] = jnp.zeros(acc.shape, F32)

    @pl.when((flags & 2) != 0)
    def _():
        tok = lax.broadcasted_iota(jnp.int32, (rows, tok_ref.shape[3]), 0) + tile_ref[b, p] * rows
        onehot = jnp.where(tok == tok_ref[0, 0], 1.0, 0.0).astype(BF16)
        acc[...] += jnp.dot(onehot, yo_ref[0], preferred_element_type=F32)

    @pl.when((flags & 4) != 0)
    def _():
        x2 = x1_ref[0] + gt_ref[0] * acc[...]
        if final_norm:
            x2 = x2 * lax.rsqrt(jnp.mean(x2 * x2, axis=-1, keepdims=True) + NORM_EPS) * g_ref[...]
        o_ref[0] = x2


def moe_combine(yo_sorted, tok_sorted, x1, gt2, final_g):
    B, R, D = yo_sorted.shape
    T = x1.shape[1]
    per = B // gt2.shape[0]
    rows, chunk = min(T, COMBINE_ROWS), min(R, COMBINE_CHUNK)
    assert T % rows == 0 and R % chunk == 0
    tile_of, chunk_of, flags = _combine_plan(tok_sorted, T, rows, chunk)
    g = jnp.ones((1, D), F32) if final_g is None else final_g.reshape(1, D)
    grid_spec = pltpu.PrefetchScalarGridSpec(
        num_scalar_prefetch=3,
        grid=(B, tile_of.shape[1]),
        in_specs=[pl.BlockSpec((1, chunk, D), lambda b, p, t, c, f: (b, c[b, p], 0)),
                  pl.BlockSpec((1, 1, 1, chunk), lambda b, p, t, c, f: (b, c[b, p], 0, 0)),
                  pl.BlockSpec((1, rows, D), lambda b, p, t, c, f: (b, t[b, p], 0)),
                  pl.BlockSpec((1, 1, D), lambda b, p, t, c, f: (b // per, 0, 0)),
                  pl.BlockSpec((1, D), lambda b, p, t, c, f: (0, 0))],
        out_specs=pl.BlockSpec((1, rows, D), lambda b, p, t, c, f: (b, t[b, p], 0)),
        scratch_shapes=[pltpu.VMEM((rows, D), F32)])
    return pl.pallas_call(
        functools.partial(_moe_combine_body, final_norm=final_g is not None),
        grid_spec=grid_spec,
        out_shape=jax.ShapeDtypeStruct((B, T, D), F32),
        compiler_params=_params(("parallel", "arbitrary")),
        name="moe_combine",
    )(tile_of, chunk_of, flags, yo_sorted, tok_sorted.reshape(B, R // chunk, 1, chunk), x1, gt2, g)


def _expert_choice_moe(h2, logits, x1, gt2, final_g, p):
    B, T, D = h2.shape
    cap = CAPACITY_FACTOR * T // N_EXPERTS
    aff = jax.nn.softmax(logits[..., :N_EXPERTS], axis=-1)
    vals, idx = lax.top_k(jnp.swapaxes(aff, 1, 2), cap)
    row_of = jnp.arange(B, dtype=idx.dtype)[:, None, None] * T + idx
    xe = h2.reshape(B * T, D)[jnp.swapaxes(row_of, 0, 1).reshape(N_EXPERTS, B * cap)]
    ve = jnp.broadcast_to(jnp.swapaxes(vals, 0, 1).reshape(N_EXPERTS, B * cap, 1), (N_EXPERTS, B * cap, LANES))
    yo = moe_experts(xe, p['moe_w_gate'], p['moe_w_up'], p['moe_w_down'], ve)
    tok = idx.reshape(B, N_EXPERTS * cap)
    order = jnp.argsort(tok, axis=-1)
    tok_sorted = jnp.take_along_axis(tok, order, axis=-1)
    src_row = (order // cap) * (B * cap) + jnp.arange(B, dtype=order.dtype)[:, None] * cap + order % cap
    yo_sorted = yo.reshape(N_EXPERTS * B * cap, D)[src_row]
    return moe_combine(yo_sorted, tok_sorted, x1, gt2, final_g)


def _layer(x, mod, s_rw, s_hg, width, final_g, p):
    B, T, D = x.shape
    rows_per_mod = (B * T) // mod.shape[0]
    xf = x.reshape(B * T, D)
    h = norm_modulate(xf, mod, p['norm1_g'], rows_per_mod=rows_per_mod)
    proj = in_proj(h, p['w_in'], p['mu'], width=width)
    proj3 = proj.reshape(B, T, PROJ_COLS)
    yf, yb, bonus, gr, of, ob, hfin, sfin = recurrent_scans(
        proj3, p['rw_vec'], p['rw_w2pad'], p['rw_a2pad'], p['rw_g2'], p['hg_lb'],
        rwkv_state_to_blockdiag(s_rw), jnp.swapaxes(s_hg, -1, -2))
    flat = lambda u: u.reshape(B * T, u.shape[-1])
    merged = branch_merge(flat(yf), flat(yb), flat(bonus), flat(gr), flat(of), flat(ob), proj,
                          p['br_vec'], p['w_br_rwkv'], p['w_br_hgrn'])
    x1, h2, logits = out_proj(merged, xf, mod, p['norm2_g'], p['w_out'], p['moe_router'],
                              rows_per_mod=rows_per_mod)
    x2 = _expert_choice_moe(h2.reshape(B, T, D), logits.reshape(B, T, LANES), x1.reshape(B, T, D),
                            mod[:, :, 5 * D:6 * D], final_g, p)
    return x2, rwkv_blockdiag_to_state(hfin), jnp.swapaxes(sfin, -1, -2)


def kernel(x_prompt, x_sample, c, state_rwkv, state_hgrn, c_ctx, norm1_g, norm2_g, final_norm_g, ada_w, ada_b, w_in, rw_mu, rw_w0, rw_w2, rw_a0, rw_a2, rw_g2, rw_k_k, rw_k_a, rw_r_k, rw_ln_w, rw_ln_b, hg_lb, hg_norm_g, w_br_rwkv, w_br_hgrn, w_out, moe_router, moe_w_gate, moe_w_up, moe_w_down):
    depth = norm1_g.shape[0]
    D = D_MODEL
    lb_all = jnp.cumsum(jax.nn.softmax(hg_lb.astype(F32), axis=0), axis=0)
    bp = x_prompt.shape[0]
    bs = x_sample.shape[0]
    xp, xs = x_prompt, x_sample
    new_rw, new_hg = [], []
    for l in range(depth):
        wl = w_in[l]
        w_perm = jnp.concatenate([wl[:, RW_COLS + HG_COLS:], wl[:, RW_COLS:RW_COLS + HG_COLS], wl[:, :RW_COLS],
                                  jnp.zeros((D, PROJ_COLS - IN_COLS), F32)], axis=1).astype(BF16)
        mu_perm = jnp.concatenate([jnp.zeros((2, GATE_COLS + HG_COLS), F32), rw_mu[l],
                                   jnp.zeros((2, PROJ_COLS - IN_COLS), F32)], axis=1)
        pr = dict(rw_w0=rw_w0[l], rw_w2=rw_w2[l], rw_a0=rw_a0[l], rw_a2=rw_a2[l], rw_g2=rw_g2[l],
                  rw_k_k=rw_k_k[l], rw_k_a=rw_k_a[l], rw_r_k=rw_r_k[l])
        rw_vec, rw_w2pad, rw_a2pad, rw_g2b = rwkv_pack_params(pr)
        zrow = jnp.zeros((RW_DIM,), F32)
        br_vec = jnp.stack([rw_ln_w[l], rw_ln_b[l], hg_norm_g[l], zrow, zrow, zrow, zrow, zrow])
        p = dict(norm1_g=norm1_g[l][None, :], norm2_g=norm2_g[l][None, :], w_in=w_perm, mu=mu_perm,
                 rw_vec=rw_vec, rw_w2pad=rw_w2pad, rw_a2pad=rw_a2pad, rw_g2=rw_g2b,
                 hg_lb=lb_all[l], br_vec=br_vec,
                 w_br_rwkv=w_br_rwkv[l].astype(BF16), w_br_hgrn=w_br_hgrn[l].astype(BF16),
                 w_out=w_out[l].astype(BF16),
                 moe_router=jnp.pad(moe_router[l], ((0, 0), (0, LANES - N_EXPERTS))).astype(BF16),
                 moe_w_gate=cast_experts_bf16(moe_w_gate[l]), moe_w_up=cast_experts_bf16(moe_w_up[l]),
                 moe_w_down=cast_experts_bf16(moe_w_down[l]))
        cond = jnp.concatenate([c_ctx[None, :], c], axis=0)
        rows = -(-(1 + bs) // BF16_SUBLANES) * BF16_SUBLANES
        cond = jnp.pad(jax.nn.silu(cond), ((0, rows - 1 - bs), (0, 0)))
        mod = matmul(cond, ada_w[l], tm=rows, tn=ADA_COLS)[:1 + bs] + ada_b[l]
        mod = mod.reshape(1 + bs, 1, N_MOD * D)
        zero_rw = jnp.zeros((bp, 2, RW_HEADS, RW_HEAD, RW_HEAD), F32)
        zero_hg = jnp.zeros((bp, 2, HG_HEADS, HG_DK, HG_DV), F32)
        final_g = final_norm_g if l == depth - 1 else None
        xp, s_rw, s_hg = _layer(xp, mod[:1], zero_rw, zero_hg, xp.shape[1], final_g, p)
        new_rw.append(s_rw)
        new_hg.append(s_hg)
        xs, _, _ = _layer(xs, mod[1:], state_rwkv[:, l].astype(F32), state_hgrn[:, l].astype(F32), GRID_W,
                          final_g, p)
    return (xp, xs, jnp.stack(new_rw, axis=1), jnp.stack(new_hg, axis=1))
```

```python
import functools

import jax
import jax.numpy as jnp
from jax import lax
from jax.experimental import pallas as pl
from jax.experimental.pallas import tpu as pltpu

F32 = jnp.float32
BF16 = jnp.bfloat16

D_MODEL = 2048
GRID_W = 64
RW_HEADS = 16
RW_HEAD = 64
RW_DIM = RW_HEADS * RW_HEAD
DECAY_LORA = 64
AAA_LORA = 64
GATE_LORA = 128
RW_LORA_COLS = 2 * DECAY_LORA + 2 * AAA_LORA + GATE_LORA
RW_LN_EPS = 64e-5
HG_HEADS = 8
HG_DK = 128
HG_DV = 128
HG_KDIM = HG_HEADS * HG_DK
HG_VDIM = HG_HEADS * HG_DV
N_EXPERTS = 16
EXPERT_FF = 1024
CAPACITY_FACTOR = 2
NORM_EPS = 1e-6
N_MOD = 6
RW_COLS = 3 * RW_DIM + RW_LORA_COLS
HG_COLS = 3 * HG_KDIM + 2 * HG_VDIM
GATE_COLS = 2 * D_MODEL
IN_COLS = RW_COLS + HG_COLS + GATE_COLS

LANES = 128
BF16_SUBLANES = 16
V7X_VMEM_LIMIT_BYTES = 52 * 1024 * 1024

NORM_ROWS = 512
PROJ_ROWS = 1024
BRANCH_ROWS = 256
OUT_ROWS = 512
EXPERT_ROWS = 512
ADA_COLS = 1024

V7X_MXU_WIDTH = 256
PROJ_TN = 5 * V7X_MXU_WIDTH
PROJ_COLS = -(-IN_COLS // PROJ_TN) * PROJ_TN
PROJ_PLAIN_TILES = (GATE_COLS + HG_COLS) // PROJ_TN
COL_GATE_R, COL_GATE_H = 0, 1
COL_HG_Q, COL_HG_FF, COL_HG_FB, COL_HG_I, COL_HG_G = (GATE_COLS // HG_KDIM + j for j in range(5))
COL_RW_R, COL_RW_K, COL_RW_V = ((GATE_COLS + HG_COLS) // RW_DIM + j for j in range(3))
COL_RW_LORA = (GATE_COLS + HG_COLS + 3 * RW_DIM) // RW_LORA_COLS

RW_CHUNK = 64
RW_PAIRS = RW_DIM // LANES
HG_L = 128
HG_FIRST_ROUND = 4
COMBINE_ROWS = 512
COMBINE_CHUNK = 512
DECAY_SCALE = 0.6065306597126334
LOG2_E = 1.4426950408889634


def _params(sem):
    return pltpu.CompilerParams(dimension_semantics=sem, vmem_limit_bytes=V7X_VMEM_LIMIT_BYTES)


def _dot(a, b):
    return jnp.dot(a.astype(BF16), b.astype(BF16), preferred_element_type=F32)


def _dot_nt(a, b):
    return lax.dot_general(a.astype(BF16), b.astype(BF16), (((1,), (1,)), ((), ())),
                           preferred_element_type=F32)


def _split_dot(x, w_bf16):
    hi = x.astype(BF16)
    lo = (x - hi.astype(F32)).astype(BF16)
    return (jnp.dot(hi, w_bf16, preferred_element_type=F32)
            + jnp.dot(lo, w_bf16, preferred_element_type=F32))


def _group_sums(x, ones_bd):
    parts = [_split_dot(x[:, p * LANES:(p + 1) * LANES], ones_bd) for p in range(x.shape[1] // LANES)]
    return jnp.concatenate(parts, axis=1)


def _ones_blocks(group):
    ri = lax.broadcasted_iota(jnp.int32, (LANES, LANES), 0)
    ci = lax.broadcasted_iota(jnp.int32, (LANES, LANES), 1)
    return jnp.where((ri // group) == (ci // group), 1.0, 0.0).astype(BF16)


def _sigmoid(x):
    return 1.0 / (1.0 + jnp.exp(-x))


def _cumsum_rows(x, reverse):
    L = x.shape[0]
    row = lax.broadcasted_iota(jnp.int32, x.shape, 0)
    s = 1
    while s < L:
        if reverse:
            x = x + jnp.where(row < L - s, pltpu.roll(x, L - s, 0), 0.0)
        else:
            x = x + jnp.where(row >= s, pltpu.roll(x, s, 0), 0.0)
        s *= 2
    return x


def _interleave(gens, first_round=None):
    results = [None] * len(gens)
    first_round = first_round or [0] * len(gens)
    active = list(enumerate(gens))
    rnd = 0
    while active:
        still = []
        for idx, g in active:
            if rnd < first_round[idx]:
                still.append((idx, g))
                continue
            try:
                next(g)
                still.append((idx, g))
            except StopIteration as e:
                results[idx] = e.value
        active = still
        rnd += 1
    return results


def _mm_body(x_ref, w_ref, o_ref):
    o_ref[...] = _dot(x_ref[...], w_ref[...]).astype(o_ref.dtype)


def matmul(x, w, *, tm, tn, out_dtype=F32):
    M, K = x.shape
    _, N = w.shape
    return pl.pallas_call(
        _mm_body,
        grid=(M // tm, N // tn),
        in_specs=[pl.BlockSpec((tm, K), lambda i, j: (i, 0)),
                  pl.BlockSpec((K, tn), lambda i, j: (0, j))],
        out_specs=pl.BlockSpec((tm, tn), lambda i, j: (i, j)),
        out_shape=jax.ShapeDtypeStruct((M, N), out_dtype),
        compiler_params=_params(("parallel", "parallel")),
        name="matmul",
    )(x, w)


def _cast_body(w_ref, o_ref):
    o_ref[...] = w_ref[...].astype(o_ref.dtype)


def cast_experts_bf16(w):
    E, K, N = w.shape
    return pl.pallas_call(
        _cast_body,
        grid=(E,),
        in_specs=[pl.BlockSpec((1, K, N), lambda e: (e, 0, 0))],
        out_specs=pl.BlockSpec((1, K, N), lambda e: (e, 0, 0)),
        out_shape=jax.ShapeDtypeStruct((E, K, N), BF16),
        compiler_params=_params(("parallel",)),
        name="cast_experts_bf16",
    )(w)


def _norm_mod_body(x_ref, mod_ref, g_ref, o_ref):
    x = x_ref[...]
    y = x * lax.rsqrt(jnp.mean(x * x, axis=-1, keepdims=True) + NORM_EPS) * g_ref[...]
    m = mod_ref[0]
    o_ref[...] = (y * (1.0 + m[:, D_MODEL:2 * D_MODEL]) + m[:, 0:D_MODEL]).astype(o_ref.dtype)


def norm_modulate(x, mod, norm_g, *, rows_per_mod, tm=NORM_ROWS):
    M, D = x.shape
    assert M % tm == 0 and rows_per_mod % tm == 0
    per = rows_per_mod // tm
    return pl.pallas_call(
        _norm_mod_body,
        grid=(M // tm,),
        in_specs=[pl.BlockSpec((tm, D), lambda i: (i, 0)),
                  pl.BlockSpec((1, 1, N_MOD * D), lambda i: (i // per, 0, 0)),
                  pl.BlockSpec((1, D), lambda i: (0, 0))],
        out_specs=pl.BlockSpec((tm, D), lambda i: (i, 0)),
        out_shape=jax.ShapeDtypeStruct((M, D), BF16),
        compiler_params=_params(("parallel",)),
        name="norm_modulate",
    )(x, mod, norm_g)


def _in_proj_body(h_ref, w_ref, mu_ref, o_ref, *, width):
    j = pl.program_id(1)

    @pl.when(j < PROJ_PLAIN_TILES)
    def _():
        o_ref[...] = jnp.dot(h_ref[...], w_ref[...], preferred_element_type=F32).astype(o_ref.dtype)

    @pl.when(j >= PROJ_PLAIN_TILES)
    def _():
        acc = jnp.dot(h_ref[...], w_ref[...], preferred_element_type=F32)
        tm = acc.shape[0]
        pos = lax.broadcasted_iota(jnp.int32, acc.shape, 0) & (width - 1)
        prev = jnp.where(pos == 0, 0.0, pltpu.roll(acc, 1, 0))
        nxt = jnp.where(pos == width - 1, 0.0, pltpu.roll(acc, tm - 1, 0))
        mu = mu_ref[...]
        o_ref[...] = (acc + mu[0:1, :] * (prev - acc) + mu[1:2, :] * (nxt - acc)).astype(o_ref.dtype)


def in_proj(h, w_perm, mu_perm, *, width, tm=PROJ_ROWS):
    M, D = h.shape
    assert M % tm == 0 and tm % width == 0
    return pl.pallas_call(
        functools.partial(_in_proj_body, width=width),
        grid=(M // tm, PROJ_COLS // PROJ_TN),
        in_specs=[pl.BlockSpec((tm, D), lambda i, j: (i, 0)),
                  pl.BlockSpec((D, PROJ_TN), lambda i, j: (0, j)),
                  pl.BlockSpec((2, PROJ_TN), lambda i, j: (0, j))],
        out_specs=pl.BlockSpec((tm, PROJ_TN), lambda i, j: (i, j)),
        out_shape=jax.ShapeDtypeStruct((M, PROJ_COLS), BF16),
        compiler_params=_params(("parallel", "parallel")),
        name="in_proj",
    )(h, w_perm, mu_perm)


def _block_diag(x, lane_lo):
    return jnp.concatenate([jnp.where(lane_lo, x, 0.0), jnp.where(lane_lo, 0.0, x)], axis=0)


def _rwkv_masks(reverse):
    L = RW_CHUNK
    n2 = 2 * L
    ri = lax.broadcasted_iota(jnp.int32, (n2, n2), 0)
    ci = lax.broadcasted_iota(jnp.int32, (n2, n2), 1)
    rt = ri & (L - 1)
    ct = ci & (L - 1)
    blocks = []
    rb, cb = rt, ct
    s = 1
    while s < L:
        rb = rb >> 1
        cb = cb >> 1
        blocks.append(rb == cb)
        s *= 2
    couplings = [jnp.logical_and(blocks[j + 1], jnp.logical_not(blocks[j])) for j in range(len(blocks) - 1)]
    return dict(
        lane_lo=lax.broadcasted_iota(jnp.int32, (L, LANES), 1) < RW_HEAD,
        eye=ri == ci,
        strict=(rt < ct) if reverse else (rt > ct),
        incl=(rt <= ct) if reverse else (rt >= ct),
        pair_block=blocks[0],
        couplings=couplings)


def _rwkv_pair_chunk(q, lw, kd, v, kk, a, reverse, m):
    L = q.shape[0]
    n2 = 2 * L
    lane_lo = m['lane_lo']
    c = _cumsum_rows(lw, reverse)
    c_last = c[0:1, :] if reverse else c[L - 1:L, :]
    g_inc = jnp.exp(c)
    g_exc = jnp.exp(c - lw)
    g_inv = jnp.exp(-c)
    g_rem = jnp.exp(c_last - c)
    beta = kk * a
    ab = _block_diag(-kk * g_exc, lane_lo)
    qb = _block_diag(q * g_inc, lane_lo)
    bt = _block_diag(beta * g_inv, lane_lo)
    kt = _block_diag(kd * g_inv, lane_lo)
    vb = _block_diag(v, lane_lo)
    bh = _block_diag(beta * g_rem, lane_lo)
    kh = _block_diag(kd * g_rem, lane_lo)

    aa = _dot_nt(jnp.concatenate([ab, qb], axis=0), jnp.concatenate([bt, kt], axis=0))
    yield
    a_ab = jnp.where(m['strict'], aa[:n2, :n2], 0.0)
    a_ak = jnp.where(m['strict'], aa[:n2, n2:], 0.0)
    a_qb = jnp.where(m['incl'], aa[n2:, :n2], 0.0)
    a_qk = jnp.where(m['incl'], aa[n2:, n2:], 0.0)

    P = jnp.where(m['eye'], 1.0, 0.0) + jnp.where(m['pair_block'], a_ab, 0.0)
    av = _dot(a_ak, vb)
    for coupling_mask in m['couplings']:
        pc = _dot(P, jnp.where(coupling_mask, a_ab, 0.0))
        yield
        P = P + _dot(pc, P)
        yield

    wu_u0 = _dot(P, jnp.concatenate([ab, av], axis=1))
    yield
    rhs = jnp.concatenate([wu_u0, jnp.concatenate([jnp.zeros_like(vb), vb], axis=1)], axis=0)
    qy = _dot(jnp.concatenate([a_qb, a_qk], axis=1), rhs)
    st = _dot(jnp.concatenate([bh, kh], axis=0).T, rhs)
    yield
    q_eff = qb + qy[:, :LANES]
    g_last = jnp.exp(jnp.broadcast_to(c_last, (n2, LANES)))
    m_mat = jnp.where(m['eye'], g_last, 0.0) + st[:, :LANES]
    return jnp.concatenate([q_eff, m_mat], axis=0), jnp.concatenate([qy[:, LANES:], st[:, LANES:]], axis=0)


def _rwkv_apply_state(lhs, add, H):
    L = RW_CHUNK
    out = _dot(lhs, H) + add
    return out[:L] + out[L:2 * L], out[2 * L:]


def _rwkv_prep(r, k, v, lora, vec, w2pad, a2pad, ones_bd, d, want_other_kd):
    lora_in_w = lora[:, 0:LANES]
    lora_in_a = lora[:, LANES:2 * LANES]
    kk = k * vec[4:5, :]
    ss = _group_sums(kk * kk, ones_bd)
    kk = kk * lax.rsqrt(jnp.maximum(ss, 1e-24))
    u = vec[d:d + 1, :] + jnp.dot(jnp.tanh(lora_in_w).astype(BF16), w2pad[d], preferred_element_type=F32)
    lw = -DECAY_SCALE * _sigmoid(u)
    a = _sigmoid(vec[2 + d:3 + d, :] + jnp.dot(lora_in_a.astype(BF16), a2pad[d], preferred_element_type=F32))
    kd = k * (1.0 + (a - 1.0) * vec[5:6, :])
    out = dict(r=r, k=k, v=v, kk=kk, lw=lw, a=a, kd=kd)
    if want_other_kd:
        o = 1 - d
        a_o = _sigmoid(vec[2 + o:3 + o, :] + jnp.dot(lora_in_a.astype(BF16), a2pad[o], preferred_element_type=F32))
        out['kd_other'] = k * (1.0 + (a_o - 1.0) * vec[5:6, :])
    return out


def _scan_body(rf_ref, kf_ref, vf_ref, lf_ref, rb_ref, kb_ref, vb_ref, lb_ref,
               qf_ref, ff_ref, if_ref, qb_ref, fb_ref, ib_ref,
               vec_ref, w2_ref, a2_ref, g2_ref, hglb_ref, h0_ref, s0_ref,
               yf_ref, yb_ref, bonus_ref, g_ref, of_ref, ob_ref, hfin_ref, sfin_ref, h_scr, s_scr):
    i = pl.program_id(1)
    n = pl.num_programs(1)

    @pl.when(i == 0)
    def _():
        h_scr[...] = h0_ref[0]
        s_scr[...] = s0_ref[0]

    vec = vec_ref[...]
    ones_bd = _ones_blocks(RW_HEAD)
    f32 = lambda ref: ref[0].astype(F32)
    lora_f = f32(lf_ref)
    fw = _rwkv_prep(f32(rf_ref), f32(kf_ref), f32(vf_ref), lora_f, vec, w2_ref, a2_ref, ones_bd, 0, True)
    bw = _rwkv_prep(f32(rb_ref), f32(kb_ref), f32(vb_ref), f32(lb_ref), vec, w2_ref, a2_ref, ones_bd, 1, False)

    rk = fw['r'] * vec[6:7, :]
    bonus_ref[0] = _group_sums(rk * (fw['kd'] + fw['kd_other']), ones_bd) * fw['v']
    gd = lora_f[:, 2 * LANES:3 * LANES]
    g_ref[0] = jnp.dot(_sigmoid(gd).astype(BF16), g2_ref[...], preferred_element_type=F32)

    chunk_order = ((0, 1), (1, 0))
    rw_masks = (_rwkv_masks(False), _rwkv_masks(True))
    hg_masks = (_gla_masks(False), _gla_masks(True))
    chains = []
    for d, t in enumerate((fw, bw)):
        for half in chunk_order[d]:
            rows = slice(half * RW_CHUNK, (half + 1) * RW_CHUNK)
            for p in range(RW_PAIRS):
                sl = slice(p * LANES, (p + 1) * LANES)
                chains.append(_rwkv_pair_chunk(t['r'][rows, sl], t['lw'][rows, sl], t['kd'][rows, sl],
                                               t['v'][rows, sl], t['kk'][rows, sl], t['a'][rows, sl], d == 1,
                                               rw_masks[d]))
    n_rw = len(chains)
    for d, (q_ref, f_ref, v_ref) in enumerate(((qf_ref, ff_ref, if_ref), (qb_ref, fb_ref, ib_ref))):
        for h in range(HG_HEADS):
            sl = slice(h * LANES, (h + 1) * LANES)
            chains.append(_gla_head_chunk(q_ref[0, :, sl].astype(F32), f_ref[0, :, sl].astype(F32),
                                          v_ref[0, :, sl].astype(F32),
                                          hglb_ref[d:d + 1, sl], s_scr[d, h], d == 1, hg_masks[d]))
    outs = _interleave(chains, [0] * n_rw + [HG_FIRST_ROUND] * (len(chains) - n_rw))

    states = [[h_scr[d, p] for p in range(RW_PAIRS)] for d in range(2)]
    for step in range(2):
        for d, y_ref in enumerate((yf_ref, yb_ref)):
            half = chunk_order[d][step]
            for p in range(RW_PAIRS):
                lhs, add = outs[(d * 2 + step) * RW_PAIRS + p]
                y, states[d][p] = _rwkv_apply_state(lhs, add, states[d][p])
                y_ref[0, half * RW_CHUNK:(half + 1) * RW_CHUNK, p * LANES:(p + 1) * LANES] = y
    for d in range(2):
        for p in range(RW_PAIRS):
            h_scr[d, p] = states[d][p]

    for d, o_ref in enumerate((of_ref, ob_ref)):
        for h in range(HG_HEADS):
            o, st_new = outs[n_rw + d * HG_HEADS + h]
            o_ref[0, :, h * LANES:(h + 1) * LANES] = o
            s_scr[d, h] = st_new

    @pl.when(i == n - 1)
    def _():
        hfin_ref[0] = h_scr[...]
        sfin_ref[0] = s_scr[...]


def recurrent_scans(proj, vec, w2pad, a2pad, g2, hg_lb, h0, s0t):
    B, T, _ = proj.shape
    assert T % HG_L == 0 and HG_L == 2 * RW_CHUNK
    n = T // HG_L
    fwd = lambda c: (lambda b, i: (b, i, c))
    bwd = lambda c: (lambda b, i: (b, n - 1 - i, c))
    wide = lambda fn: pl.BlockSpec((1, HG_L, RW_DIM), fn)
    lora = lambda fn: pl.BlockSpec((1, HG_L, RW_LORA_COLS), fn)
    const = lambda shape: pl.BlockSpec(shape, lambda b, i: (0,) * len(shape))
    rw_st = pl.BlockSpec((1, 2, RW_PAIRS, LANES, LANES), lambda b, i: (b, 0, 0, 0, 0))
    hg_st = pl.BlockSpec((1, 2, HG_HEADS, HG_DV, HG_DK), lambda b, i: (b, 0, 0, 0, 0))
    out_sd = jax.ShapeDtypeStruct((B, T, RW_DIM), F32)
    return pl.pallas_call(
        _scan_body,
        grid=(B, n),
        in_specs=[wide(fwd(COL_RW_R)), wide(fwd(COL_RW_K)), wide(fwd(COL_RW_V)), lora(fwd(COL_RW_LORA)),
                  wide(bwd(COL_RW_R)), wide(bwd(COL_RW_K)), wide(bwd(COL_RW_V)), lora(bwd(COL_RW_LORA)),
                  wide(fwd(COL_HG_Q)), wide(fwd(COL_HG_FF)), wide(fwd(COL_HG_I)),
                  wide(bwd(COL_HG_Q)), wide(bwd(COL_HG_FB)), wide(bwd(COL_HG_I)),
                  const((8, RW_DIM)), const((2, LANES, RW_DIM)), const((2, LANES, RW_DIM)),
                  const((LANES, RW_DIM)), const((2, HG_KDIM)), rw_st, hg_st],
        out_specs=[wide(fwd(0)), wide(bwd(0)), wide(fwd(0)), wide(fwd(0)), wide(fwd(0)), wide(bwd(0)),
                   rw_st, hg_st],
        out_shape=[out_sd] * 6 + [jax.ShapeDtypeStruct((B, 2, RW_PAIRS, LANES, LANES), F32),
                                  jax.ShapeDtypeStruct((B, 2, HG_HEADS, HG_DV, HG_DK), F32)],
        scratch_shapes=[pltpu.VMEM((2, RW_PAIRS, LANES, LANES), F32),
                        pltpu.VMEM((2, HG_HEADS, HG_DV, HG_DK), F32)],
        compiler_params=_params(("parallel", "arbitrary")),
        name="recurrent_scans",
    )(*([proj] * 14), vec, w2pad, a2pad, g2, hg_lb, h0, s0t)


def rwkv_pack_params(p):
    z = jnp.zeros((RW_HEAD, RW_DIM), F32)
    w2pad = jnp.stack([jnp.concatenate([p['rw_w2'][0], z], 0), jnp.concatenate([z, p['rw_w2'][1]], 0)]).astype(BF16)
    a2pad = jnp.stack([jnp.concatenate([p['rw_a2'][0], z], 0), jnp.concatenate([z, p['rw_a2'][1]], 0)]).astype(BF16)
    vec = jnp.stack([p['rw_w0'][0], p['rw_w0'][1], p['rw_a0'][0], p['rw_a0'][1], p['rw_k_k'], p['rw_k_a'],
                     p['rw_r_k'].reshape(RW_DIM), jnp.zeros((RW_DIM,), F32)])
    return vec, w2pad, a2pad, p['rw_g2'].astype(BF16)


def rwkv_state_to_blockdiag(s):
    B = s.shape[0]
    h = jnp.swapaxes(s, -1, -2).reshape(B, 2, RW_PAIRS, 2, RW_HEAD, RW_HEAD)
    z = jnp.zeros_like(h[:, :, :, 0])
    top = jnp.concatenate([h[:, :, :, 0], z], axis=-1)
    bot = jnp.concatenate([z, h[:, :, :, 1]], axis=-1)
    return jnp.concatenate([top, bot], axis=-2)


def rwkv_blockdiag_to_state(hb):
    B = hb.shape[0]
    h0 = hb[:, :, :, :RW_HEAD, :RW_HEAD]
    h1 = hb[:, :, :, RW_HEAD:, RW_HEAD:]
    h = jnp.stack([h0, h1], axis=3).reshape(B, 2, RW_HEADS, RW_HEAD, RW_HEAD)
    return jnp.swapaxes(h, -1, -2)


def _gla_masks(reverse):
    L = HG_L
    row = lax.broadcasted_iota(jnp.int32, (L, LANES), 0)
    ri = lax.broadcasted_iota(jnp.int32, (L, L), 0)
    ci = lax.broadcasted_iota(jnp.int32, (L, L), 1)
    levels = []
    half = 1
    while half < L:
        size = 2 * half
        upper = (row & half) != 0
        r_up = (ri & half) != 0
        c_up = (ci & half) != 0
        if reverse:
            is_q = jnp.logical_not(upper)
            pair = jnp.logical_and(jnp.logical_not(r_up), c_up)
        else:
            is_q = upper
            pair = jnp.logical_and(r_up, jnp.logical_not(c_up))
        if size < L:
            pair = jnp.logical_and(pair, (ri // size) == (ci // size))
        levels.append(dict(half=half, upper=upper, is_q=is_q, pair=pair,
                           scale=jnp.where(is_q, LOG2_E, -LOG2_E)))
        half = size
    return dict(eye=ri == ci, levels=levels)


def _gla_head_chunk(q_raw, fl, v, lb, St, reverse, m):
    L = q_raw.shape[0]
    q = q_raw * _sigmoid(q_raw)
    f = lb + (1.0 - lb) * _sigmoid(fl)
    k = 1.0 - f
    g = jnp.log(f)
    b = _cumsum_rows(g, reverse)
    b_last = b[0:1, :] if reverse else b[L - 1:L, :]

    att = jnp.where(m['eye'], _dot_nt(q, k), 0.0)
    fk = b
    for lv in m['levels']:
        half, upper, is_q = lv['half'], lv['upper'], lv['is_q']
        if reverse:
            bref = jnp.where(upper, fk, pltpu.roll(fk, L - half, 0))
        else:
            bref = jnp.where(upper, pltpu.roll(fk, half, 0), fk)
        e = jnp.exp2((b - bref) * lv['scale'])
        x = (jnp.where(is_q, q, k) * e).astype(BF16)
        prod = lax.dot_general(x, x, (((1,), (1,)), ((), ())), preferred_element_type=F32)
        yield
        att = jnp.where(lv['pair'], prod, att)
        if reverse:
            fk = jnp.where(upper, pltpu.roll(fk, half, 0), fk)
        else:
            fk = jnp.where(upper, fk, pltpu.roll(fk, L - half, 0))
    o = _dot(att, v) + _dot_nt(q * jnp.exp(b), St)
    kst = k * jnp.exp(b_last - b)
    st_new = St * jnp.exp(b_last) + _dot(v.T, kst)
    yield
    return o, st_new


def _branch_merge_body(yf_ref, yb_ref, bonus_ref, gr_ref, of_ref, ob_ref, gh_ref, pgr_ref, pgh_ref,
                       vec_ref, wr_ref, wh_ref, o_ref):
    vec = vec_ref[...]
    y = yf_ref[...] + yb_ref[...]
    ones_head = _ones_blocks(RW_HEAD)
    mean = _group_sums(y, ones_head) * (1.0 / RW_HEAD)
    dlt = y - mean
    var = _group_sums(dlt * dlt, ones_head) * (1.0 / RW_HEAD)
    yn = dlt * lax.rsqrt(var + RW_LN_EPS) * vec[0:1, :] + vec[1:2, :]
    out_r = (yn + bonus_ref[...]) * gr_ref[...]
    o = of_ref[...] + ob_ref[...]
    ms = _group_sums(o * o, _ones_blocks(LANES)) * (1.0 / HG_DV)
    gh = gh_ref[...].astype(F32)
    out_h = o * lax.rsqrt(ms + NORM_EPS) * vec[2:3, :] * (gh * _sigmoid(gh))
    br = _dot(out_r, wr_ref[...])
    bh = _dot(out_h, wh_ref[...])
    o_ref[...] = (_sigmoid(pgr_ref[...].astype(F32)) * br
                  + _sigmoid(pgh_ref[...].astype(F32)) * bh).astype(o_ref.dtype)


def branch_merge(yf, yb, bonus, gr, of, ob, proj, vec, w_br_r, w_br_h, *, tm=BRANCH_ROWS):
    M = yf.shape[0]
    assert M % tm == 0
    row = lambda w, c: pl.BlockSpec((tm, w), lambda i: (i, c))
    const = lambda shape: pl.BlockSpec(shape, lambda i: (0,) * len(shape), pipeline_mode=pl.Buffered(1))
    return pl.pallas_call(
        _branch_merge_body,
        grid=(M // tm,),
        in_specs=[row(RW_DIM, 0), row(RW_DIM, 0), row(RW_DIM, 0), row(RW_DIM, 0),
                  row(HG_VDIM, 0), row(HG_VDIM, 0), row(HG_VDIM, COL_HG_G),
                  row(D_MODEL, COL_GATE_R), row(D_MODEL, COL_GATE_H),
                  const((8, RW_DIM)), const((RW_DIM, D_MODEL)), const((HG_VDIM, D_MODEL))],
        out_specs=pl.BlockSpec((tm, D_MODEL), lambda i: (i, 0)),
        out_shape=jax.ShapeDtypeStruct((M, D_MODEL), BF16),
        compiler_params=_params(("parallel",)),
        name="branch_merge",
    )(yf, yb, bonus, gr, of, ob, proj, proj, proj, vec, w_br_r, w_br_h)


def _out_proj_body(m_ref, x_ref, mod_ref, g_ref, w_ref, wr_ref, x1_ref, h2_ref, lg_ref):
    mod = mod_ref[0]
    D = D_MODEL
    x1 = x_ref[...] + mod[:, 2 * D:3 * D] * jnp.dot(m_ref[...], w_ref[...], preferred_element_type=F32)
    x1_ref[...] = x1
    y = x1 * lax.rsqrt(jnp.mean(x1 * x1, axis=-1, keepdims=True) + NORM_EPS) * g_ref[...]
    h2 = (y * (1.0 + mod[:, 4 * D:5 * D]) + mod[:, 3 * D:4 * D]).astype(BF16)
    h2_ref[...] = h2
    lg_ref[...] = jnp.dot(h2, wr_ref[...], preferred_element_type=F32)


def out_proj(merged, x, mod, norm_g, w_out, w_router, *, rows_per_mod, tm=OUT_ROWS):
    M, D = x.shape
    assert M % tm == 0 and rows_per_mod % tm == 0
    per = rows_per_mod // tm
    return pl.pallas_call(
        _out_proj_body,
        grid=(M // tm,),
        in_specs=[pl.BlockSpec((tm, D), lambda i: (i, 0)),
                  pl.BlockSpec((tm, D), lambda i: (i, 0)),
                  pl.BlockSpec((1, 1, N_MOD * D), lambda i: (i // per, 0, 0)),
                  pl.BlockSpec((1, D), lambda i: (0, 0)),
                  pl.BlockSpec((D, D), lambda i: (0, 0), pipeline_mode=pl.Buffered(1)),
                  pl.BlockSpec((D, LANES), lambda i: (0, 0), pipeline_mode=pl.Buffered(1))],
        out_specs=[pl.BlockSpec((tm, D), lambda i: (i, 0)),
                   pl.BlockSpec((tm, D), lambda i: (i, 0)),
                   pl.BlockSpec((tm, LANES), lambda i: (i, 0))],
        out_shape=[jax.ShapeDtypeStruct((M, D), F32), jax.ShapeDtypeStruct((M, D), BF16),
                   jax.ShapeDtypeStruct((M, LANES), F32)],
        compiler_params=_params(("parallel",)),
        name="out_proj",
    )(merged, x, mod, norm_g, w_out, w_router)


def _experts_body(x_ref, wg_ref, wu_ref, wd_ref, val_ref, o_ref):
    x = x_ref[0]
    gate = jnp.dot(x, wg_ref[0], preferred_element_type=F32)
    up = jnp.dot(x, wu_ref[0], preferred_element_type=F32)
    hid = (gate * _sigmoid(gate) * up).astype(BF16)
    yo = jnp.dot(hid, wd_ref[0], preferred_element_type=F32)
    scale = jnp.concatenate([val_ref[0]] * (D_MODEL // LANES), axis=1)
    o_ref[0] = (yo * scale).astype(o_ref.dtype)


def moe_experts(xe, w_gate, w_up, w_down, vals, *, tm=EXPERT_ROWS):
    E, M, D = xe.shape
    tm = min(tm, M)
    assert M % tm == 0
    return pl.pallas_call(
        _experts_body,
        grid=(E, M // tm),
        in_specs=[pl.BlockSpec((1, tm, D), lambda e, i: (e, i, 0)),
                  pl.BlockSpec((1, D, EXPERT_FF), lambda e, i: (e, 0, 0)),
                  pl.BlockSpec((1, D, EXPERT_FF), lambda e, i: (e, 0, 0)),
                  pl.BlockSpec((1, EXPERT_FF, D), lambda e, i: (e, 0, 0)),
                  pl.BlockSpec((1, tm, LANES), lambda e, i: (e, i, 0))],
        out_specs=pl.BlockSpec((1, tm, D), lambda e, i: (e, i, 0)),
        out_shape=jax.ShapeDtypeStruct((E, M, D), BF16),
        compiler_params=_params(("parallel", "parallel")),
        name="moe_experts",
    )(xe, w_gate, w_up, w_down, vals)


def _combine_plan(tok_sorted, T, rows, chunk):
    B, R = tok_sorted.shape
    n_tiles, n_chunks = T // rows, R // chunk
    n_pairs = n_tiles + n_chunks
    bounds = jnp.arange(n_tiles + 1, dtype=jnp.int32) * rows
    start = jnp.sum((tok_sorted[:, None, :] < bounds[None, :, None]).astype(jnp.int32), axis=-1)
    first_chunk = jnp.minimum(start[:, :-1] // chunk, n_chunks - 1)
    last_chunk = jnp.clip((start[:, 1:] - 1) // chunk, first_chunk, n_chunks - 1)
    count = last_chunk - first_chunk + 1
    offset = jnp.cumsum(count, axis=-1) - count
    total = offset[:, -1] + count[:, -1]
    pair = jnp.arange(n_pairs, dtype=jnp.int32)
    tile_of = jnp.sum((pair[None, :, None] >= offset[:, None, :]).astype(jnp.int32), axis=-1) - 1
    k_in = pair[None, :] - jnp.take_along_axis(offset, tile_of, axis=1)
    count_t = jnp.take_along_axis(count, tile_of, axis=1)
    chunk_of = jnp.take_along_axis(first_chunk, tile_of, axis=1) + jnp.minimum(k_in, count_t - 1)
    real = pair[None, :] < total[:, None]
    flags = (jnp.logical_and(real, k_in == 0).astype(jnp.int32)
             + 2 * real.astype(jnp.int32)
             + 4 * jnp.logical_and(real, k_in == count_t - 1).astype(jnp.int32))
    return tile_of, chunk_of, flags


def _moe_combine_body(tile_ref, chunk_ref, flag_ref, yo_ref, tok_ref, x1_ref, gt_ref, g_ref, o_ref, acc, *, final_norm):
    b = pl.program_id(0)
    p = pl.program_id(1)
    flags = flag_ref[b, p]
    rows = x1_ref.shape[1]

    @pl.when((flags & 1) != 0)
    def _():
        acc[...] = jnp.zeros(acc.shape, F32)

    @pl.when((flags & 2) != 0)
    def _():
        tok = lax.broadcasted_iota(jnp.int32, (rows, tok_ref.shape[3]), 0) + tile_ref[b, p] * rows
        onehot = jnp.where(tok == tok_ref[0, 0], 1.0, 0.0).astype(BF16)
        acc[...] += jnp.dot(onehot, yo_ref[0], preferred_element_type=F32)

    @pl.when((flags & 4) != 0)
    def _():
        x2 = x1_ref[0] + gt_ref[0] * acc[...]
        if final_norm:
            x2 = x2 * lax.rsqrt(jnp.mean(x2 * x2, axis=-1, keepdims=True) + NORM_EPS) * g_ref[...]
        o_ref[0] = x2


def moe_combine(yo_sorted, tok_sorted, x1, gt2, final_g):
    B, R, D = yo_sorted.shape
    T = x1.shape[1]
    per = B // gt2.shape[0]
    rows, chunk = min(T, COMBINE_ROWS), min(R, COMBINE_CHUNK)
    assert T % rows == 0 and R % chunk == 0
    tile_of, chunk_of, flags = _combine_plan(tok_sorted, T, rows, chunk)
    g = jnp.ones((1, D), F32) if final_g is None else final_g.reshape(1, D)
    grid_spec = pltpu.PrefetchScalarGridSpec(
        num_scalar_prefetch=3,
        grid=(B, tile_of.shape[1]),
        in_specs=[pl.BlockSpec((1, chunk, D), lambda b, p, t, c, f: (b, c[b, p], 0)),
                  pl.BlockSpec((1, 1, 1, chunk), lambda b, p, t, c, f: (b, c[b, p], 0, 0)),
                  pl.BlockSpec((1, rows, D), lambda b, p, t, c, f: (b, t[b, p], 0)),
                  pl.BlockSpec((1, 1, D), lambda b, p, t, c, f: (b // per, 0, 0)),
                  pl.BlockSpec((1, D), lambda b, p, t, c, f: (0, 0))],
        out_specs=pl.BlockSpec((1, rows, D), lambda b, p, t, c, f: (b, t[b, p], 0)),
        scratch_shapes=[pltpu.VMEM((rows, D), F32)])
    return pl.pallas_call(
        functools.partial(_moe_combine_body, final_norm=final_g is not None),
        grid_spec=grid_spec,
        out_shape=jax.ShapeDtypeStruct((B, T, D), F32),
        compiler_params=_params(("parallel", "arbitrary")),
        name="moe_combine",
    )(tile_of, chunk_of, flags, yo_sorted, tok_sorted.reshape(B, R // chunk, 1, chunk), x1, gt2, g)


def _expert_choice_moe(h2, logits, x1, gt2, final_g, p):
    B, T, D = h2.shape
    cap = CAPACITY_FACTOR * T // N_EXPERTS
    aff = jax.nn.softmax(logits[..., :N_EXPERTS], axis=-1)
    vals, idx = lax.top_k(jnp.swapaxes(aff, 1, 2), cap)
    row_of = jnp.arange(B, dtype=idx.dtype)[:, None, None] * T + idx
    xe = h2.reshape(B * T, D)[jnp.swapaxes(row_of, 0, 1).reshape(N_EXPERTS, B * cap)]
    ve = jnp.broadcast_to(jnp.swapaxes(vals, 0, 1).reshape(N_EXPERTS, B * cap, 1), (N_EXPERTS, B * cap, LANES))
    yo = moe_experts(xe, p['moe_w_gate'], p['moe_w_up'], p['moe_w_down'], ve)
    tok = idx.reshape(B, N_EXPERTS * cap)
    order = jnp.argsort(tok, axis=-1)
    tok_sorted = jnp.take_along_axis(tok, order, axis=-1)
    src_row = (order // cap) * (B * cap) + jnp.arange(B, dtype=order.dtype)[:, None] * cap + order % cap
    yo_sorted = yo.reshape(N_EXPERTS * B * cap, D)[src_row]
    return moe_combine(yo_sorted, tok_sorted, x1, gt2, final_g)


def _layer(x, mod, s_rw, s_hg, width, final_g, p):
    B, T, D = x.shape
    rows_per_mod = (B * T) // mod.shape[0]
    xf = x.reshape(B * T, D)
    h = norm_modulate(xf, mod, p['norm1_g'], rows_per_mod=rows_per_mod)
    proj = in_proj(h, p['w_in'], p['mu'], width=width)
    proj3 = proj.reshape(B, T, PROJ_COLS)
    yf, yb, bonus, gr, of, ob, hfin, sfin = recurrent_scans(
        proj3, p['rw_vec'], p['rw_w2pad'], p['rw_a2pad'], p['rw_g2'], p['hg_lb'],
        rwkv_state_to_blockdiag(s_rw), jnp.swapaxes(s_hg, -1, -2))
    flat = lambda u: u.reshape(B * T, u.shape[-1])
    merged = branch_merge(flat(yf), flat(yb), flat(bonus), flat(gr), flat(of), flat(ob), proj,
                          p['br_vec'], p['w_br_rwkv'], p['w_br_hgrn'])
    x1, h2, logits = out_proj(merged, xf, mod, p['norm2_g'], p['w_out'], p['moe_router'],
                              rows_per_mod=rows_per_mod)
    x2 = _expert_choice_moe(h2.reshape(B, T, D), logits.reshape(B, T, LANES), x1.reshape(B, T, D),
                            mod[:, :, 5 * D:6 * D], final_g, p)
    return x2, rwkv_blockdiag_to_state(hfin), jnp.swapaxes(sfin, -1, -2)


def kernel(x_prompt, x_sample, c, state_rwkv, state_hgrn, c_ctx, norm1_g, norm2_g, final_norm_g, ada_w, ada_b, w_in, rw_mu, rw_w0, rw_w2, rw_a0, rw_a2, rw_g2, rw_k_k, rw_k_a, rw_r_k, rw_ln_w, rw_ln_b, hg_lb, hg_norm_g, w_br_rwkv, w_br_hgrn, w_out, moe_router, moe_w_gate, moe_w_up, moe_w_down):
    depth = norm1_g.shape[0]
    D = D_MODEL
    lb_all = jnp.cumsum(jax.nn.softmax(hg_lb.astype(F32), axis=0), axis=0)
    bp = x_prompt.shape[0]
    bs = x_sample.shape[0]
    xp, xs = x_prompt, x_sample
    new_rw, new_hg = [], []
    for l in range(depth):
        wl = w_in[l]
        w_perm = jnp.concatenate([wl[:, RW_COLS + HG_COLS:], wl[:, RW_COLS:RW_COLS + HG_COLS], wl[:, :RW_COLS],
                                  jnp.zeros((D, PROJ_COLS - IN_COLS), F32)], axis=1).astype(BF16)
        mu_perm = jnp.concatenate([jnp.zeros((2, GATE_COLS + HG_COLS), F32), rw_mu[l],
                                   jnp.zeros((2, PROJ_COLS - IN_COLS), F32)], axis=1)
        pr = dict(rw_w0=rw_w0[l], rw_w2=rw_w2[l], rw_a0=rw_a0[l], rw_a2=rw_a2[l], rw_g2=rw_g2[l],
                  rw_k_k=rw_k_k[l], rw_k_a=rw_k_a[l], rw_r_k=rw_r_k[l])
        rw_vec, rw_w2pad, rw_a2pad, rw_g2b = rwkv_pack_params(pr)
        zrow = jnp.zeros((RW_DIM,), F32)
        br_vec = jnp.stack([rw_ln_w[l], rw_ln_b[l], hg_norm_g[l], zrow, zrow, zrow, zrow, zrow])
        p = dict(norm1_g=norm1_g[l][None, :], norm2_g=norm2_g[l][None, :], w_in=w_perm, mu=mu_perm,
                 rw_vec=rw_vec, rw_w2pad=rw_w2pad, rw_a2pad=rw_a2pad, rw_g2=rw_g2b,
                 hg_lb=lb_all[l], br_vec=br_vec,
                 w_br_rwkv=w_br_rwkv[l].astype(BF16), w_br_hgrn=w_br_hgrn[l].astype(BF16),
                 w_out=w_out[l].astype(BF16),
                 moe_router=jnp.pad(moe_router[l], ((0, 0), (0, LANES - N_EXPERTS))).astype(BF16),
                 moe_w_gate=cast_experts_bf16(moe_w_gate[l]), moe_w_up=cast_experts_bf16(moe_w_up[l]),
                 moe_w_down=cast_experts_bf16(moe_w_down[l]))
        cond = jnp.concatenate([c_ctx[None, :], c], axis=0)
        rows = -(-(1 + bs) // BF16_SUBLANES) * BF16_SUBLANES
        cond = jnp.pad(jax.nn.silu(cond), ((0, rows - 1 - bs), (0, 0)))
        mod = matmul(cond, ada_w[l], tm=rows, tn=ADA_COLS)[:1 + bs] + ada_b[l]
        mod = mod.reshape(1 + bs, 1, N_MOD * D)
        zero_rw = jnp.zeros((bp, 2, RW_HEADS, RW_HEAD, RW_HEAD), F32)
        zero_hg = jnp.zeros((bp, 2, HG_HEADS, HG_DK, HG_DV), F32)
        final_g = final_norm_g if l == depth - 1 else None
        xp, s_rw, s_hg = _layer(xp, mod[:1], zero_rw, zero_hg, xp.shape[1], final_g, p)
        new_rw.append(s_rw)
        new_hg.append(s_hg)
        xs, _, _ = _layer(xs, mod[1:], state_rwkv[:, l].astype(F32), state_hgrn[:, l].astype(F32), GRID_W,
                          final_g, p)
    return (xp, xs, jnp.stack(new_rw, axis=1), jnp.stack(new_hg, axis=1))
```
